```python
import math
import jax, jax.numpy as jnp
from jax import lax
import numpy as np

D_MODEL = 4096
BATCH = 2
SEQ = 8192
DEPTH = 1

HYENA_WIDTH = D_MODEL // 2
DIFF_WIDTH = D_MODEL - HYENA_WIDTH
DIFF_HEAD_DIM = 128
DIFF_HEADS = DIFF_WIDTH // (2 * DIFF_HEAD_DIM)
DIFF_QK_WIDTH = DIFF_HEADS * 2 * DIFF_HEAD_DIM
IN_WIDTH = 3 * HYENA_WIDTH + 2 * DIFF_QK_WIDTH + DIFF_WIDTH
SHORT_CONV = 3
FILTER_EMB_DIM = 33
FILTER_BANDS = (FILTER_EMB_DIM - 1) // 2
FILTER_HIDDEN = 64
FAST_DECAY_PCT = 0.3
SLOW_DECAY_PCT = 1.5
DECAY_TARGET = 1e-2
FFN_HIDDEN = ((-(-8 * D_MODEL // 3)) + 255) // 256 * 256
ROPE_THETA = 10000.0
NORM_EPS = 1e-6
SUBLN_EPS = 1e-5
Q_BLOCK = 128
N_MOD = 6

kernel_name = "hymba_hyena_diffattn_adaln_block"


def rmsnorm(x, g, eps):
    xf = x.astype(jnp.float32)
    y = xf * lax.rsqrt(jnp.mean(xf * xf, axis=-1, keepdims=True) + eps)
    return (y * g.astype(jnp.float32)).astype(x.dtype)


def short_conv(u, w, b):
    L = u.shape[1]
    pad = SHORT_CONV // 2
    up = jnp.pad(u, ((0, 0), (pad, SHORT_CONV - 1 - pad), (0, 0)))
    out = b
    for j in range(SHORT_CONV):
        out = out + up[:, j:j + L] * w[j]
    return out


def hyena_pos_features(L):
    t = jnp.linspace(0.0, 1.0, L, dtype=jnp.float32)[:, None]
    t_idx = jnp.arange(L, dtype=jnp.float32)[:, None]
    w = 2.0 * math.pi * t_idx / L
    f = jnp.linspace(1e-4, FILTER_BANDS - 1, FILTER_BANDS, dtype=jnp.float32)[None, :]
    z = jnp.concatenate([t, jnp.cos(f * w), -jnp.sin(f * w)], axis=-1)
    return z, t


def hyena_filters(z, t, f_w1, f_b1, f_w2, f_b2, f_w3, f_b3, f_w4, f_freq):
    act = lambda h: jnp.sin(f_freq * h)
    h = act(z @ f_w1 + f_b1)
    h = act(h @ f_w2 + f_b2)
    h = act(h @ f_w3 + f_b3)
    h = (h @ f_w4).astype(jnp.float32)
    L = z.shape[0]
    min_decay = math.log(DECAY_TARGET) / SLOW_DECAY_PCT
    max_decay = math.log(DECAY_TARGET) / FAST_DECAY_PCT
    deltas = jnp.linspace(min_decay, max_decay, HYENA_WIDTH, dtype=jnp.float32)
    decay = jnp.exp(-t * jnp.abs(deltas))
    h = h.reshape(L, 2, HYENA_WIDTH) * decay[:, None, :]
    return h[:, 0], h[:, 1]


def two_sided_kernel(h_f, h_b):
    zero = jnp.zeros((1, h_f.shape[1]), h_f.dtype)
    return jnp.concatenate([h_f, zero, h_b[:0:-1]], axis=0)


def fft_long_conv(u, kern):
    L = u.shape[1]
    n = 2 * L
    U = jnp.fft.rfft(u.astype(jnp.float32), n=n, axis=1)
    K = jnp.fft.rfft(kern.astype(jnp.float32), n=n, axis=0)
    return jnp.fft.irfft(U * K[None], n=n, axis=1)[:, :L]


def hyena_mixer(u, conv_w, conv_b, z, t, f_w1, f_b1, f_w2, f_b2, f_w3, f_b3, f_w4,
                f_freq, hyena_bias):
    u = short_conv(u, conv_w, conv_b)
    x0, x1, v = jnp.split(u, 3, axis=-1)
    h_f, h_b = hyena_filters(z, t, f_w1, f_b1, f_w2, f_b2, f_w3, f_b3, f_w4, f_freq)
    kern = two_sided_kernel(h_f, h_b)
    v = v * x1
    y = fft_long_conv(v, kern).astype(v.dtype) + v * hyena_bias
    return y * x0


def rope_cos_sin(positions, dim):
    inv = 1.0 / (ROPE_THETA ** (jnp.arange(0, dim, 2, dtype=jnp.float32) / dim))
    ang = positions.astype(jnp.float32)[..., None] * inv
    ang = jnp.concatenate([ang, ang], axis=-1)
    return jnp.cos(ang), jnp.sin(ang)


def apply_rope(x, cos, sin):
    c = cos[:, :, None, None, :]
    s = sin[:, :, None, None, :]
    xf = x.astype(jnp.float32)
    x1, x2 = jnp.split(xf, 2, axis=-1)
    rot = jnp.concatenate([-x2, x1], axis=-1)
    return (xf * c + rot * s).astype(x.dtype)


def diff_attention(q, k, v, cos, sin, lambda_q1, lambda_k1, lambda_q2, lambda_k2,
                   subln_g, lam_init):
    B, S, _ = q.shape
    q = apply_rope(q.reshape(B, S, DIFF_HEADS, 2, DIFF_HEAD_DIM), cos, sin)
    k = apply_rope(k.reshape(B, S, DIFF_HEADS, 2, DIFF_HEAD_DIM), cos, sin)
    v = v.reshape(B, S, DIFF_HEADS, 2 * DIFF_HEAD_DIM)
    lam = (jnp.exp(jnp.sum(lambda_q1.astype(jnp.float32) * lambda_k1.astype(jnp.float32)))
           - jnp.exp(jnp.sum(lambda_q2.astype(jnp.float32) * lambda_k2.astype(jnp.float32)))
           + lam_init)
    scale = DIFF_HEAD_DIM ** -0.5
    nb = S // Q_BLOCK
    qb = q.reshape(B, nb, Q_BLOCK, DIFF_HEADS, 2, DIFF_HEAD_DIM).transpose(1, 0, 2, 3, 4, 5)

    def block(q_blk):
        s = jnp.einsum('bqhjd,bkhjd->bhjqk', q_blk, k).astype(jnp.float32) * scale
        p = jax.nn.softmax(s, axis=-1)
        a = p[:, :, 0] - lam * p[:, :, 1]
        return jnp.einsum('bhqk,bkhe->bqhe', a.astype(v.dtype), v)

    o = lax.map(block, qb)
    o = o.transpose(1, 0, 2, 3, 4).reshape(B, S, DIFF_HEADS, 2 * DIFF_HEAD_DIM)
    o = rmsnorm(o, subln_g, SUBLN_EPS) * (1.0 - lam_init)
    return o.reshape(B, S, DIFF_WIDTH)


def setup_inputs(seed: int = 0) -> dict:
    key = jax.random.key(seed)
    ks = jax.random.split(key, 32)
    f32 = jnp.float32
    nrm = lambda k, shape, s: jax.random.normal(k, shape, f32) * s
    C = HYENA_WIDTH
    return {
        "x": nrm(ks[0], (BATCH, SEQ, D_MODEL), 1.0),
        "c": nrm(ks[1], (BATCH, D_MODEL), 1.0),
        "positions": (jnp.arange(SEQ, dtype=jnp.int32)[None, :]
                      + jax.random.randint(ks[2], (BATCH, 1), 0, 1024, dtype=jnp.int32)),
        "w_ada": nrm(ks[3], (DEPTH, D_MODEL, N_MOD * D_MODEL), 0.5 * D_MODEL ** -0.5),
        "b_ada": nrm(ks[4], (DEPTH, N_MOD * D_MODEL), 0.1),
        "g_mix": 1.0 + nrm(ks[5], (DEPTH, D_MODEL), 0.02),
        "g_ffn": 1.0 + nrm(ks[6], (DEPTH, D_MODEL), 0.02),
        "w_in": nrm(ks[7], (DEPTH, D_MODEL, IN_WIDTH), D_MODEL ** -0.5),
        "conv_w": nrm(ks[8], (DEPTH, SHORT_CONV, 3 * C), SHORT_CONV ** -0.5),
        "conv_b": nrm(ks[9], (DEPTH, 3 * C), 0.02),
        "f_w1": nrm(ks[10], (DEPTH, FILTER_EMB_DIM, FILTER_HIDDEN), FILTER_EMB_DIM ** -0.5),
        "f_b1": nrm(ks[11], (DEPTH, FILTER_HIDDEN), 0.1),
        "f_w2": nrm(ks[12], (DEPTH, FILTER_HIDDEN, FILTER_HIDDEN), FILTER_HIDDEN ** -0.5),
        "f_b2": nrm(ks[13], (DEPTH, FILTER_HIDDEN), 0.1),
        "f_w3": nrm(ks[14], (DEPTH, FILTER_HIDDEN, FILTER_HIDDEN), FILTER_HIDDEN ** -0.5),
        "f_b3": nrm(ks[15], (DEPTH, FILTER_HIDDEN), 0.1),
        "f_w4": nrm(ks[16], (DEPTH, FILTER_HIDDEN, 2 * C), 0.02 * FILTER_HIDDEN ** -0.5),
        "f_freq": 1.0 + nrm(ks[17], (DEPTH, FILTER_HIDDEN), 0.1),
        "hyena_bias": nrm(ks[18], (DEPTH, C), 1.0),
        "lambda_q1": nrm(ks[19], (DEPTH, DIFF_HEAD_DIM), 0.1),
        "lambda_k1": nrm(ks[20], (DEPTH, DIFF_HEAD_DIM), 0.1),
        "lambda_q2": nrm(ks[21], (DEPTH, DIFF_HEAD_DIM), 0.1),
        "lambda_k2": nrm(ks[22], (DEPTH, DIFF_HEAD_DIM), 0.1),
        "subln_g": 1.0 + nrm(ks[23], (DEPTH, 2 * DIFF_HEAD_DIM), 0.02),
        "w_out": nrm(ks[24], (DEPTH, D_MODEL, D_MODEL), D_MODEL ** -0.5),
        "w_gate": nrm(ks[25], (DEPTH, D_MODEL, FFN_HIDDEN), D_MODEL ** -0.5),
        "w_up": nrm(ks[26], (DEPTH, D_MODEL, FFN_HIDDEN), D_MODEL ** -0.5),
        "w_down": nrm(ks[27], (DEPTH, FFN_HIDDEN, D_MODEL), FFN_HIDDEN ** -0.5),
        "g_final": 1.0 + nrm(ks[28], (D_MODEL,), 0.02),
    }


def reference(x, c, positions, w_ada, b_ada, g_mix, g_ffn, w_in, conv_w, conv_b,
              f_w1, f_b1, f_w2, f_b2, f_w3, f_b3, f_w4, f_freq, hyena_bias,
              lambda_q1, lambda_k1, lambda_q2, lambda_k2, subln_g, w_out,
              w_gate, w_up, w_down, g_final):
    B, L, D = x.shape
    z, t = hyena_pos_features(L)
    cos, sin = rope_cos_sin(positions, DIFF_HEAD_DIM)
    c_act = jax.nn.silu(c)
    split_at = [3 * HYENA_WIDTH, 3 * HYENA_WIDTH + DIFF_QK_WIDTH,
                3 * HYENA_WIDTH + 2 * DIFF_QK_WIDTH]
    for l in range(DEPTH):
        lam_init = 0.8 - 0.6 * math.exp(-0.3 * l)
        mod = c_act @ w_ada[l] + b_ada[l]
        sh1, sc1, gt1, sh2, sc2, gt2 = jnp.split(mod[:, None, :], N_MOD, axis=-1)
        h = rmsnorm(x, g_mix[l], NORM_EPS) * (1.0 + sc1) + sh1
        proj = h @ w_in[l]
        u_hy, q, k, v = jnp.split(proj, split_at, axis=-1)
        y_hy = hyena_mixer(u_hy, conv_w[l], conv_b[l], z, t, f_w1[l], f_b1[l], f_w2[l],
                           f_b2[l], f_w3[l], f_b3[l], f_w4[l], f_freq[l], hyena_bias[l])
        y_da = diff_attention(q, k, v, cos, sin, lambda_q1[l], lambda_k1[l], lambda_q2[l],
                              lambda_k2[l], subln_g[l], lam_init)
        y = jnp.concatenate([y_hy, y_da], axis=-1) @ w_out[l]
        x = x + gt1 * y
        h = rmsnorm(x, g_ffn[l], NORM_EPS) * (1.0 + sc2) + sh2
        f = (jax.nn.silu(h @ w_gate[l]) * (h @ w_up[l])) @ w_down[l]
        x = x + gt2 * f
    return rmsnorm(x, g_final, NORM_EPS)
```

```python
import functools
import math

import numpy as np
import jax
import jax.numpy as jnp
from jax import lax
from jax.experimental import pallas as pl
from jax.experimental.pallas import tpu as pltpu

F32 = jnp.float32
BF16 = jnp.bfloat16

NORM_EPS = 1e-6
SUBLN_EPS = 1e-5
ROPE_THETA = 10000.0
FAST_DECAY_PCT = 0.3
SLOW_DECAY_PCT = 1.5
DECAY_TARGET = 1e-2
N_MOD = 6

V7X_VMEM_LIMIT_BYTES = 56 * 1024 * 1024
LANES = 128


def _cparams(sem):
    return pltpu.CompilerParams(dimension_semantics=sem, vmem_limit_bytes=V7X_VMEM_LIMIT_BYTES)


def _tile(n, pref):
    t = min(n, pref)
    while n % t:
        t -= 1
    return t


def _silu(x):
    return x * (1.0 / (1.0 + jnp.exp(-x)))


def _split_bf16(a):
    hi = a.astype(BF16)
    lo = (a - hi.astype(F32)).astype(BF16)
    return hi, lo


def _dot3(a, b):
    ah, al = _split_bf16(a)
    bh, bl = _split_bf16(b)
    d = functools.partial(jnp.dot, preferred_element_type=F32)
    return d(ah, bh) + (d(ah, bl) + d(al, bh))


def _ada_kernel(c_ref, w_ref, b_ref, o_ref):
    o_ref[...] = _dot3(_silu(c_ref[...]), w_ref[...]) + b_ref[...]


def _ada(c, w, b):
    bsz, d = c.shape
    n = w.shape[1]
    rows = 8
    cp = jnp.zeros((rows, d), F32).at[:bsz].set(c)
    tn = _tile(n, 512)
    out = pl.pallas_call(
        _ada_kernel,
        grid=(n // tn,),
        in_specs=[pl.BlockSpec((rows, d), lambda j: (0, 0)),
                  pl.BlockSpec((d, tn), lambda j: (0, j)),
                  pl.BlockSpec((1, tn), lambda j: (0, j))],
        out_specs=pl.BlockSpec((rows, tn), lambda j: (0, j)),
        out_shape=jax.ShapeDtypeStruct((rows, n), F32),
        compiler_params=_cparams(("parallel",)),
        name="ada",
    )(cp, w, b.reshape(1, n))
    return out[:bsz]


def _norm_mod_kernel(x_ref, g_ref, sc_ref, sh_ref, o_ref, *, eps):
    x = x_ref[0]
    y = x * lax.rsqrt(jnp.mean(x * x, axis=-1, keepdims=True) + eps) * g_ref[...]
    o_ref[0] = (y * (1.0 + sc_ref[0]) + sh_ref[0]).astype(o_ref.dtype)


def _norm_mod(x, g, sc, sh, eps):
    bsz, l, d = x.shape
    tr = _tile(l, 512)
    return pl.pallas_call(
        functools.partial(_norm_mod_kernel, eps=eps),
        grid=(bsz, l // tr),
        in_specs=[pl.BlockSpec((1, tr, d), lambda b, i: (b, i, 0)),
                  pl.BlockSpec((1, d), lambda b, i: (0, 0)),
                  pl.BlockSpec((1, 1, d), lambda b, i: (b, 0, 0)),
                  pl.BlockSpec((1, 1, d), lambda b, i: (b, 0, 0))],
        out_specs=pl.BlockSpec((1, tr, d), lambda b, i: (b, i, 0)),
        out_shape=jax.ShapeDtypeStruct((bsz, l, d), BF16),
        compiler_params=_cparams(("parallel", "parallel")),
        name="norm_mod",
    )(x, g.reshape(1, d), sc, sh)


def _final_norm_kernel(x_ref, g_ref, o_ref, *, eps):
    x = x_ref[0]
    o_ref[0] = x * lax.rsqrt(jnp.mean(x * x, axis=-1, keepdims=True) + eps) * g_ref[...]


def _final_norm(x, g, eps):
    bsz, l, d = x.shape
    tr = _tile(l, 512)
    return pl.pallas_call(
        functools.partial(_final_norm_kernel, eps=eps),
        grid=(bsz, l // tr),
        in_specs=[pl.BlockSpec((1, tr, d), lambda b, i: (b, i, 0)),
                  pl.BlockSpec((1, d), lambda b, i: (0, 0))],
        out_specs=pl.BlockSpec((1, tr, d), lambda b, i: (b, i, 0)),
        out_shape=jax.ShapeDtypeStruct((bsz, l, d), F32),
        compiler_params=_cparams(("parallel", "parallel")),
        name="final_norm",
    )(x, g.reshape(1, d))


def _rope_kernel(pos_ref, inv_ref, sgn_ref, cos_ref, sin_ref):
    ang = pos_ref[0].astype(F32) * inv_ref[...]
    cos_ref[0] = jnp.cos(ang)
    sin_ref[0] = jnp.sin(ang) * sgn_ref[...]


def _rope_tables(positions, dim):
    bsz, l = positions.shape
    half = dim // 2
    inv = 1.0 / (ROPE_THETA ** (np.arange(0, dim, 2, dtype=np.float32) / dim))
    inv = jnp.asarray(np.concatenate([inv, inv]).reshape(1, dim), F32)
    sgn = jnp.asarray(np.concatenate([-np.ones(half), np.ones(half)]).reshape(1, dim), F32)
    tr = _tile(l, 1024)
    shp = jax.ShapeDtypeStruct((bsz, l, dim), F32)
    return pl.pallas_call(
        _rope_kernel,
        grid=(bsz, l // tr),
        in_specs=[pl.BlockSpec((1, tr, 1), lambda b, i: (b, i, 0)),
                  pl.BlockSpec((1, dim), lambda b, i: (0, 0)),
                  pl.BlockSpec((1, dim), lambda b, i: (0, 0))],
        out_specs=[pl.BlockSpec((1, tr, dim), lambda b, i: (b, i, 0))] * 2,
        out_shape=[shp, shp],
        compiler_params=_cparams(("parallel", "parallel")),
        name="rope_tables",
    )(positions.reshape(bsz, l, 1), inv, sgn)


def _proj_t_kernel(w_ref, h_ref, o_ref):
    o_ref[0] = lax.dot_general(w_ref[...], h_ref[0], (((1,), (1,)), ((), ())),
                               preferred_element_type=F32).astype(o_ref.dtype)


def _proj_t(h, w_t, out_dtype):
    bsz, l, d = h.shape
    n = w_t.shape[0]
    tm, tn = _tile(l, 1024), _tile(n, 1024)
    return pl.pallas_call(
        _proj_t_kernel,
        grid=(bsz, l // tm, n // tn),
        in_specs=[pl.BlockSpec((tn, d), lambda b, i, j: (j, 0)),
                  pl.BlockSpec((1, tm, d), lambda b, i, j: (b, i, 0))],
        out_specs=pl.BlockSpec((1, tn, tm), lambda b, i, j: (b, j, i)),
        out_shape=jax.ShapeDtypeStruct((bsz, n, l), out_dtype),
        compiler_params=_cparams(("parallel", "parallel", "arbitrary")),
        name="proj_t",
    )(w_t, h)


def _proj_kernel(h_ref, w_ref, *rest, rope_dim, out_scale):
    acc = jnp.dot(h_ref[0], w_ref[...], preferred_element_type=F32)
    if rope_dim:
        cos_ref, sin_ref, o_ref = rest
        cos, sin = cos_ref[0], sin_ref[0]
        for g in range(acc.shape[1] // rope_dim):
            xg = acc[:, g * rope_dim:(g + 1) * rope_dim]
            yg = xg * cos + pltpu.roll(xg, rope_dim // 2, 1) * sin
            o_ref[0, :, g * rope_dim:(g + 1) * rope_dim] = (yg * out_scale).astype(o_ref.dtype)
    else:
        (o_ref,) = rest
        o_ref[0] = acc.astype(o_ref.dtype)


def _proj(h, w, rope=None, out_scale=1.0):
    bsz, l, d = h.shape
    n = w.shape[1]
    tm, tn = _tile(l, 1024), _tile(n, 1024)
    in_specs = [pl.BlockSpec((1, tm, d), lambda b, i, j: (b, i, 0)),
                pl.BlockSpec((d, tn), lambda b, i, j: (0, j))]
    args = [h, w]
    rope_dim = 0
    if rope is not None:
        rope_dim = rope[0].shape[-1]
        in_specs += [pl.BlockSpec((1, tm, rope_dim), lambda b, i, j: (b, i, 0))] * 2
        args += list(rope)
    return pl.pallas_call(
        functools.partial(_proj_kernel, rope_dim=rope_dim, out_scale=out_scale),
        grid=(bsz, l // tm, n // tn),
        in_specs=in_specs,
        out_specs=pl.BlockSpec((1, tm, tn), lambda b, i, j: (b, i, j)),
        out_shape=jax.ShapeDtypeStruct((bsz, l, n), BF16),
        compiler_params=_cparams(("parallel", "parallel", "arbitrary")),
        name="proj_rope" if rope_dim else "proj",
    )(*args)


def _filt_kernel(z_ref, t_ref, w1_ref, b1_ref, w2_ref, b2_ref, w3_ref, b3_ref, fr_ref,
                 w4_ref, ad_ref, o_ref):
    fr = fr_ref[...]
    h = jnp.sin(fr * (_dot3(w1_ref[...], z_ref[...]) + b1_ref[...]))
    h = jnp.sin(fr * (_dot3(w2_ref[...], h) + b2_ref[...]))
    h = jnp.sin(fr * (_dot3(w3_ref[...], h) + b3_ref[...]))
    t = t_ref[...]
    decay = jnp.exp(-(ad_ref[...] * t[0:1, :]))
    o_ref[...] = _dot3(w4_ref[0], h) * decay * t[1:2, :]


def _hyena_kernel_taps(l, c, f_w1, f_b1, f_w2, f_b2, f_w3, f_b3, f_w4, f_freq):
    n = 2 * l
    emb, hid = f_w1.shape
    bands = (emb - 1) // 2
    m = np.concatenate([np.arange(l), [0], np.arange(l - 1, 0, -1)]).astype(np.float32)
    t = (m / np.float32(l - 1)).astype(np.float32)
    w = (2.0 * math.pi * m / l).astype(np.float32)
    f = np.linspace(1e-4, bands - 1, bands, dtype=np.float32)[:, None]
    z = np.concatenate([t[None], np.cos(f * w[None]), -np.sin(f * w[None])], axis=0)
    embp = -(-emb // 8) * 8
    z = np.concatenate([z, np.zeros((embp - emb, n), np.float32)], axis=0)
    mask = np.ones(n, np.float32)
    mask[l] = 0.0
    tm = jnp.asarray(np.stack([t, mask]), F32)
    min_decay = math.log(DECAY_TARGET) / SLOW_DECAY_PCT
    max_decay = math.log(DECAY_TARGET) / FAST_DECAY_PCT
    ad = jnp.asarray(np.abs(np.linspace(min_decay, max_decay, c, dtype=np.float32)).reshape(c, 1))
    w1t = jnp.zeros((hid, embp), F32).at[:, :emb].set(f_w1.T)
    w4t = f_w4.T.reshape(2, c, hid)
    col = lambda a: a.reshape(hid, 1)
    tn, tc = _tile(l, 2048), _tile(c, 512)
    nt = n // tn
    full = lambda shape: pl.BlockSpec(shape, lambda i, j: (0,) * len(shape))
    return pl.pallas_call(
        _filt_kernel,
        grid=(nt, c // tc),
        in_specs=[pl.BlockSpec((embp, tn), lambda i, j: (0, i)),
                  pl.BlockSpec((2, tn), lambda i, j: (0, i)),
                  full((hid, embp)), full((hid, 1)), full((hid, hid)), full((hid, 1)),
                  full((hid, hid)), full((hid, 1)), full((hid, 1)),
                  pl.BlockSpec((1, tc, hid), lambda i, j: ((2 * i) // nt, j, 0)),
                  pl.BlockSpec((tc, 1), lambda i, j: (j, 0))],
        out_specs=pl.BlockSpec((tc, tn), lambda i, j: (j, i)),
        out_shape=jax.ShapeDtypeStruct((c, n), F32),
        compiler_params=_cparams(("parallel", "arbitrary")),
        name="hyena_filter",
    )(jnp.asarray(z), tm, w1t, col(f_b1), f_w2.T, col(f_b2), f_w3.T, col(f_b3), col(f_freq), w4t, ad)


def _dft_consts(r, tc):
    half = r // 2
    idx = np.arange(r)
    ang = -2.0 * np.pi * np.outer(idx, idx) / r
    fr, fi = np.cos(ang), np.sin(ang)
    angt = -2.0 * np.pi * np.outer(idx, idx) / (r * r)
    fa = np.block([[fr[:, :half], -fi[:, :half]], [fi[:, :half], fr[:, :half]]])
    fk = np.concatenate([fr, fi], axis=0)
    g = np.concatenate([fr, fi], axis=1)
    fin = np.block([[fr[:half], fi[:half]], [-fi[:half], fr[:half]]]) / float(r * r)
    bc = lambda a: jnp.asarray(np.broadcast_to(a[None], (tc,) + a.shape), BF16)
    return (bc(fa), bc(fk), jnp.asarray(g, BF16), jnp.asarray(np.cos(angt), F32),
            jnp.asarray(np.sin(angt), F32), bc(fin))


def _shift_prev(u, lane, row):
    nr, nl = u.shape[-2], u.shape[-1]
    a = pltpu.roll(u, 1, u.ndim - 1)
    b = pltpu.roll(a, 1, u.ndim - 2)
    p = jnp.where(lane == 0, b, a)
    return jnp.where((lane == 0) & (row == 0), 0.0, p)


def _shift_next(u, lane, row):
    nr, nl = u.shape[-2], u.shape[-1]
    a = pltpu.roll(u, nl - 1, u.ndim - 1)
    b = pltpu.roll(a, nr - 1, u.ndim - 2)
    p = jnp.where(lane == nl - 1, b, a)
    return jnp.where((lane == nl - 1) & (row == nr - 1), 0.0, p)


def _lane_stage(y2, g_ref):
    tc, r2, r = y2.shape
    p = jnp.dot(y2.reshape(tc * r2, r).astype(BF16), g_ref[...], preferred_element_type=F32)
    p = p.reshape(tc, r2, r2)
    return p[:, :r, :r], p[:, :r, r:], p[:, r:, :r], p[:, r:, r:]


def _fwd_fft(x2, f1_ref, g_ref, tr, ti):
    r = tr.shape[0]
    a = jnp.einsum("cmk,ckr->cmr", f1_ref[...], x2.astype(BF16), preferred_element_type=F32)
    ar, ai = a[:, :r], a[:, r:]
    br = ar * tr - ai * ti
    bi = ar * ti + ai * tr
    p00, p01, p10, p11 = _lane_stage(jnp.concatenate([br, bi], axis=1), g_ref)
    return p00 - p11, p01 + p10


def _hyena_kernel(x0_ref, x1_ref, v_ref, w0_ref, w1_ref, wv_ref, b0_ref, b1_ref, bv_ref,
                  hb_ref, k_ref, fa_ref, fk_ref, g_ref, tr_ref, ti_ref, fin_ref, o_ref):
    shape = x0_ref.shape[1:]
    lane = lax.broadcasted_iota(jnp.int32, shape, 2)
    row = lax.broadcasted_iota(jnp.int32, shape, 1)
    tr, ti = tr_ref[...], ti_ref[...]

    def sconv(u_ref, w_ref, b_ref, b):
        u = u_ref[b].astype(F32)
        return (b_ref[...] + w_ref[0] * _shift_prev(u, lane, row) + w_ref[1] * u
                + w_ref[2] * _shift_next(u, lane, row))

    nb = x0_ref.shape[0]
    x0 = [sconv(x0_ref, w0_ref, b0_ref, b) for b in range(nb)]
    vx = [sconv(v_ref, wv_ref, bv_ref, b) * sconv(x1_ref, w1_ref, b1_ref, b) for b in range(nb)]

    kr, ki = _fwd_fft(k_ref[...], fk_ref, g_ref, tr, ti)
    for b0 in range(0, nb, 2):
        pair = vx[b0:b0 + 2]
        xi = pair[1] if len(pair) == 2 else jnp.zeros_like(pair[0])
        sr, si = _fwd_fft(jnp.concatenate([pair[0], xi], axis=1), fa_ref, g_ref, tr, ti)
        yr = sr * kr - si * ki
        yi = sr * ki + si * kr
        p00, p01, p10, p11 = _lane_stage(jnp.concatenate([yr, yi], axis=1), g_ref)
        cr, ci = p00 + p11, p10 - p01
        dr = cr * tr + ci * ti
        di = ci * tr - cr * ti
        d2 = jnp.concatenate([dr, di], axis=1).astype(BF16)
        y = jnp.einsum("cmk,ckr->cmr", fin_ref[...], d2, preferred_element_type=F32)
        half = shape[1]
        for j, yb in enumerate((y[:, :half], y[:, half:])[:len(pair)]):
            b = b0 + j
            o_ref[b] = ((yb + vx[b] * hb_ref[...]) * x0[b]).astype(o_ref.dtype)


def _hyena(u_t, kern_t, conv_w, conv_b, hyena_bias, out_dtype):
    bsz, c3, l = u_t.shape
    c = c3 // 3
    r = int(round(math.sqrt(2 * l)))
    assert r * r == 2 * l and r % 2 == 0
    half = r // 2
    tc = _tile(c, 16)
    nc = c // tc
    consts = _dft_consts(r, tc)
    u4 = u_t.reshape(bsz, c3, half, r)
    k3 = kern_t.reshape(c, r, r)
    cw = conv_w.reshape(conv_w.shape[0], c3, 1, 1)
    cb = conv_b.reshape(c3, 1, 1)
    hb = hyena_bias.reshape(c, 1, 1)
    stream = lambda s: pl.BlockSpec((bsz, tc, half, r), lambda j, s=s: (0, j + s * nc, 0, 0))
    wspec = lambda s: pl.BlockSpec((conv_w.shape[0], tc, 1, 1), lambda j, s=s: (0, j + s * nc, 0, 0))
    bspec = lambda s: pl.BlockSpec((tc, 1, 1), lambda j, s=s: (j + s * nc, 0, 0))
    cspec = lambda a: pl.BlockSpec(a.shape, lambda j, nd=a.ndim: (0,) * nd)
    out = pl.pallas_call(
        _hyena_kernel,
        grid=(nc,),
        in_specs=[stream(0), stream(1), stream(2), wspec(0), wspec(1), wspec(2),
                  bspec(0), bspec(1), bspec(2), bspec(0),
                  pl.BlockSpec((tc, r, r), lambda j: (j, 0, 0))] + [cspec(a) for a in consts],
        out_specs=pl.BlockSpec((bsz, tc, half, r), lambda j: (0, j, 0, 0)),
        out_shape=jax.ShapeDtypeStruct((bsz, c, half, r), out_dtype),
        compiler_params=_cparams(("parallel",)),
        name="hyena",
    )(u4, u4, u4, cw, cw, cw, cb, cb, cb, hb, k3, *consts)
    return out.reshape(bsz, c, l)


def _attn_kernel(q_ref, k_ref, v_ref, lq1_ref, lk1_ref, lq2_ref, lk2_ref, g_ref, o_ref,
                 m_ref, l_ref, acc_ref, *, hd, lam_init, eps):
    kv = pl.program_id(3)

    @pl.when(kv == 0)
    def _():
        m_ref[...] = jnp.full(m_ref.shape, -jnp.inf, F32)
        l_ref[...] = jnp.zeros(l_ref.shape, F32)
        acc_ref[...] = jnp.zeros(acc_ref.shape, F32)

    v = v_ref[0]
    for j in range(2):
        qj = q_ref[0, :, j * hd:(j + 1) * hd]
        kj = k_ref[0, :, j * hd:(j + 1) * hd]
        s = lax.dot_general(qj, kj, (((1,), (1,)), ((), ())), preferred_element_type=F32)
        m_prev = m_ref[j]
        m_new = jnp.maximum(m_prev, jnp.max(s, axis=-1, keepdims=True))
        alpha = jnp.exp2(m_prev - m_new)
        p = jnp.exp2(s - m_new)
        l_ref[j] = alpha * l_ref[j] + jnp.sum(p, axis=-1, keepdims=True)
        acc_ref[j] = alpha * acc_ref[j] + jnp.dot(p.astype(v.dtype), v, preferred_element_type=F32)
        m_ref[j] = m_new

    @pl.when(kv == pl.num_programs(3) - 1)
    def _():
        lam = (jnp.exp(jnp.sum(lq1_ref[...] * lk1_ref[...], axis=-1, keepdims=True))
               - jnp.exp(jnp.sum(lq2_ref[...] * lk2_ref[...], axis=-1, keepdims=True)) + lam_init)
        o = acc_ref[0] / l_ref[0] - lam * (acc_ref[1] / l_ref[1])
        o = o * lax.rsqrt(jnp.mean(o * o, axis=-1, keepdims=True) + eps) * g_ref[...]
        o_ref[0] = (o * (1.0 - lam_init)).astype(o_ref.dtype)


def _diff_attention(q, k, v, lq1, lk1, lq2, lk2, subln_g, lam_init):
    bsz, l, width = q.shape
    hd = lq1.shape[-1]
    hw = 2 * hd
    heads = width // hw
    tq, tk = _tile(l, 1024), _tile(l, 1024)
    vec = lambda a: a.reshape(1, -1).astype(F32)
    vspec = lambda n: pl.BlockSpec((1, n), lambda b, h, i, j: (0, 0))
    return pl.pallas_call(
        functools.partial(_attn_kernel, hd=hd, lam_init=lam_init, eps=SUBLN_EPS),
        grid=(bsz, heads, l // tq, l // tk),
        in_specs=[pl.BlockSpec((1, tq, hw), lambda b, h, i, j: (b, i, h)),
                  pl.BlockSpec((1, tk, hw), lambda b, h, i, j: (b, j, h)),
                  pl.BlockSpec((1, tk, hw), lambda b, h, i, j: (b, j, h)),
                  vspec(hd), vspec(hd), vspec(hd), vspec(hd), vspec(hw)],
        out_specs=pl.BlockSpec((1, tq, hw), lambda b, h, i, j: (b, i, h)),
        out_shape=jax.ShapeDtypeStruct((bsz, l, width), BF16),
        scratch_shapes=[pltpu.VMEM((2, tq, 1), F32), pltpu.VMEM((2, tq, 1), F32),
                        pltpu.VMEM((2, tq, hw), F32)],
        compiler_params=_cparams(("parallel", "parallel", "parallel", "arbitrary")),
        name="diff_attn",
    )(q, k, v, vec(lq1), vec(lk1), vec(lq2), vec(lk2), vec(subln_g))


def _outproj_kernel(yh_ref, yd_ref, w1_ref, w2_ref, x_ref, gt_ref, o_ref):
    acc = lax.dot_general(yh_ref[0], w1_ref[...], (((0,), (0,)), ((), ())),
                          preferred_element_type=F32)
    acc = acc + jnp.dot(yd_ref[0], w2_ref[...], preferred_element_type=F32)
    o_ref[0] = x_ref[0] + gt_ref[0] * acc


def _outproj(yh_t, yd, w1, w2, x, gt):
    bsz, l, d = x.shape
    c, kd = yh_t.shape[1], yd.shape[2]
    tm, tn = _tile(l, 1024), _tile(d, 1024)
    return pl.pallas_call(
        _outproj_kernel,
        grid=(bsz, l // tm, d // tn),
        in_specs=[pl.BlockSpec((1, c, tm), lambda b, i, j: (b, 0, i)),
                  pl.BlockSpec((1, tm, kd), lambda b, i, j: (b, i, 0)),
                  pl.BlockSpec((c, tn), lambda b, i, j: (0, j)),
                  pl.BlockSpec((kd, tn), lambda b, i, j: (0, j)),
                  pl.BlockSpec((1, tm, tn), lambda b, i, j: (b, i, j)),
                  pl.BlockSpec((1, 1, tn), lambda b, i, j: (b, 0, j))],
        out_specs=pl.BlockSpec((1, tm, tn), lambda b, i, j: (b, i, j)),
        out_shape=jax.ShapeDtypeStruct((bsz, l, d), F32),
        compiler_params=_cparams(("parallel", "parallel", "arbitrary")),
        name="outproj",
    )(yh_t, yd, w1, w2, x, gt)


def _gateup_kernel(h_ref, wg_ref, wu_ref, o_ref):
    h = h_ref[0]
    g = jnp.dot(h, wg_ref[...], preferred_element_type=F32)
    u = jnp.dot(h, wu_ref[...], preferred_element_type=F32)
    o_ref[0] = (_silu(g) * u).astype(o_ref.dtype)


def _gateup(h, wg, wu):
    bsz, l, d = h.shape
    n = wg.shape[1]
    tm, tn = _tile(l, 1024), _tile(n, 512)
    wspec = pl.BlockSpec((d, tn), lambda b, i, j: (0, j))
    return pl.pallas_call(
        _gateup_kernel,
        grid=(bsz, l // tm, n // tn),
        in_specs=[pl.BlockSpec((1, tm, d), lambda b, i, j: (b, i, 0)), wspec, wspec],
        out_specs=pl.BlockSpec((1, tm, tn), lambda b, i, j: (b, i, j)),
        out_shape=jax.ShapeDtypeStruct((bsz, l, n), BF16),
        compiler_params=_cparams(("parallel", "parallel", "arbitrary")),
        name="gateup",
    )(h, wg, wu)


def _down_kernel(g_ref, w_ref, x_ref, gt_ref, o_ref):
    k = pl.program_id(3)
    part = jnp.dot(g_ref[0], w_ref[...], preferred_element_type=F32)

    @pl.when(k == 0)
    def _():
        o_ref[0] = part

    @pl.when(k > 0)
    def _():
        o_ref[0] += part

    @pl.when(k == pl.num_programs(3) - 1)
    def _():
        o_ref[0] = x_ref[0] + gt_ref[0] * o_ref[0]


def _down(g, w, x, gt):
    bsz, l, d = x.shape
    kdim = g.shape[2]
    tm, tn, tk = _tile(l, 1024), _tile(d, 1024), _tile(kdim, 1024)
    return pl.pallas_call(
        _down_kernel,
        grid=(bsz, l // tm, d // tn, kdim // tk),
        in_specs=[pl.BlockSpec((1, tm, tk), lambda b, i, j, k: (b, i, k)),
                  pl.BlockSpec((tk, tn), lambda b, i, j, k: (k, j)),
                  pl.BlockSpec((1, tm, tn), lambda b, i, j, k: (b, i, j)),
                  pl.BlockSpec((1, 1, tn), lambda b, i, j, k: (b, 0, j))],
        out_specs=pl.BlockSpec((1, tm, tn), lambda b, i, j, k: (b, i, j)),
        out_shape=jax.ShapeDtypeStruct((bsz, l, d), F32),
        compiler_params=_cparams(("parallel", "parallel", "parallel", "arbitrary")),
        name="down",
    )(g, w, x, gt)


def _pad_to(a, axis, mult):
    n = a.shape[axis]
    pad = -n % mult
    if not pad:
        return a
    widths = [(0, 0)] * a.ndim
    widths[axis] = (0, pad)
    return jnp.pad(a, widths)


def kernel(x, c, positions, w_ada, b_ada, g_mix, g_ffn, w_in, conv_w, conv_b, f_w1, f_b1, f_w2, f_b2, f_w3, f_b3, f_w4, f_freq, hyena_bias, lambda_q1, lambda_k1, lambda_q2, lambda_k2, subln_g, w_out, w_gate, w_up, w_down, g_final):
    bsz, l, d = x.shape
    depth = w_ada.shape[0]
    ch = hyena_bias.shape[-1]
    hd = lambda_q1.shape[-1]
    qk = (w_in.shape[-1] - 3 * ch - (d - ch)) // 2
    cos, sin = _rope_tables(positions, hd)
    q_scale = hd ** -0.5 * math.log2(math.e)
    for i in range(depth):
        lam_init = 0.8 - 0.6 * math.exp(-0.3 * i)
        mod = _ada(c, w_ada[i], b_ada[i])
        sh1, sc1, gt1, sh2, sc2, gt2 = [mod[:, None, j * d:(j + 1) * d] for j in range(N_MOD)]
        h = _norm_mod(x, g_mix[i], sc1, sh1, NORM_EPS)
        w = w_in[i].astype(BF16)
        o1, o2, o3 = 3 * ch, 3 * ch + qk, 3 * ch + 2 * qk
        u_t = _proj_t(h, w[:, :o1].T, F32)
        q = _proj(h, w[:, o1:o2], rope=(cos, sin), out_scale=q_scale)
        k = _proj(h, w[:, o2:o3], rope=(cos, sin))
        v = _proj(h, w[:, o3:])
        kern_t = _hyena_kernel_taps(l, ch, f_w1[i], f_b1[i], f_w2[i], f_b2[i], f_w3[i], f_b3[i],
                                    f_w4[i], f_freq[i])
        y_hy = _hyena(u_t, kern_t, conv_w[i], conv_b[i], hyena_bias[i], BF16)
        y_da = _diff_attention(q, k, v, lambda_q1[i], lambda_k1[i], lambda_q2[i], lambda_k2[i],
                               subln_g[i], lam_init)
        wo = w_out[i].astype(BF16)
        x = _outproj(y_hy, y_da, wo[:ch], wo[ch:], x, gt1)
        h = _norm_mod(x, g_ffn[i], sc2, sh2, NORM_EPS)
        wg = _pad_to(w_gate[i].astype(BF16), 1, 1024)
        wu = _pad_to(w_up[i].astype(BF16), 1, 1024)
        wd = _pad_to(w_down[i].astype(BF16), 0, 1024)
        g = _gateup(h, wg, wu)
        x = _down(g, wd, x, gt2)
    return _final_norm(x, g_final, NORM_EPS)
```

```python
import functools
import math

import numpy as np
import jax
import jax.numpy as jnp
from jax import lax
from jax.experimental import pallas as pl
from jax.experimental.pallas import tpu as pltpu

F32 = jnp.float32
BF16 = jnp.bfloat16

NORM_EPS = 1e-6
SUBLN_EPS = 1e-5
ROPE_THETA = 10000.0
FAST_DECAY_PCT = 0.3
SLOW_DECAY_PCT = 1.5
DECAY_TARGET = 1e-2
N_MOD = 6

V7X_VMEM_LIMIT_BYTES = 56 * 1024 * 1024
LANES = 128


def _cparams(sem):
    return pltpu.CompilerParams(dimension_semantics=sem, vmem_limit_bytes=V7X_VMEM_LIMIT_BYTES)


def _tile(n, pref):
    t = min(n, pref)
    while n % t:
        t -= 1
    return t


def _silu(x):
    return x * (1.0 / (1.0 + jnp.exp(-x)))


def _split_bf16(a):
    hi = a.astype(BF16)
    lo = (a - hi.astype(F32)).astype(BF16)
    return hi, lo


def _dot3(a, b):
    ah, al = _split_bf16(a)
    bh, bl = _split_bf16(b)
    d = functools.partial(jnp.dot, preferred_element_type=F32)
    return d(ah, bh) + (d(ah, bl) + d(al, bh))


def _ada_kernel(c_ref, w_ref, b_ref, o_ref):
    o_ref[...] = _dot3(_silu(c_ref[...]), w_ref[...]) + b_ref[...]


def _ada(c, w, b):
    bsz, d = c.shape
    n = w.shape[1]
    rows = 8
    cp = jnp.zeros((rows, d), F32).at[:bsz].set(c)
    tn = _tile(n, 512)
    out = pl.pallas_call(
        _ada_kernel,
        grid=(n // tn,),
        in_specs=[pl.BlockSpec((rows, d), lambda j: (0, 0)),
                  pl.BlockSpec((d, tn), lambda j: (0, j)),
                  pl.BlockSpec((1, tn), lambda j: (0, j))],
        out_specs=pl.BlockSpec((rows, tn), lambda j: (0, j)),
        out_shape=jax.ShapeDtypeStruct((rows, n), F32),
        compiler_params=_cparams(("parallel",)),
        name="ada",
    )(cp, w, b.reshape(1, n))
    return out[:bsz]


def _norm_mod_kernel(x_ref, g_ref, sc_ref, sh_ref, o_ref, *, eps):
    x = x_ref[0]
    y = x * lax.rsqrt(jnp.mean(x * x, axis=-1, keepdims=True) + eps) * g_ref[...]
    o_ref[0] = (y * (1.0 + sc_ref[0]) + sh_ref[0]).astype(o_ref.dtype)


def _norm_mod(x, g, sc, sh, eps):
    bsz, l, d = x.shape
    tr = _tile(l, 512)
    return pl.pallas_call(
        functools.partial(_norm_mod_kernel, eps=eps),
        grid=(bsz, l // tr),
        in_specs=[pl.BlockSpec((1, tr, d), lambda b, i: (b, i, 0)),
                  pl.BlockSpec((1, d), lambda b, i: (0, 0)),
                  pl.BlockSpec((1, 1, d), lambda b, i: (b, 0, 0)),
                  pl.BlockSpec((1, 1, d), lambda b, i: (b, 0, 0))],
        out_specs=pl.BlockSpec((1, tr, d), lambda b, i: (b, i, 0)),
        out_shape=jax.ShapeDtypeStruct((bsz, l, d), BF16),
        compiler_params=_cparams(("parallel", "parallel")),
        name="norm_mod",
    )(x, g.reshape(1, d), sc, sh)


def _final_norm_kernel(x_ref, g_ref, o_ref, *, eps):
    x = x_ref[0]
    o_ref[0] = x * lax.rsqrt(jnp.mean(x * x, axis=-1, keepdims=True) + eps) * g_ref[...]


def _final_norm(x, g, eps):
    bsz, l, d = x.shape
    tr = _tile(l, 512)
    return pl.pallas_call(
        functools.partial(_final_norm_kernel, eps=eps),
        grid=(bsz, l // tr),
        in_specs=[pl.BlockSpec((1, tr, d), lambda b, i: (b, i, 0)),
                  pl.BlockSpec((1, d), lambda b, i: (0, 0))],
        out_specs=pl.BlockSpec((1, tr, d), lambda b, i: (b, i, 0)),
        out_shape=jax.ShapeDtypeStruct((bsz, l, d), F32),
        compiler_params=_cparams(("parallel", "parallel")),
        name="final_norm",
    )(x, g.reshape(1, d))


def _rope_kernel(pos_ref, inv_ref, sgn_ref, cos_ref, sin_ref):
    ang = pos_ref[0].astype(F32) * inv_ref[...]
    cos_ref[0] = jnp.cos(ang)
    sin_ref[0] = jnp.sin(ang) * sgn_ref[...]


def _rope_tables(positions, dim):
    bsz, l = positions.shape
    half = dim // 2
    inv = 1.0 / (ROPE_THETA ** (jnp.arange(0, dim, 2, dtype=F32) / dim))
    inv = jnp.concatenate([inv, inv]).reshape(1, dim)
    sgn = jnp.asarray(np.concatenate([-np.ones(half), np.ones(half)]).reshape(1, dim), F32)
    tr = _tile(l, 1024)
    shp = jax.ShapeDtypeStruct((bsz, l, dim), F32)
    return pl.pallas_call(
        _rope_kernel,
        grid=(bsz, l // tr),
        in_specs=[pl.BlockSpec((1, tr, 1), lambda b, i: (b, i, 0)),
                  pl.BlockSpec((1, dim), lambda b, i: (0, 0)),
                  pl.BlockSpec((1, dim), lambda b, i: (0, 0))],
        out_specs=[pl.BlockSpec((1, tr, dim), lambda b, i: (b, i, 0))] * 2,
        out_shape=[shp, shp],
        compiler_params=_cparams(("parallel", "parallel")),
        name="rope_tables",
    )(positions.reshape(bsz, l, 1), inv, sgn)


def _proj_t_kernel(w_ref, h_ref, o_ref):
    o_ref[0] = lax.dot_general(w_ref[...].astype(BF16), h_ref[0], (((0,), (1,)), ((), ())),
                               preferred_element_type=F32).astype(o_ref.dtype)


def _proj_t(h, w, col0, n, out_dtype):
    bsz, l, d = h.shape
    tm, tn = _tile(l, 1024), _tile(math.gcd(n, col0) if col0 else n, 512)
    j0 = col0 // tn
    return pl.pallas_call(
        _proj_t_kernel,
        grid=(bsz, l // tm, n // tn),
        in_specs=[pl.BlockSpec((d, tn), lambda b, i, j: (0, j + j0)),
                  pl.BlockSpec((1, tm, d), lambda b, i, j: (b, i, 0))],
        out_specs=pl.BlockSpec((1, tn, tm), lambda b, i, j: (b, j, i)),
        out_shape=jax.ShapeDtypeStruct((bsz, n, l), out_dtype),
        compiler_params=_cparams(("parallel", "parallel", "arbitrary")),
        name="proj_t",
    )(w, h)


def _proj_kernel(h_ref, w_ref, *rest, rope_dim, out_scale):
    acc = jnp.dot(h_ref[0], w_ref[...].astype(BF16), preferred_element_type=F32)
    if rope_dim:
        cos_ref, sin_ref, o_ref = rest
        cos, sin = cos_ref[0], sin_ref[0]
        for g in range(acc.shape[1] // rope_dim):
            xg = acc[:, g * rope_dim:(g + 1) * rope_dim]
            yg = xg * cos + pltpu.roll(xg, rope_dim // 2, 1) * sin
            o_ref[0, :, g * rope_dim:(g + 1) * rope_dim] = (yg * out_scale).astype(o_ref.dtype)
    else:
        (o_ref,) = rest
        o_ref[0] = acc.astype(o_ref.dtype)


def _proj(h, w, col0, n, rope=None, out_scale=1.0):
    bsz, l, d = h.shape
    tm, tn = _tile(l, 1024), _tile(math.gcd(n, col0) if col0 else n, 512)
    j0 = col0 // tn
    in_specs = [pl.BlockSpec((1, tm, d), lambda b, i, j: (b, i, 0)),
                pl.BlockSpec((d, tn), lambda b, i, j: (0, j + j0))]
    args = [h, w]
    rope_dim = 0
    if rope is not None:
        rope_dim = rope[0].shape[-1]
        in_specs += [pl.BlockSpec((1, tm, rope_dim), lambda b, i, j: (b, i, 0))] * 2
        args += list(rope)
    return pl.pallas_call(
        functools.partial(_proj_kernel, rope_dim=rope_dim, out_scale=out_scale),
        grid=(bsz, l // tm, n // tn),
        in_specs=in_specs,
        out_specs=pl.BlockSpec((1, tm, tn), lambda b, i, j: (b, i, j)),
        out_shape=jax.ShapeDtypeStruct((bsz, l, n), BF16),
        compiler_params=_cparams(("parallel", "parallel", "arbitrary")),
        name="proj_rope" if rope_dim else "proj",
    )(*args)


def _filt_kernel(z_ref, t_ref, w1_ref, b1_ref, w2_ref, b2_ref, w3_ref, b3_ref, fr_ref,
                 w4_ref, ad_ref, o_ref):
    fr = fr_ref[...]
    h = jnp.sin(fr * (_dot3(w1_ref[...], z_ref[...]) + b1_ref[...]))
    h = jnp.sin(fr * (_dot3(w2_ref[...], h) + b2_ref[...]))
    h = jnp.sin(fr * (_dot3(w3_ref[...], h) + b3_ref[...]))
    t = t_ref[...]
    decay = jnp.exp(-(ad_ref[...] * t[0:1, :]))
    o_ref[...] = _dot3(w4_ref[0], h) * decay * t[1:2, :]


def _hyena_kernel_taps(l, c, f_w1, f_b1, f_w2, f_b2, f_w3, f_b3, f_w4, f_freq):
    n = 2 * l
    emb, hid = f_w1.shape
    bands = (emb - 1) // 2
    t = jnp.linspace(0.0, 1.0, l, dtype=F32)[:, None]
    w = 2.0 * math.pi * jnp.arange(l, dtype=F32)[:, None] / l
    f = jnp.linspace(1e-4, bands - 1, bands, dtype=F32)[None, :]
    z = jnp.concatenate([t, jnp.cos(f * w), -jnp.sin(f * w)], axis=-1)
    cols = lambda a: jnp.concatenate([a, a[:1], a[:0:-1]], axis=0).T
    embp = -(-emb // 8) * 8
    z = jnp.pad(cols(z), ((0, embp - emb), (0, 0)))
    mask = jnp.ones((1, n), F32).at[0, l].set(0.0)
    tm = jnp.concatenate([cols(t), mask], axis=0)
    min_decay = math.log(DECAY_TARGET) / SLOW_DECAY_PCT
    max_decay = math.log(DECAY_TARGET) / FAST_DECAY_PCT
    ad = jnp.abs(jnp.linspace(min_decay, max_decay, c, dtype=F32)).reshape(c, 1)
    w1t = jnp.pad(f_w1.T, ((0, 0), (0, embp - emb)))
    w4t = f_w4.T.reshape(2, c, hid)
    col = lambda a: a.reshape(hid, 1)
    tn = _tile(l, 1024)
    nt = n // tn
    full = lambda shape: pl.BlockSpec(shape, lambda i: (0,) * len(shape))
    return pl.pallas_call(
        _filt_kernel,
        grid=(nt,),
        in_specs=[pl.BlockSpec((embp, tn), lambda i: (0, i)),
                  pl.BlockSpec((2, tn), lambda i: (0, i)),
                  full((hid, embp)), full((hid, 1)), full((hid, hid)), full((hid, 1)),
                  full((hid, hid)), full((hid, 1)), full((hid, 1)),
                  pl.BlockSpec((1, c, hid), lambda i: ((2 * i) // nt, 0, 0)),
                  full((c, 1))],
        out_specs=pl.BlockSpec((c, tn), lambda i: (0, i)),
        out_shape=jax.ShapeDtypeStruct((c, n), F32),
        compiler_params=_cparams(("parallel",)),
        name="hyena_filter",
    )(z, tm, w1t, col(f_b1), f_w2.T, col(f_b2), f_w3.T, col(f_b3), col(f_freq), w4t, ad)


def _dft_consts(r, tc):
    half = r // 2
    idx = np.arange(r)
    ang = -2.0 * np.pi * np.outer(idx, idx) / r
    fr, fi = np.cos(ang), np.sin(ang)
    angt = -2.0 * np.pi * np.outer(idx, idx) / (r * r)
    fa = np.block([[fr[:, :half], -fi[:, :half]], [fi[:, :half], fr[:, :half]]])
    fk = np.concatenate([fr, fi], axis=0)
    g = np.concatenate([fr, fi], axis=1)
    fin = np.block([[fr[:half], fi[:half]], [-fi[:half], fr[:half]]]) / float(r * r)
    mx = lambda a: jnp.asarray(a, F32).astype(BF16)
    bc = lambda a: jnp.broadcast_to(mx(a)[None], (tc,) + a.shape)
    return (bc(fa), bc(fk), mx(g), jnp.asarray(np.cos(angt), F32),
            jnp.asarray(np.sin(angt), F32), bc(fin))


def _shift_prev(u, lane, row):
    nr, nl = u.shape[-2], u.shape[-1]
    a = pltpu.roll(u, 1, u.ndim - 1)
    b = pltpu.roll(a, 1, u.ndim - 2)
    p = jnp.where(lane == 0, b, a)
    return jnp.where((lane == 0) & (row == 0), 0.0, p)


def _shift_next(u, lane, row):
    nr, nl = u.shape[-2], u.shape[-1]
    a = pltpu.roll(u, nl - 1, u.ndim - 1)
    b = pltpu.roll(a, nr - 1, u.ndim - 2)
    p = jnp.where(lane == nl - 1, b, a)
    return jnp.where((lane == nl - 1) & (row == nr - 1), 0.0, p)


def _lane_stage(y2, g_ref):
    tc, r2, r = y2.shape
    p = jnp.dot(y2.reshape(tc * r2, r).astype(BF16), g_ref[...], preferred_element_type=F32)
    p = p.reshape(tc, r2, r2)
    return p[:, :r, :r], p[:, :r, r:], p[:, r:, :r], p[:, r:, r:]


def _fwd_fft(x2, f1_ref, g_ref, tr, ti):
    r = tr.shape[0]
    a = jnp.einsum("cmk,ckr->cmr", f1_ref[...], x2.astype(BF16), preferred_element_type=F32)
    ar, ai = a[:, :r], a[:, r:]
    br = ar * tr - ai * ti
    bi = ar * ti + ai * tr
    p00, p01, p10, p11 = _lane_stage(jnp.concatenate([br, bi], axis=1), g_ref)
    return p00 - p11, p01 + p10


def _hyena_kernel(x0_ref, x1_ref, v_ref, w0_ref, w1_ref, wv_ref, b0_ref, b1_ref, bv_ref,
                  hb_ref, k_ref, fa_ref, fk_ref, g_ref, tr_ref, ti_ref, fin_ref, o_ref):
    shape = x0_ref.shape[1:]
    lane = lax.broadcasted_iota(jnp.int32, shape, 2)
    row = lax.broadcasted_iota(jnp.int32, shape, 1)
    tr, ti = tr_ref[...], ti_ref[...]

    def sconv(u_ref, w_ref, b_ref, b):
        u = u_ref[b].astype(F32)
        return (b_ref[...] + w_ref[0] * _shift_prev(u, lane, row) + w_ref[1] * u
                + w_ref[2] * _shift_next(u, lane, row))

    nb = x0_ref.shape[0]
    x0 = [sconv(x0_ref, w0_ref, b0_ref, b) for b in range(nb)]
    vx = [sconv(v_ref, wv_ref, bv_ref, b) * sconv(x1_ref, w1_ref, b1_ref, b) for b in range(nb)]

    kr, ki = _fwd_fft(k_ref[...], fk_ref, g_ref, tr, ti)
    for b0 in range(0, nb, 2):
        pair = vx[b0:b0 + 2]
        xi = pair[1] if len(pair) == 2 else jnp.zeros_like(pair[0])
        sr, si = _fwd_fft(jnp.concatenate([pair[0], xi], axis=1), fa_ref, g_ref, tr, ti)
        yr = sr * kr - si * ki
        yi = sr * ki + si * kr
        p00, p01, p10, p11 = _lane_stage(jnp.concatenate([yr, yi], axis=1), g_ref)
        cr, ci = p00 + p11, p10 - p01
        dr = cr * tr + ci * ti
        di = ci * tr - cr * ti
        d2 = jnp.concatenate([dr, di], axis=1).astype(BF16)
        y = jnp.einsum("cmk,ckr->cmr", fin_ref[...], d2, preferred_element_type=F32)
        half = shape[1]
        for j, yb in enumerate((y[:, :half], y[:, half:])[:len(pair)]):
            b = b0 + j
            o_ref[b] = ((yb + vx[b] * hb_ref[...]) * x0[b]).astype(o_ref.dtype)


def _hyena(u_t, kern_t, conv_w, conv_b, hyena_bias, out_dtype):
    bsz, c3, l = u_t.shape
    c = c3 // 3
    r = int(round(math.sqrt(2 * l)))
    assert r * r == 2 * l and r % 2 == 0
    half = r // 2
    tc = _tile(c, 16)
    nc = c // tc
    consts = _dft_consts(r, tc)
    u4 = u_t.reshape(bsz, c3, half, r)
    k3 = kern_t.reshape(c, r, r)
    cw = conv_w.reshape(conv_w.shape[0], c3, 1, 1)
    cb = conv_b.reshape(c3, 1, 1)
    hb = hyena_bias.reshape(c, 1, 1)
    stream = lambda s: pl.BlockSpec((bsz, tc, half, r), lambda j, s=s: (0, j + s * nc, 0, 0))
    wspec = lambda s: pl.BlockSpec((conv_w.shape[0], tc, 1, 1), lambda j, s=s: (0, j + s * nc, 0, 0))
    bspec = lambda s: pl.BlockSpec((tc, 1, 1), lambda j, s=s: (j + s * nc, 0, 0))
    cspec = lambda a: pl.BlockSpec(a.shape, lambda j, nd=a.ndim: (0,) * nd)
    out = pl.pallas_call(
        _hyena_kernel,
        grid=(nc,),
        in_specs=[stream(0), stream(1), stream(2), wspec(0), wspec(1), wspec(2),
                  bspec(0), bspec(1), bspec(2), bspec(0),
                  pl.BlockSpec((tc, r, r), lambda j: (j, 0, 0))] + [cspec(a) for a in consts],
        out_specs=pl.BlockSpec((bsz, tc, half, r), lambda j: (0, j, 0, 0)),
        out_shape=jax.ShapeDtypeStruct((bsz, c, half, r), out_dtype),
        compiler_params=_cparams(("parallel",)),
        name="hyena",
    )(u4, u4, u4, cw, cw, cw, cb, cb, cb, hb, k3, *consts)
    return out.reshape(bsz, c, l)


def _rep(x, n):
    if n % LANES:
        return x[:, :n]
    return x if n == LANES else jnp.concatenate([x] * (n // LANES), axis=1)


def _attn_kernel(q_ref, k_ref, v_ref, lq1_ref, lk1_ref, lq2_ref, lk2_ref, g_ref, o_ref,
                 s_ref, mb_ref, m_ref, l_ref, acc_ref, *, hd, tk, lam_init, eps):
    nkv = k_ref.shape[1] // tk
    tq = q_ref.shape[1]
    hw = v_ref.shape[2]

    def scores(t, slot):
        for j in range(2):
            qj = q_ref[0, :, j * hd:(j + 1) * hd]
            kj = k_ref[0, t * tk:(t + 1) * tk, j * hd:(j + 1) * hd]
            s = lax.dot_general(qj, kj, (((1,), (1,)), ((), ())), preferred_element_type=F32)
            s_ref[slot, j] = s
            mb_ref[slot, j] = jnp.broadcast_to(jnp.max(s, axis=-1, keepdims=True), (tq, LANES))

    def consume(t, slot):
        v = v_ref[0, t * tk:(t + 1) * tk, :]
        for j in range(2):
            m_prev = m_ref[j]
            m_new = jnp.maximum(m_prev, mb_ref[slot, j])
            alpha = jnp.exp2(m_prev - m_new)
            ps = [jnp.exp2(s_ref[slot, j, :, c * LANES:(c + 1) * LANES] - m_new)
                  for c in range(tk // LANES)]
            l_ref[j] = alpha * l_ref[j] + functools.reduce(lambda a, b: a + b, ps)
            p = jnp.concatenate(ps, axis=1).astype(v.dtype)
            acc_ref[j] = _rep(alpha, hw) * acc_ref[j] + jnp.dot(p, v, preferred_element_type=F32)
            m_ref[j] = m_new

    m_ref[...] = jnp.full(m_ref.shape, -jnp.inf, F32)
    l_ref[...] = jnp.zeros(l_ref.shape, F32)
    acc_ref[...] = jnp.zeros(acc_ref.shape, F32)
    scores(0, 0)
    for t in range(nkv):
        if t + 1 < nkv:
            scores(t + 1, (t + 1) % 2)
        consume(t, t % 2)

    lam = (jnp.exp(jnp.sum(lq1_ref[...] * lk1_ref[...], axis=-1, keepdims=True))
           - jnp.exp(jnp.sum(lq2_ref[...] * lk2_ref[...], axis=-1, keepdims=True)) + lam_init)
    l0 = jnp.sum(l_ref[0], axis=-1, keepdims=True)
    l1 = jnp.sum(l_ref[1], axis=-1, keepdims=True)
    o = acc_ref[0] / l0 - lam * (acc_ref[1] / l1)
    o = o * lax.rsqrt(jnp.mean(o * o, axis=-1, keepdims=True) + eps) * g_ref[...]
    o_ref[0] = (o * (1.0 - lam_init)).astype(o_ref.dtype)


def _diff_attention(q, k, v, lq1, lk1, lq2, lk2, subln_g, lam_init):
    bsz, l, width = q.shape
    hd = lq1.shape[-1]
    hw = 2 * hd
    heads = width // hw
    tq, tk = _tile(l, 512), _tile(l, 1024)
    vec = lambda a: a.reshape(1, -1).astype(F32)
    vspec = lambda n: pl.BlockSpec((1, n), lambda b, h, i: (0, 0))
    return pl.pallas_call(
        functools.partial(_attn_kernel, hd=hd, tk=tk, lam_init=lam_init, eps=SUBLN_EPS),
        grid=(bsz, heads, l // tq),
        in_specs=[pl.BlockSpec((1, tq, hw), lambda b, h, i: (b, i, h)),
                  pl.BlockSpec((1, l, hw), lambda b, h, i: (b, 0, h)),
                  pl.BlockSpec((1, l, hw), lambda b, h, i: (b, 0, h)),
                  vspec(hd), vspec(hd), vspec(hd), vspec(hd), vspec(hw)],
        out_specs=pl.BlockSpec((1, tq, hw), lambda b, h, i: (b, i, h)),
        out_shape=jax.ShapeDtypeStruct((bsz, l, width), BF16),
        scratch_shapes=[pltpu.VMEM((2, 2, tq, tk), F32), pltpu.VMEM((2, 2, tq, LANES), F32),
                        pltpu.VMEM((2, tq, LANES), F32), pltpu.VMEM((2, tq, LANES), F32),
                        pltpu.VMEM((2, tq, hw), F32)],
        compiler_params=_cparams(("parallel", "parallel", "arbitrary")),
        name="diff_attn",
    )(q, k, v, vec(lq1), vec(lk1), vec(lq2), vec(lk2), vec(subln_g))


def _outproj_kernel(yh_ref, yd_ref, w1_ref, w2_ref, x_ref, gt_ref, o_ref):
    acc = lax.dot_general(yh_ref[0], w1_ref[...].astype(BF16), (((0,), (0,)), ((), ())),
                          preferred_element_type=F32)
    acc = acc + jnp.dot(yd_ref[0], w2_ref[...].astype(BF16), preferred_element_type=F32)
    o_ref[0] = x_ref[0] + gt_ref[0] * acc


def _outproj(yh_t, yd, w, x, gt):
    bsz, l, d = x.shape
    c, kd = yh_t.shape[1], yd.shape[2]
    assert c == kd and w.shape[0] == c + kd
    tm, tn = _tile(l, 1024), _tile(d, 512)
    return pl.pallas_call(
        _outproj_kernel,
        grid=(bsz, l // tm, d // tn),
        in_specs=[pl.BlockSpec((1, c, tm), lambda b, i, j: (b, 0, i)),
                  pl.BlockSpec((1, tm, kd), lambda b, i, j: (b, i, 0)),
                  pl.BlockSpec((c, tn), lambda b, i, j: (0, j)),
                  pl.BlockSpec((kd, tn), lambda b, i, j: (1, j)),
                  pl.BlockSpec((1, tm, tn), lambda b, i, j: (b, i, j)),
                  pl.BlockSpec((1, 1, tn), lambda b, i, j: (b, 0, j))],
        out_specs=pl.BlockSpec((1, tm, tn), lambda b, i, j: (b, i, j)),
        out_shape=jax.ShapeDtypeStruct((bsz, l, d), F32),
        compiler_params=_cparams(("parallel", "parallel", "arbitrary")),
        name="outproj",
    )(yh_t, yd, w, w, x, gt)


def _gateup_kernel(h_ref, wg_ref, wu_ref, o_ref, *, n_real):
    j = pl.program_id(2)

    @pl.when(j < n_real)
    def _():
        h = h_ref[0]
        g = jnp.dot(h, wg_ref[...].astype(BF16), preferred_element_type=F32)
        u = jnp.dot(h, wu_ref[...].astype(BF16), preferred_element_type=F32)
        o_ref[0] = (_silu(g) * u).astype(o_ref.dtype)

    @pl.when(j >= n_real)
    def _():
        o_ref[0] = jnp.zeros(o_ref.shape[1:], o_ref.dtype)


def _gateup(h, wg, wu, n_pad):
    bsz, l, d = h.shape
    n = wg.shape[1]
    tm, tn = _tile(l, 1024), _tile(math.gcd(n, n_pad), 256)
    n_real = n // tn
    wspec = pl.BlockSpec((d, tn), lambda b, i, j: (0, jnp.minimum(j, n_real - 1)))
    return pl.pallas_call(
        functools.partial(_gateup_kernel, n_real=n_real),
        grid=(bsz, l // tm, n_pad // tn),
        in_specs=[pl.BlockSpec((1, tm, d), lambda b, i, j: (b, i, 0)), wspec, wspec],
        out_specs=pl.BlockSpec((1, tm, tn), lambda b, i, j: (b, i, j)),
        out_shape=jax.ShapeDtypeStruct((bsz, l, n_pad), BF16),
        compiler_params=_cparams(("parallel", "parallel", "arbitrary")),
        name="gateup",
    )(h, wg, wu)


def _down_kernel(g_ref, w_ref, x_ref, gt_ref, o_ref):
    k = pl.program_id(3)
    part = jnp.dot(g_ref[0], w_ref[...], preferred_element_type=F32)

    @pl.when(k == 0)
    def _():
        o_ref[0] = part

    @pl.when(k > 0)
    def _():
        o_ref[0] += part

    @pl.when(k == pl.num_programs(3) - 1)
    def _():
        o_ref[0] = x_ref[0] + gt_ref[0] * o_ref[0]


def _down(g, w, x, gt):
    bsz, l, d = x.shape
    kdim = g.shape[2]
    tm, tn, tk = _tile(l, 1024), _tile(d, 1024), _tile(kdim, 2816)
    return pl.pallas_call(
        _down_kernel,
        grid=(bsz, l // tm, d // tn, kdim // tk),
        in_specs=[pl.BlockSpec((1, tm, tk), lambda b, i, j, k: (b, i, k)),
                  pl.BlockSpec((tk, tn), lambda b, i, j, k: (k, j)),
                  pl.BlockSpec((1, tm, tn), lambda b, i, j, k: (b, i, j)),
                  pl.BlockSpec((1, 1, tn), lambda b, i, j, k: (b, 0, j))],
        out_specs=pl.BlockSpec((1, tm, tn), lambda b, i, j, k: (b, i, j)),
        out_shape=jax.ShapeDtypeStruct((bsz, l, d), F32),
        compiler_params=_cparams(("parallel", "parallel", "parallel", "arbitrary")),
        name="down",
    )(g, w, x, gt)


def _pad_to(a, axis, mult):
    n = a.shape[axis]
    pad = -n % mult
    if not pad:
        return a
    widths = [(0, 0)] * a.ndim
    widths[axis] = (0, pad)
    return jnp.pad(a, widths)


def kernel(x, c, positions, w_ada, b_ada, g_mix, g_ffn, w_in, conv_w, conv_b, f_w1, f_b1, f_w2, f_b2, f_w3, f_b3, f_w4, f_freq, hyena_bias, lambda_q1, lambda_k1, lambda_q2, lambda_k2, subln_g, w_out, w_gate, w_up, w_down, g_final):
    bsz, l, d = x.shape
    depth = w_ada.shape[0]
    ch = hyena_bias.shape[-1]
    hd = lambda_q1.shape[-1]
    qk = (w_in.shape[-1] - 3 * ch - (d - ch)) // 2
    cos, sin = _rope_tables(positions, hd)
    q_scale = hd ** -0.5 * math.log2(math.e)
    for i in range(depth):
        lam_init = 0.8 - 0.6 * math.exp(-0.3 * i)
        mod = _ada(c, w_ada[i], b_ada[i])
        sh1, sc1, gt1, sh2, sc2, gt2 = [mod[:, None, j * d:(j + 1) * d] for j in range(N_MOD)]
        h = _norm_mod(x, g_mix[i], sc1, sh1, NORM_EPS)
        w = w_in[i]
        o1, o2, o3 = 3 * ch, 3 * ch + qk, 3 * ch + 2 * qk
        u_t = _proj_t(h, w, 0, o1, F32)
        q = _proj(h, w, o1, qk, rope=(cos, sin), out_scale=q_scale)
        k = _proj(h, w, o2, qk, rope=(cos, sin))
        v = _proj(h, w, o3, w.shape[1] - o3)
        kern_t = _hyena_kernel_taps(l, ch, f_w1[i], f_b1[i], f_w2[i], f_b2[i], f_w3[i], f_b3[i],
                                    f_w4[i], f_freq[i])
        y_hy = _hyena(u_t, kern_t, conv_w[i], conv_b[i], hyena_bias[i], BF16)
        y_da = _diff_attention(q, k, v, lambda_q1[i], lambda_k1[i], lambda_q2[i], lambda_k2[i],
                               subln_g[i], lam_init)
        x = _outproj(y_hy, y_da, w_out[i], x, gt1)
        h = _norm_mod(x, g_ffn[i], sc2, sh2, NORM_EPS)
        wd = _pad_to(w_down[i].astype(BF16), 0, 1024)
        g = _gateup(h, w_gate[i], w_up[i], wd.shape[0])
        x = _down(g, wd, x, gt2)
    return _final_norm(x, g_final, NORM_EPS)
```

```python
import functools
import math

import numpy as np
import jax
import jax.numpy as jnp
from jax import lax
from jax.experimental import pallas as pl
from jax.experimental.pallas import tpu as pltpu

F32 = jnp.float32
BF16 = jnp.bfloat16

NORM_EPS = 1e-6
SUBLN_EPS = 1e-5
ROPE_THETA = 10000.0
FAST_DECAY_PCT = 0.3
SLOW_DECAY_PCT = 1.5
DECAY_TARGET = 1e-2
N_MOD = 6

V7X_VMEM_LIMIT_BYTES = 56 * 1024 * 1024
LANES = 128


def _cparams(sem):
    return pltpu.CompilerParams(dimension_semantics=sem, vmem_limit_bytes=V7X_VMEM_LIMIT_BYTES)


def _tile(n, pref):
    t = min(n, pref)
    while n % t:
        t -= 1
    return t


def _weight_cols(w):
    return 1024 if w.dtype == BF16 else 512


def _silu(x):
    return x * (1.0 / (1.0 + jnp.exp(-x)))


def _split_bf16(a):
    hi = a.astype(BF16)
    lo = (a - hi.astype(F32)).astype(BF16)
    return hi, lo


def _dot3(a, b):
    ah, al = _split_bf16(a)
    bh, bl = _split_bf16(b)
    d = functools.partial(jnp.dot, preferred_element_type=F32)
    return d(ah, bh) + (d(ah, bl) + d(al, bh))


def _ada_kernel(c_ref, w_ref, b_ref, o_ref):
    o_ref[...] = _dot3(_silu(c_ref[...]), w_ref[...]) + b_ref[...]


def _ada(c, w, b):
    bsz, d = c.shape
    n = w.shape[1]
    rows = 8
    cp = jnp.zeros((rows, d), F32).at[:bsz].set(c)
    tn = _tile(n, 512)
    out = pl.pallas_call(
        _ada_kernel,
        grid=(n // tn,),
        in_specs=[pl.BlockSpec((rows, d), lambda j: (0, 0)),
                  pl.BlockSpec((d, tn), lambda j: (0, j)),
                  pl.BlockSpec((1, tn), lambda j: (0, j))],
        out_specs=pl.BlockSpec((rows, tn), lambda j: (0, j)),
        out_shape=jax.ShapeDtypeStruct((rows, n), F32),
        compiler_params=_cparams(("parallel",)),
        name="ada",
    )(cp, w, b.reshape(1, n))
    return out[:bsz]


def _norm_mod_kernel(x_ref, g_ref, sc_ref, sh_ref, o_ref, *, eps):
    x = x_ref[0]
    y = x * lax.rsqrt(jnp.mean(x * x, axis=-1, keepdims=True) + eps) * g_ref[...]
    o_ref[0] = (y * (1.0 + sc_ref[0]) + sh_ref[0]).astype(o_ref.dtype)


def _norm_mod(x, g, sc, sh, eps):
    bsz, l, d = x.shape
    tr = _tile(l, 512)
    return pl.pallas_call(
        functools.partial(_norm_mod_kernel, eps=eps),
        grid=(bsz, l // tr),
        in_specs=[pl.BlockSpec((1, tr, d), lambda b, i: (b, i, 0)),
                  pl.BlockSpec((1, d), lambda b, i: (0, 0)),
                  pl.BlockSpec((1, 1, d), lambda b, i: (b, 0, 0)),
                  pl.BlockSpec((1, 1, d), lambda b, i: (b, 0, 0))],
        out_specs=pl.BlockSpec((1, tr, d), lambda b, i: (b, i, 0)),
        out_shape=jax.ShapeDtypeStruct((bsz, l, d), BF16),
        compiler_params=_cparams(("parallel", "parallel")),
        name="norm_mod",
    )(x, g.reshape(1, d), sc, sh)


def _final_norm_kernel(x_ref, g_ref, o_ref, *, eps):
    x = x_ref[0]
    o_ref[0] = x * lax.rsqrt(jnp.mean(x * x, axis=-1, keepdims=True) + eps) * g_ref[...]


def _final_norm(x, g, eps):
    bsz, l, d = x.shape
    tr = _tile(l, 512)
    return pl.pallas_call(
        functools.partial(_final_norm_kernel, eps=eps),
        grid=(bsz, l // tr),
        in_specs=[pl.BlockSpec((1, tr, d), lambda b, i: (b, i, 0)),
                  pl.BlockSpec((1, d), lambda b, i: (0, 0))],
        out_specs=pl.BlockSpec((1, tr, d), lambda b, i: (b, i, 0)),
        out_shape=jax.ShapeDtypeStruct((bsz, l, d), F32),
        compiler_params=_cparams(("parallel", "parallel")),
        name="final_norm",
    )(x, g.reshape(1, d))


def _rope_kernel(pos_ref, inv_ref, sgn_ref, cos_ref, sin_ref):
    ang = pos_ref[0].astype(F32) * inv_ref[...]
    cos_ref[0] = jnp.cos(ang)
    sin_ref[0] = jnp.sin(ang) * sgn_ref[...]


def _rope_tables(positions, dim):
    bsz, l = positions.shape
    half = dim // 2
    inv = 1.0 / (ROPE_THETA ** (jnp.arange(0, dim, 2, dtype=F32) / dim))
    inv = jnp.concatenate([inv, inv]).reshape(1, dim)
    sgn = jnp.asarray(np.concatenate([-np.ones(half), np.ones(half)]).reshape(1, dim), F32)
    tr = _tile(l, 1024)
    shp = jax.ShapeDtypeStruct((bsz, l, dim), F32)
    return pl.pallas_call(
        _rope_kernel,
        grid=(bsz, l // tr),
        in_specs=[pl.BlockSpec((1, tr, 1), lambda b, i: (b, i, 0)),
                  pl.BlockSpec((1, dim), lambda b, i: (0, 0)),
                  pl.BlockSpec((1, dim), lambda b, i: (0, 0))],
        out_specs=[pl.BlockSpec((1, tr, dim), lambda b, i: (b, i, 0))] * 2,
        out_shape=[shp, shp],
        compiler_params=_cparams(("parallel", "parallel")),
        name="rope_tables",
    )(positions.reshape(bsz, l, 1), inv, sgn)


def _proj_t_kernel(w_ref, h_ref, o_ref):
    o_ref[0] = lax.dot_general(w_ref[...].astype(BF16), h_ref[0], (((0,), (1,)), ((), ())),
                               preferred_element_type=F32).astype(o_ref.dtype)


def _proj_t(h, w, col0, n, out_dtype):
    bsz, l, d = h.shape
    tm, tn = _tile(l, 1024), _tile(math.gcd(n, col0) if col0 else n, _weight_cols(w))
    j0 = col0 // tn
    return pl.pallas_call(
        _proj_t_kernel,
        grid=(bsz, l // tm, n // tn),
        in_specs=[pl.BlockSpec((d, tn), lambda b, i, j: (0, j + j0)),
                  pl.BlockSpec((1, tm, d), lambda b, i, j: (b, i, 0))],
        out_specs=pl.BlockSpec((1, tn, tm), lambda b, i, j: (b, j, i)),
        out_shape=jax.ShapeDtypeStruct((bsz, n, l), out_dtype),
        compiler_params=_cparams(("parallel", "parallel", "arbitrary")),
        name="proj_t",
    )(w, h)


def _proj_kernel(h_ref, w_ref, *rest, rope_dim, out_scale):
    acc = jnp.dot(h_ref[0], w_ref[...].astype(BF16), preferred_element_type=F32)
    if rope_dim:
        cos_ref, sin_ref, o_ref = rest
        cos, sin = cos_ref[0], sin_ref[0]
        for g in range(acc.shape[1] // rope_dim):
            xg = acc[:, g * rope_dim:(g + 1) * rope_dim]
            yg = xg * cos + pltpu.roll(xg, rope_dim // 2, 1) * sin
            o_ref[0, :, g * rope_dim:(g + 1) * rope_dim] = (yg * out_scale).astype(o_ref.dtype)
    else:
        (o_ref,) = rest
        o_ref[0] = acc.astype(o_ref.dtype)


def _proj(h, w, col0, n, rope=None, out_scale=1.0):
    bsz, l, d = h.shape
    tm, tn = _tile(l, 1024), _tile(math.gcd(n, col0) if col0 else n, _weight_cols(w))
    j0 = col0 // tn
    in_specs = [pl.BlockSpec((1, tm, d), lambda b, i, j: (b, i, 0)),
                pl.BlockSpec((d, tn), lambda b, i, j: (0, j + j0))]
    args = [h, w]
    rope_dim = 0
    if rope is not None:
        rope_dim = rope[0].shape[-1]
        in_specs += [pl.BlockSpec((1, tm, rope_dim), lambda b, i, j: (b, i, 0))] * 2
        args += list(rope)
    return pl.pallas_call(
        functools.partial(_proj_kernel, rope_dim=rope_dim, out_scale=out_scale),
        grid=(bsz, l // tm, n // tn),
        in_specs=in_specs,
        out_specs=pl.BlockSpec((1, tm, tn), lambda b, i, j: (b, i, j)),
        out_shape=jax.ShapeDtypeStruct((bsz, l, n), BF16),
        compiler_params=_cparams(("parallel", "parallel", "arbitrary")),
        name="proj_rope" if rope_dim else "proj",
    )(*args)


def _filt_kernel(z_ref, t_ref, w1_ref, b1_ref, w2_ref, b2_ref, w3_ref, b3_ref, fr_ref,
                 w4_ref, ad_ref, o_ref):
    fr = fr_ref[...]
    h = jnp.sin(fr * (_dot3(w1_ref[...], z_ref[...]) + b1_ref[...]))
    h = jnp.sin(fr * (_dot3(w2_ref[...], h) + b2_ref[...]))
    h = jnp.sin(fr * (_dot3(w3_ref[...], h) + b3_ref[...]))
    t = t_ref[...]
    decay = jnp.exp(-(ad_ref[...] * t[0:1, :]))
    o_ref[...] = _dot3(w4_ref[0], h) * decay * t[1:2, :]


def _hyena_kernel_taps(l, c, f_w1, f_b1, f_w2, f_b2, f_w3, f_b3, f_w4, f_freq):
    n = 2 * l
    emb, hid = f_w1.shape
    bands = (emb - 1) // 2
    t = jnp.linspace(0.0, 1.0, l, dtype=F32)[:, None]
    w = 2.0 * math.pi * jnp.arange(l, dtype=F32)[:, None] / l
    f = jnp.linspace(1e-4, bands - 1, bands, dtype=F32)[None, :]
    z = jnp.concatenate([t, jnp.cos(f * w), -jnp.sin(f * w)], axis=-1)
    cols = lambda a: jnp.concatenate([a, a[:1], a[:0:-1]], axis=0).T
    embp = -(-emb // 8) * 8
    z = jnp.pad(cols(z), ((0, embp - emb), (0, 0)))
    mask = jnp.ones((1, n), F32).at[0, l].set(0.0)
    tm = jnp.concatenate([cols(t), mask], axis=0)
    min_decay = math.log(DECAY_TARGET) / SLOW_DECAY_PCT
    max_decay = math.log(DECAY_TARGET) / FAST_DECAY_PCT
    ad = jnp.abs(jnp.linspace(min_decay, max_decay, c, dtype=F32)).reshape(c, 1)
    w1t = jnp.pad(f_w1.T, ((0, 0), (0, embp - emb)))
    w4t = f_w4.T.reshape(2, c, hid)
    col = lambda a: a.reshape(hid, 1)
    tn = _tile(l, 1024)
    nt = n // tn
    full = lambda shape: pl.BlockSpec(shape, lambda i: (0,) * len(shape))
    return pl.pallas_call(
        _filt_kernel,
        grid=(nt,),
        in_specs=[pl.BlockSpec((embp, tn), lambda i: (0, i)),
                  pl.BlockSpec((2, tn), lambda i: (0, i)),
                  full((hid, embp)), full((hid, 1)), full((hid, hid)), full((hid, 1)),
                  full((hid, hid)), full((hid, 1)), full((hid, 1)),
                  pl.BlockSpec((1, c, hid), lambda i: ((2 * i) // nt, 0, 0)),
                  full((c, 1))],
        out_specs=pl.BlockSpec((c, tn), lambda i: (0, i)),
        out_shape=jax.ShapeDtypeStruct((c, n), F32),
        compiler_params=_cparams(("parallel",)),
        name="hyena_filter",
    )(z, tm, w1t, col(f_b1), f_w2.T, col(f_b2), f_w3.T, col(f_b3), col(f_freq), w4t, ad)


def _dft_consts(r, tc):
    half = r // 2
    idx = np.arange(r)
    ang = -2.0 * np.pi * np.outer(idx, idx) / r
    fr, fi = np.cos(ang), np.sin(ang)
    angt = -2.0 * np.pi * np.outer(idx, idx) / (r * r)
    fa = np.block([[fr[:, :half], -fi[:, :half]], [fi[:, :half], fr[:, :half]]])
    fk = np.concatenate([fr, fi], axis=0)
    g = np.concatenate([fr, fi], axis=1)
    fin = np.block([[fr[:half], fi[:half]], [-fi[:half], fr[:half]]]) / float(r * r)
    mx = lambda a: jnp.asarray(a, F32).astype(BF16)
    bc = lambda a: jnp.broadcast_to(mx(a)[None], (tc,) + a.shape)
    return (bc(fa), bc(fk), mx(g), jnp.asarray(np.cos(angt), F32),
            jnp.asarray(np.sin(angt), F32), bc(fin))


def _shift_prev(u, lane, row):
    nr, nl = u.shape[-2], u.shape[-1]
    a = pltpu.roll(u, 1, u.ndim - 1)
    b = pltpu.roll(a, 1, u.ndim - 2)
    p = jnp.where(lane == 0, b, a)
    return jnp.where((lane == 0) & (row == 0), 0.0, p)


def _shift_next(u, lane, row):
    nr, nl = u.shape[-2], u.shape[-1]
    a = pltpu.roll(u, nl - 1, u.ndim - 1)
    b = pltpu.roll(a, nr - 1, u.ndim - 2)
    p = jnp.where(lane == nl - 1, b, a)
    return jnp.where((lane == nl - 1) & (row == nr - 1), 0.0, p)


def _lane_stage(y2, g_ref):
    tc, r2, r = y2.shape
    p = jnp.dot(y2.reshape(tc * r2, r).astype(BF16), g_ref[...], preferred_element_type=F32)
    p = p.reshape(tc, r2, r2)
    return p[:, :r, :r], p[:, :r, r:], p[:, r:, :r], p[:, r:, r:]


def _fwd_fft(x2, f1_ref, g_ref, tr, ti):
    r = tr.shape[0]
    a = jnp.einsum("cmk,ckr->cmr", f1_ref[...], x2.astype(BF16), preferred_element_type=F32)
    ar, ai = a[:, :r], a[:, r:]
    br = ar * tr - ai * ti
    bi = ar * ti + ai * tr
    p00, p01, p10, p11 = _lane_stage(jnp.concatenate([br, bi], axis=1), g_ref)
    return p00 - p11, p01 + p10


def _hyena_kernel(x0_ref, x1_ref, v_ref, w0_ref, w1_ref, wv_ref, b0_ref, b1_ref, bv_ref,
                  hb_ref, k_ref, fa_ref, fk_ref, g_ref, tr_ref, ti_ref, fin_ref, o_ref):
    shape = x0_ref.shape[1:]
    lane = lax.broadcasted_iota(jnp.int32, shape, 2)
    row = lax.broadcasted_iota(jnp.int32, shape, 1)
    tr, ti = tr_ref[...], ti_ref[...]

    def sconv(u_ref, w_ref, b_ref, b):
        u = u_ref[b].astype(F32)
        return (b_ref[...] + w_ref[0] * _shift_prev(u, lane, row) + w_ref[1] * u
                + w_ref[2] * _shift_next(u, lane, row))

    nb = x0_ref.shape[0]
    x0 = [sconv(x0_ref, w0_ref, b0_ref, b) for b in range(nb)]
    vx = [sconv(v_ref, wv_ref, bv_ref, b) * sconv(x1_ref, w1_ref, b1_ref, b) for b in range(nb)]

    kr, ki = _fwd_fft(k_ref[...], fk_ref, g_ref, tr, ti)
    for b0 in range(0, nb, 2):
        pair = vx[b0:b0 + 2]
        xi = pair[1] if len(pair) == 2 else jnp.zeros_like(pair[0])
        sr, si = _fwd_fft(jnp.concatenate([pair[0], xi], axis=1), fa_ref, g_ref, tr, ti)
        yr = sr * kr - si * ki
        yi = sr * ki + si * kr
        p00, p01, p10, p11 = _lane_stage(jnp.concatenate([yr, yi], axis=1), g_ref)
        cr, ci = p00 + p11, p10 - p01
        dr = cr * tr + ci * ti
        di = ci * tr - cr * ti
        d2 = jnp.concatenate([dr, di], axis=1).astype(BF16)
        y = jnp.einsum("cmk,ckr->cmr", fin_ref[...], d2, preferred_element_type=F32)
        half = shape[1]
        for j, yb in enumerate((y[:, :half], y[:, half:])[:len(pair)]):
            b = b0 + j
            o_ref[b] = ((yb + vx[b] * hb_ref[...]) * x0[b]).astype(o_ref.dtype)


def _hyena(u_t, kern_t, conv_w, conv_b, hyena_bias, out_dtype):
    bsz, c3, l = u_t.shape
    c = c3 // 3
    r = int(round(math.sqrt(2 * l)))
    assert r * r == 2 * l and r % 2 == 0
    half = r // 2
    tc = _tile(c, 16)
    nc = c // tc
    consts = _dft_consts(r, tc)
    u4 = u_t.reshape(bsz, c3, half, r)
    k3 = kern_t.reshape(c, r, r)
    cw = conv_w.reshape(conv_w.shape[0], c3, 1, 1)
    cb = conv_b.reshape(c3, 1, 1)
    hb = hyena_bias.reshape(c, 1, 1)
    stream = lambda s: pl.BlockSpec((bsz, tc, half, r), lambda j, s=s: (0, j + s * nc, 0, 0))
    wspec = lambda s: pl.BlockSpec((conv_w.shape[0], tc, 1, 1), lambda j, s=s: (0, j + s * nc, 0, 0))
    bspec = lambda s: pl.BlockSpec((tc, 1, 1), lambda j, s=s: (j + s * nc, 0, 0))
    cspec = lambda a: pl.BlockSpec(a.shape, lambda j, nd=a.ndim: (0,) * nd)
    out = pl.pallas_call(
        _hyena_kernel,
        grid=(nc,),
        in_specs=[stream(0), stream(1), stream(2), wspec(0), wspec(1), wspec(2),
                  bspec(0), bspec(1), bspec(2), bspec(0),
                  pl.BlockSpec((tc, r, r), lambda j: (j, 0, 0))] + [cspec(a) for a in consts],
        out_specs=pl.BlockSpec((bsz, tc, half, r), lambda j: (0, j, 0, 0)),
        out_shape=jax.ShapeDtypeStruct((bsz, c, half, r), out_dtype),
        compiler_params=_cparams(("parallel",)),
        name="hyena",
    )(u4, u4, u4, cw, cw, cw, cb, cb, cb, hb, k3, *consts)
    return out.reshape(bsz, c, l)


def _rep(x, n):
    if n % LANES:
        return x[:, :n]
    return x if n == LANES else jnp.concatenate([x] * (n // LANES), axis=1)


def _attn_kernel(q_ref, k_ref, v_ref, lq1_ref, lk1_ref, lq2_ref, lk2_ref, g_ref, *rest,
                 ncast, hd, tk, lam_init, eps):
    cast_in, o_ref, cast_out = rest[:ncast], rest[ncast], rest[ncast + 1:2 * ncast + 1]
    s_ref, mb_ref, m_ref, l_ref, acc_ref = rest[2 * ncast + 1:]
    for src, dst in zip(cast_in, cast_out):
        dst[...] = src[...].astype(dst.dtype)
    nkv = k_ref.shape[1] // tk
    tq = q_ref.shape[1]
    hw = v_ref.shape[2]

    def scores(t, slot):
        for j in range(2):
            qj = q_ref[0, :, j * hd:(j + 1) * hd]
            kj = k_ref[0, t * tk:(t + 1) * tk, j * hd:(j + 1) * hd]
            s = lax.dot_general(qj, kj, (((1,), (1,)), ((), ())), preferred_element_type=F32)
            s_ref[slot, j] = s
            mb_ref[slot, j] = jnp.broadcast_to(jnp.max(s, axis=-1, keepdims=True), (tq, LANES))

    def consume(t, slot):
        v = v_ref[0, t * tk:(t + 1) * tk, :]
        for j in range(2):
            m_prev = m_ref[j]
            m_new = jnp.maximum(m_prev, mb_ref[slot, j])
            alpha = jnp.exp2(m_prev - m_new)
            ps = [jnp.exp2(s_ref[slot, j, :, c * LANES:(c + 1) * LANES] - m_new)
                  for c in range(tk // LANES)]
            l_ref[j] = alpha * l_ref[j] + functools.reduce(lambda a, b: a + b, ps)
            p = jnp.concatenate(ps, axis=1).astype(v.dtype)
            acc_ref[j] = _rep(alpha, hw) * acc_ref[j] + jnp.dot(p, v, preferred_element_type=F32)
            m_ref[j] = m_new

    m_ref[...] = jnp.full(m_ref.shape, -jnp.inf, F32)
    l_ref[...] = jnp.zeros(l_ref.shape, F32)
    acc_ref[...] = jnp.zeros(acc_ref.shape, F32)
    scores(0, 0)
    for t in range(nkv):
        if t + 1 < nkv:
            scores(t + 1, (t + 1) % 2)
        consume(t, t % 2)

    lam = (jnp.exp(jnp.sum(lq1_ref[...] * lk1_ref[...], axis=-1, keepdims=True))
           - jnp.exp(jnp.sum(lq2_ref[...] * lk2_ref[...], axis=-1, keepdims=True)) + lam_init)
    l0 = jnp.sum(l_ref[0], axis=-1, keepdims=True)
    l1 = jnp.sum(l_ref[1], axis=-1, keepdims=True)
    o = acc_ref[0] / l0 - lam * (acc_ref[1] / l1)
    o = o * lax.rsqrt(jnp.mean(o * o, axis=-1, keepdims=True) + eps) * g_ref[...]
    o_ref[0] = (o * (1.0 - lam_init)).astype(o_ref.dtype)


def _cast_blocks(shape, steps):
    r, c = shape
    for a in range(steps, 0, -1):
        b = steps // a
        if steps % a or r % a or c % b:
            continue
        rb, cb = r // a, c // b
        if rb % 16 == 0 and (b == 1 or cb % LANES == 0):
            return rb, cb, b
    return None


def _diff_attention(q, k, v, lq1, lk1, lq2, lk2, subln_g, lam_init, cast=()):
    bsz, l, width = q.shape
    hd = lq1.shape[-1]
    hw = 2 * hd
    heads = width // hw
    tq, tk = _tile(l, 512), _tile(l, 1024)
    nq = l // tq
    steps = bsz * heads * nq
    vec = lambda a: a.reshape(1, -1).astype(F32)
    vspec = lambda n: pl.BlockSpec((1, n), lambda b, h, i: (0, 0))
    plans = [_cast_blocks(a.shape, steps) for a in cast]
    riders = [(a, p) for a, p in zip(cast, plans) if p is not None]

    def cast_spec(p):
        rb, cb, ncb = p
        return pl.BlockSpec((rb, cb), lambda b, h, i: (((b * heads + h) * nq + i) // ncb,
                                                        ((b * heads + h) * nq + i) % ncb))

    cast_specs = [cast_spec(p) for _, p in riders]
    outs = pl.pallas_call(
        functools.partial(_attn_kernel, ncast=len(riders), hd=hd, tk=tk, lam_init=lam_init,
                          eps=SUBLN_EPS),
        grid=(bsz, heads, nq),
        in_specs=[pl.BlockSpec((1, tq, hw), lambda b, h, i: (b, i, h)),
                  pl.BlockSpec((1, l, hw), lambda b, h, i: (b, 0, h)),
                  pl.BlockSpec((1, l, hw), lambda b, h, i: (b, 0, h)),
                  vspec(hd), vspec(hd), vspec(hd), vspec(hd), vspec(hw)] + cast_specs,
        out_specs=[pl.BlockSpec((1, tq, hw), lambda b, h, i: (b, i, h))] + cast_specs,
        out_shape=[jax.ShapeDtypeStruct((bsz, l, width), BF16)]
                  + [jax.ShapeDtypeStruct(a.shape, BF16) for a, _ in riders],
        scratch_shapes=[pltpu.VMEM((2, 2, tq, tk), F32), pltpu.VMEM((2, 2, tq, LANES), F32),
                        pltpu.VMEM((2, tq, LANES), F32), pltpu.VMEM((2, tq, LANES), F32),
                        pltpu.VMEM((2, tq, hw), F32)],
        compiler_params=_cparams(("parallel", "parallel", "arbitrary")),
        name="diff_attn",
    )(q, k, v, vec(lq1), vec(lk1), vec(lq2), vec(lk2), vec(subln_g), *[a for a, _ in riders])
    rounded = iter(outs[1:])
    return outs[0], [next(rounded) if p is not None else a.astype(BF16)
                     for a, p in zip(cast, plans)]


def _outproj_kernel(yh_ref, yd_ref, w1_ref, w2_ref, x_ref, gt_ref, o_ref):
    acc = lax.dot_general(yh_ref[0], w1_ref[...].astype(BF16), (((0,), (0,)), ((), ())),
                          preferred_element_type=F32)
    acc = acc + jnp.dot(yd_ref[0], w2_ref[...].astype(BF16), preferred_element_type=F32)
    o_ref[0] = x_ref[0] + gt_ref[0] * acc


def _outproj(yh_t, yd, w, x, gt):
    bsz, l, d = x.shape
    c, kd = yh_t.shape[1], yd.shape[2]
    assert c == kd and w.shape[0] == c + kd
    tm, tn = _tile(l, 1024), _tile(d, _weight_cols(w))
    return pl.pallas_call(
        _outproj_kernel,
        grid=(bsz, l // tm, d // tn),
        in_specs=[pl.BlockSpec((1, c, tm), lambda b, i, j: (b, 0, i)),
                  pl.BlockSpec((1, tm, kd), lambda b, i, j: (b, i, 0)),
                  pl.BlockSpec((c, tn), lambda b, i, j: (0, j)),
                  pl.BlockSpec((kd, tn), lambda b, i, j: (1, j)),
                  pl.BlockSpec((1, tm, tn), lambda b, i, j: (b, i, j)),
                  pl.BlockSpec((1, 1, tn), lambda b, i, j: (b, 0, j))],
        out_specs=pl.BlockSpec((1, tm, tn), lambda b, i, j: (b, i, j)),
        out_shape=jax.ShapeDtypeStruct((bsz, l, d), F32),
        compiler_params=_cparams(("parallel", "parallel", "arbitrary")),
        name="outproj",
    )(yh_t, yd, w, w, x, gt)


def _gateup_kernel(h_ref, wg_ref, wu_ref, o_ref, *, n_real):
    j = pl.program_id(2)

    @pl.when(j < n_real)
    def _():
        h = h_ref[0]
        g = jnp.dot(h, wg_ref[...].astype(BF16), preferred_element_type=F32)
        u = jnp.dot(h, wu_ref[...].astype(BF16), preferred_element_type=F32)
        o_ref[0] = (_silu(g) * u).astype(o_ref.dtype)

    @pl.when(j >= n_real)
    def _():
        o_ref[0] = jnp.zeros(o_ref.shape[1:], o_ref.dtype)


def _gateup(h, wg, wu, n_pad):
    bsz, l, d = h.shape
    n = wg.shape[1]
    tm, tn = _tile(l, 1024), _tile(math.gcd(n, n_pad), 256)
    n_real = n // tn
    wspec = pl.BlockSpec((d, tn), lambda b, i, j: (0, jnp.minimum(j, n_real - 1)))
    return pl.pallas_call(
        functools.partial(_gateup_kernel, n_real=n_real),
        grid=(bsz, l // tm, n_pad // tn),
        in_specs=[pl.BlockSpec((1, tm, d), lambda b, i, j: (b, i, 0)), wspec, wspec],
        out_specs=pl.BlockSpec((1, tm, tn), lambda b, i, j: (b, i, j)),
        out_shape=jax.ShapeDtypeStruct((bsz, l, n_pad), BF16),
        compiler_params=_cparams(("parallel", "parallel", "arbitrary")),
        name="gateup",
    )(h, wg, wu)


def _down_kernel(g_ref, w_ref, x_ref, gt_ref, o_ref):
    k = pl.program_id(3)
    part = jnp.dot(g_ref[0], w_ref[...], preferred_element_type=F32)

    @pl.when(k == 0)
    def _():
        o_ref[0] = part

    @pl.when(k > 0)
    def _():
        o_ref[0] += part

    @pl.when(k == pl.num_programs(3) - 1)
    def _():
        o_ref[0] = x_ref[0] + gt_ref[0] * o_ref[0]


def _down(g, w, x, gt):
    bsz, l, d = x.shape
    kdim = g.shape[2]
    tm, tn, tk = _tile(l, 1024), _tile(d, 1024), _tile(kdim, 2816)
    return pl.pallas_call(
        _down_kernel,
        grid=(bsz, l // tm, d // tn, kdim // tk),
        in_specs=[pl.BlockSpec((1, tm, tk), lambda b, i, j, k: (b, i, k)),
                  pl.BlockSpec((tk, tn), lambda b, i, j, k: (k, j)),
                  pl.BlockSpec((1, tm, tn), lambda b, i, j, k: (b, i, j)),
                  pl.BlockSpec((1, 1, tn), lambda b, i, j, k: (b, 0, j))],
        out_specs=pl.BlockSpec((1, tm, tn), lambda b, i, j, k: (b, i, j)),
        out_shape=jax.ShapeDtypeStruct((bsz, l, d), F32),
        compiler_params=_cparams(("parallel", "parallel", "parallel", "arbitrary")),
        name="down",
    )(g, w, x, gt)


def _pad_to(a, axis, mult):
    n = a.shape[axis]
    pad = -n % mult
    if not pad:
        return a
    widths = [(0, 0)] * a.ndim
    widths[axis] = (0, pad)
    return jnp.pad(a, widths)


def kernel(x, c, positions, w_ada, b_ada, g_mix, g_ffn, w_in, conv_w, conv_b, f_w1, f_b1, f_w2, f_b2, f_w3, f_b3, f_w4, f_freq, hyena_bias, lambda_q1, lambda_k1, lambda_q2, lambda_k2, subln_g, w_out, w_gate, w_up, w_down, g_final):
    bsz, l, d = x.shape
    depth = w_ada.shape[0]
    ch = hyena_bias.shape[-1]
    hd = lambda_q1.shape[-1]
    qk = (w_in.shape[-1] - 3 * ch - (d - ch)) // 2
    cos, sin = _rope_tables(positions, hd)
    q_scale = hd ** -0.5 * math.log2(math.e)
    for i in range(depth):
        lam_init = 0.8 - 0.6 * math.exp(-0.3 * i)
        mod = _ada(c, w_ada[i], b_ada[i])
        sh1, sc1, gt1, sh2, sc2, gt2 = [mod[:, None, j * d:(j + 1) * d] for j in range(N_MOD)]
        h = _norm_mod(x, g_mix[i], sc1, sh1, NORM_EPS)
        w = w_in[i].astype(BF16)
        o1, o2, o3 = 3 * ch, 3 * ch + qk, 3 * ch + 2 * qk
        u_t = _proj_t(h, w, 0, o1, F32)
        q = _proj(h, w, o1, qk, rope=(cos, sin), out_scale=q_scale)
        k = _proj(h, w, o2, qk, rope=(cos, sin))
        v = _proj(h, w, o3, w.shape[1] - o3)
        kern_t = _hyena_kernel_taps(l, ch, f_w1[i], f_b1[i], f_w2[i], f_b2[i], f_w3[i], f_b3[i],
                                    f_w4[i], f_freq[i])
        y_hy = _hyena(u_t, kern_t, conv_w[i], conv_b[i], hyena_bias[i], BF16)
        y_da, (wo, wg, wu, wd) = _diff_attention(
            q, k, v, lambda_q1[i], lambda_k1[i], lambda_q2[i], lambda_k2[i], subln_g[i], lam_init,
            cast=(w_out[i], w_gate[i], w_up[i], w_down[i]))
        x = _outproj(y_hy, y_da, wo, x, gt1)
        h = _norm_mod(x, g_ffn[i], sc2, sh2, NORM_EPS)
        wd = _pad_to(wd, 0, 1024)
        g = _gateup(h, wg, wu, wd.shape[0])
        x = _down(g, wd, x, gt2)
    return _final_norm(x, g_final, NORM_EPS)
```

```python
import functools
import math

import numpy as np
import jax
import jax.numpy as jnp
from jax import lax
from jax.experimental import pallas as pl
from jax.experimental.pallas import tpu as pltpu

F32 = jnp.float32
BF16 = jnp.bfloat16

NORM_EPS = 1e-6
SUBLN_EPS = 1e-5
ROPE_THETA = 10000.0
FAST_DECAY_PCT = 0.3
SLOW_DECAY_PCT = 1.5
DECAY_TARGET = 1e-2
N_MOD = 6

V7X_VMEM_LIMIT_BYTES = 56 * 1024 * 1024
LANES = 128
EPILOGUE_ROWS = 16


def _cparams(sem):
    return pltpu.CompilerParams(dimension_semantics=sem, vmem_limit_bytes=V7X_VMEM_LIMIT_BYTES)


def _tile(n, pref):
    t = min(n, pref)
    while n % t:
        t -= 1
    return t


def _weight_cols(w):
    return 1024 if w.dtype == BF16 else 512


def _silu(x):
    return x * (1.0 / (1.0 + jnp.exp(-x)))


def _split_bf16(a):
    hi = a.astype(BF16)
    lo = (a - hi.astype(F32)).astype(BF16)
    return hi, lo


def _dot3(a, b):
    ah, al = _split_bf16(a)
    bh, bl = _split_bf16(b)
    d = functools.partial(jnp.dot, preferred_element_type=F32)
    return d(ah, bh) + (d(ah, bl) + d(al, bh))


def _ada_kernel(c_ref, w_ref, b_ref, o_ref):
    o_ref[...] = _dot3(_silu(c_ref[...]), w_ref[...]) + b_ref[...]


def _ada(c, w, b):
    bsz, d = c.shape
    n = w.shape[1]
    rows = 8
    cp = jnp.zeros((rows, d), F32).at[:bsz].set(c)
    tn = _tile(n, 512)
    out = pl.pallas_call(
        _ada_kernel,
        grid=(n // tn,),
        in_specs=[pl.BlockSpec((rows, d), lambda j: (0, 0)),
                  pl.BlockSpec((d, tn), lambda j: (0, j)),
                  pl.BlockSpec((1, tn), lambda j: (0, j))],
        out_specs=pl.BlockSpec((rows, tn), lambda j: (0, j)),
        out_shape=jax.ShapeDtypeStruct((rows, n), F32),
        compiler_params=_cparams(("parallel",)),
        name="ada",
    )(cp, w, b.reshape(1, n))
    return out[:bsz]


def _norm_mod_kernel(x_ref, g_ref, sc_ref, sh_ref, o_ref, *, eps):
    x = x_ref[0]
    y = x * lax.rsqrt(jnp.mean(x * x, axis=-1, keepdims=True) + eps) * g_ref[...]
    o_ref[0] = (y * (1.0 + sc_ref[0]) + sh_ref[0]).astype(o_ref.dtype)


def _norm_mod(x, g, sc, sh, eps):
    bsz, l, d = x.shape
    tr = _tile(l, 512)
    return pl.pallas_call(
        functools.partial(_norm_mod_kernel, eps=eps),
        grid=(bsz, l // tr),
        in_specs=[pl.BlockSpec((1, tr, d), lambda b, i: (b, i, 0)),
                  pl.BlockSpec((1, d), lambda b, i: (0, 0)),
                  pl.BlockSpec((1, 1, d), lambda b, i: (b, 0, 0)),
                  pl.BlockSpec((1, 1, d), lambda b, i: (b, 0, 0))],
        out_specs=pl.BlockSpec((1, tr, d), lambda b, i: (b, i, 0)),
        out_shape=jax.ShapeDtypeStruct((bsz, l, d), BF16),
        compiler_params=_cparams(("parallel", "parallel")),
        name="norm_mod",
    )(x, g.reshape(1, d), sc, sh)


def _rope_kernel(pos_ref, inv_ref, sgn_ref, cos_ref, sin_ref):
    ang = pos_ref[0].astype(F32) * inv_ref[...]
    cos_ref[0] = jnp.cos(ang)
    sin_ref[0] = jnp.sin(ang) * sgn_ref[...]


def _rope_tables(positions, dim):
    bsz, l = positions.shape
    half = dim // 2
    inv = 1.0 / (ROPE_THETA ** (jnp.arange(0, dim, 2, dtype=F32) / dim))
    inv = jnp.concatenate([inv, inv]).reshape(1, dim)
    sgn = jnp.asarray(np.concatenate([-np.ones(half), np.ones(half)]).reshape(1, dim), F32)
    tr = _tile(l, 1024)
    shp = jax.ShapeDtypeStruct((bsz, l, dim), F32)
    return pl.pallas_call(
        _rope_kernel,
        grid=(bsz, l // tr),
        in_specs=[pl.BlockSpec((1, tr, 1), lambda b, i: (b, i, 0)),
                  pl.BlockSpec((1, dim), lambda b, i: (0, 0)),
                  pl.BlockSpec((1, dim), lambda b, i: (0, 0))],
        out_specs=[pl.BlockSpec((1, tr, dim), lambda b, i: (b, i, 0))] * 2,
        out_shape=[shp, shp],
        compiler_params=_cparams(("parallel", "parallel")),
        name="rope_tables",
    )(positions.reshape(bsz, l, 1), inv, sgn)


def _proj_t_kernel(w_ref, h_ref, o_ref):
    o_ref[0] = lax.dot_general(w_ref[...].astype(BF16), h_ref[0], (((0,), (1,)), ((), ())),
                               preferred_element_type=F32).astype(o_ref.dtype)


def _proj_t(h, w, col0, n, out_dtype):
    bsz, l, d = h.shape
    tm, tn = _tile(l, 1024), _tile(math.gcd(n, col0) if col0 else n, _weight_cols(w))
    j0 = col0 // tn
    return pl.pallas_call(
        _proj_t_kernel,
        grid=(bsz, l // tm, n // tn),
        in_specs=[pl.BlockSpec((d, tn), lambda b, i, j: (0, j + j0)),
                  pl.BlockSpec((1, tm, d), lambda b, i, j: (b, i, 0))],
        out_specs=pl.BlockSpec((1, tn, tm), lambda b, i, j: (b, j, i)),
        out_shape=jax.ShapeDtypeStruct((bsz, n, l), out_dtype),
        compiler_params=_cparams(("parallel", "parallel", "arbitrary")),
        name="proj_t",
    )(w, h)


def _proj_kernel(h_ref, w_ref, *rest, rope_dim, out_scale):
    acc = jnp.dot(h_ref[0], w_ref[...].astype(BF16), preferred_element_type=F32)
    if rope_dim:
        cos_ref, sin_ref, o_ref = rest
        cos, sin = cos_ref[0], sin_ref[0]
        for g in range(acc.shape[1] // rope_dim):
            xg = acc[:, g * rope_dim:(g + 1) * rope_dim]
            yg = xg * cos + pltpu.roll(xg, rope_dim // 2, 1) * sin
            o_ref[0, :, g * rope_dim:(g + 1) * rope_dim] = (yg * out_scale).astype(o_ref.dtype)
    else:
        (o_ref,) = rest
        o_ref[0] = acc.astype(o_ref.dtype)


def _proj(h, w, col0, n, rope=None, out_scale=1.0):
    bsz, l, d = h.shape
    tm, tn = _tile(l, 1024), _tile(math.gcd(n, col0) if col0 else n, _weight_cols(w))
    j0 = col0 // tn
    in_specs = [pl.BlockSpec((1, tm, d), lambda b, i, j: (b, i, 0)),
                pl.BlockSpec((d, tn), lambda b, i, j: (0, j + j0))]
    args = [h, w]
    rope_dim = 0
    if rope is not None:
        rope_dim = rope[0].shape[-1]
        in_specs += [pl.BlockSpec((1, tm, rope_dim), lambda b, i, j: (b, i, 0))] * 2
        args += list(rope)
    return pl.pallas_call(
        functools.partial(_proj_kernel, rope_dim=rope_dim, out_scale=out_scale),
        grid=(bsz, l // tm, n // tn),
        in_specs=in_specs,
        out_specs=pl.BlockSpec((1, tm, tn), lambda b, i, j: (b, i, j)),
        out_shape=jax.ShapeDtypeStruct((bsz, l, n), BF16),
        compiler_params=_cparams(("parallel", "parallel", "arbitrary")),
        name="proj_rope" if rope_dim else "proj",
    )(*args)


def _filt_kernel(z_ref, t_ref, w1_ref, b1_ref, w2_ref, b2_ref, w3_ref, b3_ref, fr_ref,
                 w4_ref, ad_ref, o_ref):
    fr = fr_ref[...]
    h = jnp.sin(fr * (_dot3(w1_ref[...], z_ref[...]) + b1_ref[...]))
    h = jnp.sin(fr * (_dot3(w2_ref[...], h) + b2_ref[...]))
    h = jnp.sin(fr * (_dot3(w3_ref[...], h) + b3_ref[...]))
    t = t_ref[...]
    decay = jnp.exp(-(ad_ref[...] * t[0:1, :]))
    o_ref[...] = _dot3(w4_ref[0], h) * decay * t[1:2, :]


def _hyena_kernel_taps(l, c, f_w1, f_b1, f_w2, f_b2, f_w3, f_b3, f_w4, f_freq):
    n = 2 * l
    emb, hid = f_w1.shape
    bands = (emb - 1) // 2
    t = jnp.linspace(0.0, 1.0, l, dtype=F32)[:, None]
    w = 2.0 * math.pi * jnp.arange(l, dtype=F32)[:, None] / l
    f = jnp.linspace(1e-4, bands - 1, bands, dtype=F32)[None, :]
    z = jnp.concatenate([t, jnp.cos(f * w), -jnp.sin(f * w)], axis=-1)
    cols = lambda a: jnp.concatenate([a, a[:1], a[:0:-1]], axis=0).T
    embp = -(-emb // 8) * 8
    z = jnp.pad(cols(z), ((0, embp - emb), (0, 0)))
    mask = jnp.ones((1, n), F32).at[0, l].set(0.0)
    tm = jnp.concatenate([cols(t), mask], axis=0)
    min_decay = math.log(DECAY_TARGET) / SLOW_DECAY_PCT
    max_decay = math.log(DECAY_TARGET) / FAST_DECAY_PCT
    ad = jnp.abs(jnp.linspace(min_decay, max_decay, c, dtype=F32)).reshape(c, 1)
    w1t = jnp.pad(f_w1.T, ((0, 0), (0, embp - emb)))
    w4t = f_w4.T.reshape(2, c, hid)
    col = lambda a: a.reshape(hid, 1)
    tn = _tile(l, 1024)
    nt = n // tn
    full = lambda shape: pl.BlockSpec(shape, lambda i: (0,) * len(shape))
    return pl.pallas_call(
        _filt_kernel,
        grid=(nt,),
        in_specs=[pl.BlockSpec((embp, tn), lambda i: (0, i)),
                  pl.BlockSpec((2, tn), lambda i: (0, i)),
                  full((hid, embp)), full((hid, 1)), full((hid, hid)), full((hid, 1)),
                  full((hid, hid)), full((hid, 1)), full((hid, 1)),
                  pl.BlockSpec((1, c, hid), lambda i: ((2 * i) // nt, 0, 0)),
                  full((c, 1))],
        out_specs=pl.BlockSpec((c, tn), lambda i: (0, i)),
        out_shape=jax.ShapeDtypeStruct((c, n), F32),
        compiler_params=_cparams(("parallel",)),
        name="hyena_filter",
    )(z, tm, w1t, col(f_b1), f_w2.T, col(f_b2), f_w3.T, col(f_b3), col(f_freq), w4t, ad)


def _dft_consts(r, tc):
    half = r // 2
    idx = np.arange(r)
    ang = -2.0 * np.pi * np.outer(idx, idx) / r
    fr, fi = np.cos(ang), np.sin(ang)
    angt = -2.0 * np.pi * np.outer(idx, idx) / (r * r)
    fa = np.block([[fr[:, :half], -fi[:, :half]], [fi[:, :half], fr[:, :half]]])
    fk = np.concatenate([fr, fi], axis=0)
    g = np.concatenate([fr, fi], axis=1)
    fin = np.block([[fr[:half], fi[:half]], [-fi[:half], fr[:half]]]) / float(r * r)
    mx = lambda a: jnp.asarray(a, F32).astype(BF16)
    bc = lambda a: jnp.broadcast_to(mx(a)[None], (tc,) + a.shape)
    return (bc(fa), bc(fk), mx(g), jnp.asarray(np.cos(angt), F32),
            jnp.asarray(np.sin(angt), F32), bc(fin))


def _shift_prev(u, lane, row):
    nr, nl = u.shape[-2], u.shape[-1]
    a = pltpu.roll(u, 1, u.ndim - 1)
    b = pltpu.roll(a, 1, u.ndim - 2)
    p = jnp.where(lane == 0, b, a)
    return jnp.where((lane == 0) & (row == 0), 0.0, p)


def _shift_next(u, lane, row):
    nr, nl = u.shape[-2], u.shape[-1]
    a = pltpu.roll(u, nl - 1, u.ndim - 1)
    b = pltpu.roll(a, nr - 1, u.ndim - 2)
    p = jnp.where(lane == nl - 1, b, a)
    return jnp.where((lane == nl - 1) & (row == nr - 1), 0.0, p)


def _lane_stage(y2, g_ref):
    tc, r2, r = y2.shape
    p = jnp.dot(y2.reshape(tc * r2, r).astype(BF16), g_ref[...], preferred_element_type=F32)
    p = p.reshape(tc, r2, r2)
    return p[:, :r, :r], p[:, :r, r:], p[:, r:, :r], p[:, r:, r:]


def _fwd_fft(x2, f1_ref, g_ref, tr, ti):
    r = tr.shape[0]
    a = jnp.einsum("cmk,ckr->cmr", f1_ref[...], x2.astype(BF16), preferred_element_type=F32)
    ar, ai = a[:, :r], a[:, r:]
    br = ar * tr - ai * ti
    bi = ar * ti + ai * tr
    p00, p01, p10, p11 = _lane_stage(jnp.concatenate([br, bi], axis=1), g_ref)
    return p00 - p11, p01 + p10


def _hyena_kernel(x0_ref, x1_ref, v_ref, w0_ref, w1_ref, wv_ref, b0_ref, b1_ref, bv_ref,
                  hb_ref, k_ref, fa_ref, fk_ref, g_ref, tr_ref, ti_ref, fin_ref, o_ref):
    shape = x0_ref.shape[1:]
    lane = lax.broadcasted_iota(jnp.int32, shape, 2)
    row = lax.broadcasted_iota(jnp.int32, shape, 1)
    tr, ti = tr_ref[...], ti_ref[...]

    def sconv(u_ref, w_ref, b_ref, b):
        u = u_ref[b].astype(F32)
        return (b_ref[...] + w_ref[0] * _shift_prev(u, lane, row) + w_ref[1] * u
                + w_ref[2] * _shift_next(u, lane, row))

    nb = x0_ref.shape[0]
    x0 = [sconv(x0_ref, w0_ref, b0_ref, b) for b in range(nb)]
    vx = [sconv(v_ref, wv_ref, bv_ref, b) * sconv(x1_ref, w1_ref, b1_ref, b) for b in range(nb)]

    kr, ki = _fwd_fft(k_ref[...], fk_ref, g_ref, tr, ti)
    for b0 in range(0, nb, 2):
        pair = vx[b0:b0 + 2]
        xi = pair[1] if len(pair) == 2 else jnp.zeros_like(pair[0])
        sr, si = _fwd_fft(jnp.concatenate([pair[0], xi], axis=1), fa_ref, g_ref, tr, ti)
        yr = sr * kr - si * ki
        yi = sr * ki + si * kr
        p00, p01, p10, p11 = _lane_stage(jnp.concatenate([yr, yi], axis=1), g_ref)
        cr, ci = p00 + p11, p10 - p01
        dr = cr * tr + ci * ti
        di = ci * tr - cr * ti
        d2 = jnp.concatenate([dr, di], axis=1).astype(BF16)
        y = jnp.einsum("cmk,ckr->cmr", fin_ref[...], d2, preferred_element_type=F32)
        half = shape[1]
        for j, yb in enumerate((y[:, :half], y[:, half:])[:len(pair)]):
            b = b0 + j
            o_ref[b] = ((yb + vx[b] * hb_ref[...]) * x0[b]).astype(o_ref.dtype)


def _hyena(u_t, kern_t, conv_w, conv_b, hyena_bias, out_dtype):
    bsz, c3, l = u_t.shape
    c = c3 // 3
    r = int(round(math.sqrt(2 * l)))
    assert r * r == 2 * l and r % 2 == 0
    half = r // 2
    tc = _tile(c, 16)
    nc = c // tc
    consts = _dft_consts(r, tc)
    u4 = u_t.reshape(bsz, c3, half, r)
    k3 = kern_t.reshape(c, r, r)
    cw = conv_w.reshape(conv_w.shape[0], c3, 1, 1)
    cb = conv_b.reshape(c3, 1, 1)
    hb = hyena_bias.reshape(c, 1, 1)
    stream = lambda s: pl.BlockSpec((bsz, tc, half, r), lambda j, s=s: (0, j + s * nc, 0, 0))
    wspec = lambda s: pl.BlockSpec((conv_w.shape[0], tc, 1, 1), lambda j, s=s: (0, j + s * nc, 0, 0))
    bspec = lambda s: pl.BlockSpec((tc, 1, 1), lambda j, s=s: (j + s * nc, 0, 0))
    cspec = lambda a: pl.BlockSpec(a.shape, lambda j, nd=a.ndim: (0,) * nd)
    out = pl.pallas_call(
        _hyena_kernel,
        grid=(nc,),
        in_specs=[stream(0), stream(1), stream(2), wspec(0), wspec(1), wspec(2),
                  bspec(0), bspec(1), bspec(2), bspec(0),
                  pl.BlockSpec((tc, r, r), lambda j: (j, 0, 0))] + [cspec(a) for a in consts],
        out_specs=pl.BlockSpec((bsz, tc, half, r), lambda j: (0, j, 0, 0)),
        out_shape=jax.ShapeDtypeStruct((bsz, c, half, r), out_dtype),
        compiler_params=_cparams(("parallel",)),
        name="hyena",
    )(u4, u4, u4, cw, cw, cw, cb, cb, cb, hb, k3, *consts)
    return out.reshape(bsz, c, l)


def _rep(x, n):
    if n % LANES:
        return x[:, :n]
    return x if n == LANES else jnp.concatenate([x] * (n // LANES), axis=1)


def _attn_kernel(q_ref, k_ref, v_ref, lq1_ref, lk1_ref, lq2_ref, lk2_ref, g_ref, *rest,
                 ncast, hd, tk, lam_init, eps):
    cast_in, o_ref, cast_out = rest[:ncast], rest[ncast], rest[ncast + 1:2 * ncast + 1]
    s_ref, mb_ref, m_ref, l_ref, acc_ref = rest[2 * ncast + 1:]
    for src, dst in zip(cast_in, cast_out):
        c = src.shape[1]
        dst[:, :c] = src[...].astype(dst.dtype)
        if dst.shape[1] > c:
            dst[:, c:] = jnp.zeros((dst.shape[0], dst.shape[1] - c), dst.dtype)
    nkv = k_ref.shape[1] // tk
    tq = q_ref.shape[1]
    hw = v_ref.shape[2]

    def scores(t, slot):
        for j in range(2):
            qj = q_ref[0, :, j * hd:(j + 1) * hd]
            kj = k_ref[0, t * tk:(t + 1) * tk, j * hd:(j + 1) * hd]
            s = lax.dot_general(qj, kj, (((1,), (1,)), ((), ())), preferred_element_type=F32)
            s_ref[slot, j] = s
            mb_ref[slot, j] = jnp.broadcast_to(jnp.max(s, axis=-1, keepdims=True), (tq, LANES))

    def consume(t, slot):
        v = v_ref[0, t * tk:(t + 1) * tk, :]
        for j in range(2):
            m_prev = m_ref[j]
            m_new = jnp.maximum(m_prev, mb_ref[slot, j])
            alpha = jnp.exp2(m_prev - m_new)
            ps = [jnp.exp2(s_ref[slot, j, :, c * LANES:(c + 1) * LANES] - m_new)
                  for c in range(tk // LANES)]
            l_ref[j] = alpha * l_ref[j] + functools.reduce(lambda a, b: a + b, ps)
            p = jnp.concatenate(ps, axis=1).astype(v.dtype)
            acc_ref[j] = _rep(alpha, hw) * acc_ref[j] + jnp.dot(p, v, preferred_element_type=F32)
            m_ref[j] = m_new

    m_ref[...] = jnp.full(m_ref.shape, -jnp.inf, F32)
    l_ref[...] = jnp.zeros(l_ref.shape, F32)
    acc_ref[...] = jnp.zeros(acc_ref.shape, F32)
    scores(0, 0)
    for t in range(nkv):
        if t + 1 < nkv:
            scores(t + 1, (t + 1) % 2)
        consume(t, t % 2)

    lam = (jnp.exp(jnp.sum(lq1_ref[...] * lk1_ref[...], axis=-1, keepdims=True))
           - jnp.exp(jnp.sum(lq2_ref[...] * lk2_ref[...], axis=-1, keepdims=True)) + lam_init)
    l0 = jnp.sum(l_ref[0], axis=-1, keepdims=True)
    l1 = jnp.sum(l_ref[1], axis=-1, keepdims=True)
    o = acc_ref[0] / l0 - lam * (acc_ref[1] / l1)
    o = o * lax.rsqrt(jnp.mean(o * o, axis=-1, keepdims=True) + eps) * g_ref[...]
    o_ref[0] = (o * (1.0 - lam_init)).astype(o_ref.dtype)


def _cast_blocks(shape, steps, full_rows):
    r, c = shape
    for a in range(steps, 0, -1):
        b = steps // a
        if steps % a or r % a or c % b or (full_rows and b > 1):
            continue
        rb, cb = r // a, c // b
        if rb % 16 == 0 and cb % LANES == 0:
            return rb, cb, b
    return None


def _diff_attention(q, k, v, lq1, lk1, lq2, lk2, subln_g, lam_init, cast=()):
    bsz, l, width = q.shape
    hd = lq1.shape[-1]
    hw = 2 * hd
    heads = width // hw
    tq, tk = _tile(l, 512), _tile(l, 1024)
    nq = l // tq
    steps = bsz * heads * nq
    vec = lambda a: a.reshape(1, -1).astype(F32)
    vspec = lambda n: pl.BlockSpec((1, n), lambda b, h, i: (0, 0))
    plans = [_cast_blocks(a.shape, steps, pad > 0) for a, pad in cast]
    riders = [(a, pad, p) for (a, pad), p in zip(cast, plans) if p is not None]

    def cast_spec(p, pad):
        rb, cb, ncb = p
        return pl.BlockSpec((rb, cb + pad), lambda b, h, i: (((b * heads + h) * nq + i) // ncb,
                                                              ((b * heads + h) * nq + i) % ncb))

    cast_in_specs = [cast_spec(p, 0) for _, _, p in riders]
    cast_out_specs = [cast_spec(p, pad) for _, pad, p in riders]
    outs = pl.pallas_call(
        functools.partial(_attn_kernel, ncast=len(riders), hd=hd, tk=tk, lam_init=lam_init,
                          eps=SUBLN_EPS),
        grid=(bsz, heads, nq),
        in_specs=[pl.BlockSpec((1, tq, hw), lambda b, h, i: (b, i, h)),
                  pl.BlockSpec((1, l, hw), lambda b, h, i: (b, 0, h)),
                  pl.BlockSpec((1, l, hw), lambda b, h, i: (b, 0, h)),
                  vspec(hd), vspec(hd), vspec(hd), vspec(hd), vspec(hw)] + cast_in_specs,
        out_specs=[pl.BlockSpec((1, tq, hw), lambda b, h, i: (b, i, h))] + cast_out_specs,
        out_shape=[jax.ShapeDtypeStruct((bsz, l, width), BF16)]
                  + [jax.ShapeDtypeStruct((a.shape[0], a.shape[1] + pad), BF16)
                     for a, pad, _ in riders],
        scratch_shapes=[pltpu.VMEM((2, 2, tq, tk), F32), pltpu.VMEM((2, 2, tq, LANES), F32),
                        pltpu.VMEM((2, tq, LANES), F32), pltpu.VMEM((2, tq, LANES), F32),
                        pltpu.VMEM((2, tq, hw), F32)],
        compiler_params=_cparams(("parallel", "parallel", "arbitrary")),
        name="diff_attn",
    )(q, k, v, vec(lq1), vec(lk1), vec(lq2), vec(lk2), vec(subln_g), *[a for a, _, _ in riders])
    rounded = iter(outs[1:])
    return outs[0], [next(rounded) if p is not None else jnp.pad(a.astype(BF16), ((0, 0), (0, pad)))
                     for (a, pad), p in zip(cast, plans)]


def _outproj_kernel(yh_ref, yd_ref, w1_ref, w2_ref, x_ref, gt_ref, o_ref):
    acc = lax.dot_general(yh_ref[0].astype(BF16), w1_ref[...].astype(BF16),
                          (((0,), (0,)), ((), ())), preferred_element_type=F32)
    acc = acc + jnp.dot(yd_ref[0], w2_ref[...].astype(BF16), preferred_element_type=F32)
    o_ref[0] = x_ref[0] + gt_ref[0] * acc


def _outproj(yh_t, yd, w, x, gt):
    bsz, l, d = x.shape
    c, kd = yh_t.shape[1], yd.shape[2]
    assert c == kd and w.shape[0] == c + kd
    tm, tn = _tile(l, 512), _tile(d, _weight_cols(w))
    return pl.pallas_call(
        _outproj_kernel,
        grid=(bsz, l // tm, d // tn),
        in_specs=[pl.BlockSpec((1, c, tm), lambda b, i, j: (b, 0, i)),
                  pl.BlockSpec((1, tm, kd), lambda b, i, j: (b, i, 0)),
                  pl.BlockSpec((c, tn), lambda b, i, j: (0, j)),
                  pl.BlockSpec((kd, tn), lambda b, i, j: (1, j)),
                  pl.BlockSpec((1, tm, tn), lambda b, i, j: (b, i, j)),
                  pl.BlockSpec((1, 1, tn), lambda b, i, j: (b, 0, j))],
        out_specs=pl.BlockSpec((1, tm, tn), lambda b, i, j: (b, i, j)),
        out_shape=jax.ShapeDtypeStruct((bsz, l, d), F32),
        compiler_params=_cparams(("parallel", "parallel", "arbitrary")),
        name="outproj",
    )(yh_t, yd, w, w, x, gt)


def _gateup_kernel(h_ref, wg_ref, wu_ref, o_ref):
    h = h_ref[0]
    g = jnp.dot(h, wg_ref[...], preferred_element_type=F32)
    u = jnp.dot(h, wu_ref[...], preferred_element_type=F32)
    o_ref[0] = (_silu(g) * u).astype(o_ref.dtype)


def _gateup(h, wg, wu):
    bsz, l, d = h.shape
    n = wg.shape[1]
    tm, tn = _tile(l, 1024), _tile(n, 512)
    wspec = pl.BlockSpec((d, tn), lambda b, i, j: (0, j))
    return pl.pallas_call(
        _gateup_kernel,
        grid=(bsz, l // tm, n // tn),
        in_specs=[pl.BlockSpec((1, tm, d), lambda b, i, j: (b, i, 0)), wspec, wspec],
        out_specs=pl.BlockSpec((1, tm, tn), lambda b, i, j: (b, i, j)),
        out_shape=jax.ShapeDtypeStruct((bsz, l, n), BF16),
        compiler_params=_cparams(("parallel", "parallel", "arbitrary")),
        name="gateup",
    )(h, wg, wu)


def _down_kernel(g_ref, w_ref, x_ref, gt_ref, gf_ref, o_ref, *, final_eps):
    k = pl.program_id(2)
    d = o_ref.shape[2]
    tn = _tile(d, 1024)
    chunks = [slice(c * tn, (c + 1) * tn) for c in range(d // tn)]
    part = lambda cols: jnp.dot(g_ref[0], w_ref[:, cols], preferred_element_type=F32)

    @pl.when(k == 0)
    def _():
        for cols in chunks:
            o_ref[0, :, cols] = part(cols)

    @pl.when(k > 0)
    def _():
        for cols in chunks:
            o_ref[0, :, cols] += part(cols)

    @pl.when(k == pl.num_programs(2) - 1)
    def _():
        def finish(r, carry):
            rows = pl.ds(pl.multiple_of(r * EPILOGUE_ROWS, EPILOGUE_ROWS), EPILOGUE_ROWS)
            x2 = x_ref[0, rows, :] + gt_ref[0] * o_ref[0, rows, :]
            if final_eps is not None:
                ms = jnp.mean(x2 * x2, axis=-1, keepdims=True)
                x2 = x2 * lax.rsqrt(ms + final_eps) * gf_ref[...]
            o_ref[0, rows, :] = x2
            return carry

        lax.fori_loop(0, o_ref.shape[1] // EPILOGUE_ROWS, finish, 0)


def _down(g, w, x, gt, g_final, final_eps):
    bsz, l, d = x.shape
    kdim = g.shape[2]
    tm, tk = _tile(l, 512), _tile(kdim, 1024)
    return pl.pallas_call(
        functools.partial(_down_kernel, final_eps=final_eps),
        grid=(bsz, l // tm, kdim // tk),
        in_specs=[pl.BlockSpec((1, tm, tk), lambda b, i, k: (b, i, k)),
                  pl.BlockSpec((tk, d), lambda b, i, k: (k, 0)),
                  pl.BlockSpec((1, tm, d), lambda b, i, k: (b, i, 0)),
                  pl.BlockSpec((1, 1, d), lambda b, i, k: (b, 0, 0)),
                  pl.BlockSpec((1, d), lambda b, i, k: (0, 0))],
        out_specs=pl.BlockSpec((1, tm, d), lambda b, i, k: (b, i, 0)),
        out_shape=jax.ShapeDtypeStruct((bsz, l, d), F32),
        compiler_params=_cparams(("parallel", "parallel", "arbitrary")),
        name="down",
    )(g, w, x, gt, g_final.reshape(1, d))


def _pad_to(a, axis, mult):
    n = a.shape[axis]
    pad = -n % mult
    if not pad:
        return a
    widths = [(0, 0)] * a.ndim
    widths[axis] = (0, pad)
    return jnp.pad(a, widths)


def kernel(x, c, positions, w_ada, b_ada, g_mix, g_ffn, w_in, conv_w, conv_b, f_w1, f_b1, f_w2, f_b2, f_w3, f_b3, f_w4, f_freq, hyena_bias, lambda_q1, lambda_k1, lambda_q2, lambda_k2, subln_g, w_out, w_gate, w_up, w_down, g_final):
    bsz, l, d = x.shape
    depth = w_ada.shape[0]
    ch = hyena_bias.shape[-1]
    hd = lambda_q1.shape[-1]
    qk = (w_in.shape[-1] - 3 * ch - (d - ch)) // 2
    cos, sin = _rope_tables(positions, hd)
    q_scale = hd ** -0.5 * math.log2(math.e)
    for i in range(depth):
        lam_init = 0.8 - 0.6 * math.exp(-0.3 * i)
        mod = _ada(c, w_ada[i], b_ada[i])
        sh1, sc1, gt1, sh2, sc2, gt2 = [mod[:, None, j * d:(j + 1) * d] for j in range(N_MOD)]
        h = _norm_mod(x, g_mix[i], sc1, sh1, NORM_EPS)
        w = w_in[i].astype(BF16)
        o1, o2, o3 = 3 * ch, 3 * ch + qk, 3 * ch + 2 * qk
        u_t = _proj_t(h, w, 0, o1, F32)
        q = _proj(h, w, o1, qk, rope=(cos, sin), out_scale=q_scale)
        k = _proj(h, w, o2, qk, rope=(cos, sin))
        v = _proj(h, w, o3, w.shape[1] - o3)
        kern_t = _hyena_kernel_taps(l, ch, f_w1[i], f_b1[i], f_w2[i], f_b2[i], f_w3[i], f_b3[i],
                                    f_w4[i], f_freq[i])
        y_hy = _hyena(u_t, kern_t, conv_w[i], conv_b[i], hyena_bias[i], F32)
        hpad = -w_gate.shape[-1] % 1024
        y_da, (wo, wg, wu, wd) = _diff_attention(
            q, k, v, lambda_q1[i], lambda_k1[i], lambda_q2[i], lambda_k2[i], subln_g[i], lam_init,
            cast=((w_out[i], 0), (w_gate[i], hpad), (w_up[i], hpad), (w_down[i], 0)))
        x = _outproj(y_hy, y_da, wo, x, gt1)
        h = _norm_mod(x, g_ffn[i], sc2, sh2, NORM_EPS)
        g = _gateup(h, wg, wu)
        last = i == depth - 1
        x = _down(g, _pad_to(wd, 0, 1024), x, gt2, g_final, NORM_EPS if last else None)
    return x
```

```python
import functools
import math

import numpy as np
import jax
import jax.numpy as jnp
from jax import lax
from jax.experimental import pallas as pl
from jax.experimental.pallas import tpu as pltpu

F32 = jnp.float32
BF16 = jnp.bfloat16

NORM_EPS = 1e-6
SUBLN_EPS = 1e-5
ROPE_THETA = 10000.0
FAST_DECAY_PCT = 0.3
SLOW_DECAY_PCT = 1.5
DECAY_TARGET = 1e-2
N_MOD = 6

V7X_VMEM_LIMIT_BYTES = 56 * 1024 * 1024
LANES = 128
EPILOGUE_ROWS = 16


def _cparams(sem):
    return pltpu.CompilerParams(dimension_semantics=sem, vmem_limit_bytes=V7X_VMEM_LIMIT_BYTES)


def _tile(n, pref):
    t = min(n, pref)
    while n % t:
        t -= 1
    return t


def _weight_cols(w):
    return 1024 if w.dtype == BF16 else 512


def _silu(x):
    return x * (1.0 / (1.0 + jnp.exp(-x)))


def _split_bf16(a):
    hi = a.astype(BF16)
    lo = (a - hi.astype(F32)).astype(BF16)
    return hi, lo


def _dot3(a, b):
    ah, al = _split_bf16(a)
    bh, bl = _split_bf16(b)
    d = functools.partial(jnp.dot, preferred_element_type=F32)
    return d(ah, bh) + (d(ah, bl) + d(al, bh))


def _ada_kernel(c_ref, w_ref, b_ref, o_ref):
    o_ref[...] = _dot3(_silu(c_ref[...]), w_ref[...]) + b_ref[...]


def _ada(c, w, b):
    bsz, d = c.shape
    n = w.shape[1]
    rows = 8
    cp = jnp.zeros((rows, d), F32).at[:bsz].set(c)
    tn = _tile(n, 512)
    out = pl.pallas_call(
        _ada_kernel,
        grid=(n // tn,),
        in_specs=[pl.BlockSpec((rows, d), lambda j: (0, 0)),
                  pl.BlockSpec((d, tn), lambda j: (0, j)),
                  pl.BlockSpec((1, tn), lambda j: (0, j))],
        out_specs=pl.BlockSpec((rows, tn), lambda j: (0, j)),
        out_shape=jax.ShapeDtypeStruct((rows, n), F32),
        compiler_params=_cparams(("parallel",)),
        name="ada",
    )(cp, w, b.reshape(1, n))
    return out[:bsz]


def _norm_mod_kernel(x_ref, g_ref, sc_ref, sh_ref, o_ref, *, eps):
    x = x_ref[0]
    y = x * lax.rsqrt(jnp.mean(x * x, axis=-1, keepdims=True) + eps) * g_ref[...]
    o_ref[0] = (y * (1.0 + sc_ref[0]) + sh_ref[0]).astype(o_ref.dtype)


def _norm_mod(x, g, sc, sh, eps):
    bsz, l, d = x.shape
    tr = _tile(l, 512)
    return pl.pallas_call(
        functools.partial(_norm_mod_kernel, eps=eps),
        grid=(bsz, l // tr),
        in_specs=[pl.BlockSpec((1, tr, d), lambda b, i: (b, i, 0)),
                  pl.BlockSpec((1, d), lambda b, i: (0, 0)),
                  pl.BlockSpec((1, 1, d), lambda b, i: (b, 0, 0)),
                  pl.BlockSpec((1, 1, d), lambda b, i: (b, 0, 0))],
        out_specs=pl.BlockSpec((1, tr, d), lambda b, i: (b, i, 0)),
        out_shape=jax.ShapeDtypeStruct((bsz, l, d), BF16),
        compiler_params=_cparams(("parallel", "parallel")),
        name="norm_mod",
    )(x, g.reshape(1, d), sc, sh)


def _rope_kernel(pos_ref, inv_ref, sgn_ref, cos_ref, sin_ref):
    ang = pos_ref[0].astype(F32) * inv_ref[...]
    cos_ref[0] = jnp.cos(ang)
    sin_ref[0] = jnp.sin(ang) * sgn_ref[...]


def _rope_tables(positions, dim):
    bsz, l = positions.shape
    half = dim // 2
    inv = 1.0 / (ROPE_THETA ** (jnp.arange(0, dim, 2, dtype=F32) / dim))
    inv = jnp.concatenate([inv, inv]).reshape(1, dim)
    sgn = jnp.asarray(np.concatenate([-np.ones(half), np.ones(half)]).reshape(1, dim), F32)
    tr = _tile(l, 1024)
    shp = jax.ShapeDtypeStruct((bsz, l, dim), F32)
    return pl.pallas_call(
        _rope_kernel,
        grid=(bsz, l // tr),
        in_specs=[pl.BlockSpec((1, tr, 1), lambda b, i: (b, i, 0)),
                  pl.BlockSpec((1, dim), lambda b, i: (0, 0)),
                  pl.BlockSpec((1, dim), lambda b, i: (0, 0))],
        out_specs=[pl.BlockSpec((1, tr, dim), lambda b, i: (b, i, 0))] * 2,
        out_shape=[shp, shp],
        compiler_params=_cparams(("parallel", "parallel")),
        name="rope_tables",
    )(positions.reshape(bsz, l, 1), inv, sgn)


def _proj_t_kernel(w_ref, h_ref, o_ref):
    o_ref[0] = lax.dot_general(w_ref[...].astype(BF16), h_ref[0], (((0,), (1,)), ((), ())),
                               preferred_element_type=F32).astype(o_ref.dtype)


def _proj_t(h, w, col0, n, out_dtype):
    bsz, l, d = h.shape
    tm, tn = _tile(l, 1024), _tile(math.gcd(n, col0) if col0 else n, _weight_cols(w))
    j0 = col0 // tn
    return pl.pallas_call(
        _proj_t_kernel,
        grid=(bsz, l // tm, n // tn),
        in_specs=[pl.BlockSpec((d, tn), lambda b, i, j: (0, j + j0)),
                  pl.BlockSpec((1, tm, d), lambda b, i, j: (b, i, 0))],
        out_specs=pl.BlockSpec((1, tn, tm), lambda b, i, j: (b, j, i)),
        out_shape=jax.ShapeDtypeStruct((bsz, n, l), out_dtype),
        compiler_params=_cparams(("parallel", "parallel", "arbitrary")),
        name="proj_t",
    )(w, h)


def _proj_kernel(h_ref, w_ref, *rest, rope_dim, out_scale):
    acc = jnp.dot(h_ref[0], w_ref[...].astype(BF16), preferred_element_type=F32)
    if rope_dim:
        cos_ref, sin_ref, o_ref = rest
        cos, sin = cos_ref[0], sin_ref[0]
        for g in range(acc.shape[1] // rope_dim):
            xg = acc[:, g * rope_dim:(g + 1) * rope_dim]
            yg = xg * cos + pltpu.roll(xg, rope_dim // 2, 1) * sin
            o_ref[0, :, g * rope_dim:(g + 1) * rope_dim] = (yg * out_scale).astype(o_ref.dtype)
    else:
        (o_ref,) = rest
        o_ref[0] = acc.astype(o_ref.dtype)


def _proj(h, w, col0, n, rope=None, out_scale=1.0):
    bsz, l, d = h.shape
    tm, tn = _tile(l, 1024), _tile(math.gcd(n, col0) if col0 else n, _weight_cols(w))
    j0 = col0 // tn
    in_specs = [pl.BlockSpec((1, tm, d), lambda b, i, j: (b, i, 0)),
                pl.BlockSpec((d, tn), lambda b, i, j: (0, j + j0))]
    args = [h, w]
    rope_dim = 0
    if rope is not None:
        rope_dim = rope[0].shape[-1]
        in_specs += [pl.BlockSpec((1, tm, rope_dim), lambda b, i, j: (b, i, 0))] * 2
        args += list(rope)
    return pl.pallas_call(
        functools.partial(_proj_kernel, rope_dim=rope_dim, out_scale=out_scale),
        grid=(bsz, l // tm, n // tn),
        in_specs=in_specs,
        out_specs=pl.BlockSpec((1, tm, tn), lambda b, i, j: (b, i, j)),
        out_shape=jax.ShapeDtypeStruct((bsz, l, n), BF16),
        compiler_params=_cparams(("parallel", "parallel", "arbitrary")),
        name="proj_rope" if rope_dim else "proj",
    )(*args)


def _filt_kernel(z_ref, t_ref, w1_ref, b1_ref, w2_ref, b2_ref, w3_ref, b3_ref, fr_ref,
                 w4_ref, ad_ref, o_ref):
    fr = fr_ref[...]
    h = jnp.sin(fr * (_dot3(w1_ref[...], z_ref[...]) + b1_ref[...]))
    h = jnp.sin(fr * (_dot3(w2_ref[...], h) + b2_ref[...]))
    h = jnp.sin(fr * (_dot3(w3_ref[...], h) + b3_ref[...]))
    t = t_ref[...]
    decay = jnp.exp(-(ad_ref[...] * t[0:1, :]))
    o_ref[...] = _dot3(w4_ref[0], h) * decay * t[1:2, :]


def _hyena_kernel_taps(l, c, f_w1, f_b1, f_w2, f_b2, f_w3, f_b3, f_w4, f_freq):
    n = 2 * l
    emb, hid = f_w1.shape
    bands = (emb - 1) // 2
    t = jnp.linspace(0.0, 1.0, l, dtype=F32)[:, None]
    w = 2.0 * math.pi * jnp.arange(l, dtype=F32)[:, None] / l
    f = jnp.linspace(1e-4, bands - 1, bands, dtype=F32)[None, :]
    z = jnp.concatenate([t, jnp.cos(f * w), -jnp.sin(f * w)], axis=-1)
    cols = lambda a: jnp.concatenate([a, a[:1], a[:0:-1]], axis=0).T
    embp = -(-emb // 8) * 8
    z = jnp.pad(cols(z), ((0, embp - emb), (0, 0)))
    mask = jnp.ones((1, n), F32).at[0, l].set(0.0)
    tm = jnp.concatenate([cols(t), mask], axis=0)
    min_decay = math.log(DECAY_TARGET) / SLOW_DECAY_PCT
    max_decay = math.log(DECAY_TARGET) / FAST_DECAY_PCT
    ad = jnp.abs(jnp.linspace(min_decay, max_decay, c, dtype=F32)).reshape(c, 1)
    w1t = jnp.pad(f_w1.T, ((0, 0), (0, embp - emb)))
    w4t = f_w4.T.reshape(2, c, hid)
    col = lambda a: a.reshape(hid, 1)
    tn = _tile(l, 1024)
    nt = n // tn
    full = lambda shape: pl.BlockSpec(shape, lambda i: (0,) * len(shape))
    return pl.pallas_call(
        _filt_kernel,
        grid=(nt,),
        in_specs=[pl.BlockSpec((embp, tn), lambda i: (0, i)),
                  pl.BlockSpec((2, tn), lambda i: (0, i)),
                  full((hid, embp)), full((hid, 1)), full((hid, hid)), full((hid, 1)),
                  full((hid, hid)), full((hid, 1)), full((hid, 1)),
                  pl.BlockSpec((1, c, hid), lambda i: ((2 * i) // nt, 0, 0)),
                  full((c, 1))],
        out_specs=pl.BlockSpec((c, tn), lambda i: (0, i)),
        out_shape=jax.ShapeDtypeStruct((c, n), F32),
        compiler_params=_cparams(("parallel",)),
        name="hyena_filter",
    )(z, tm, w1t, col(f_b1), f_w2.T, col(f_b2), f_w3.T, col(f_b3), col(f_freq), w4t, ad)


def _dft_consts(r, tc):
    half = r // 2
    idx = np.arange(r)
    ang = -2.0 * np.pi * np.outer(idx, idx) / r
    fr, fi = np.cos(ang), np.sin(ang)
    angt = -2.0 * np.pi * np.outer(idx, idx) / (r * r)
    fa = np.block([[fr[:, :half], -fi[:, :half]], [fi[:, :half], fr[:, :half]]])
    fk = np.concatenate([fr, fi], axis=0)
    g = np.concatenate([fr, fi], axis=1)
    fin = np.block([[fr[:half], fi[:half]], [-fi[:half], fr[:half]]]) / float(r * r)
    mx = lambda a: jnp.asarray(a, F32).astype(BF16)
    bc = lambda a: jnp.broadcast_to(mx(a)[None], (tc,) + a.shape)
    return (bc(fa), bc(fk), mx(g), jnp.asarray(np.cos(angt), F32),
            jnp.asarray(np.sin(angt), F32), bc(fin))


def _shift_prev(u, lane, row):
    nr, nl = u.shape[-2], u.shape[-1]
    a = pltpu.roll(u, 1, u.ndim - 1)
    b = pltpu.roll(a, 1, u.ndim - 2)
    p = jnp.where(lane == 0, b, a)
    return jnp.where((lane == 0) & (row == 0), 0.0, p)


def _shift_next(u, lane, row):
    nr, nl = u.shape[-2], u.shape[-1]
    a = pltpu.roll(u, nl - 1, u.ndim - 1)
    b = pltpu.roll(a, nr - 1, u.ndim - 2)
    p = jnp.where(lane == nl - 1, b, a)
    return jnp.where((lane == nl - 1) & (row == nr - 1), 0.0, p)


def _lane_stage(y2, g_ref):
    tc, r2, r = y2.shape
    p = jnp.dot(y2.reshape(tc * r2, r).astype(BF16), g_ref[...], preferred_element_type=F32)
    p = p.reshape(tc, r2, r2)
    return p[:, :r, :r], p[:, :r, r:], p[:, r:, :r], p[:, r:, r:]


def _fwd_fft(x2, f1_ref, g_ref, tr, ti):
    r = tr.shape[0]
    a = jnp.einsum("cmk,ckr->cmr", f1_ref[...], x2.astype(BF16), preferred_element_type=F32)
    ar, ai = a[:, :r], a[:, r:]
    br = ar * tr - ai * ti
    bi = ar * ti + ai * tr
    p00, p01, p10, p11 = _lane_stage(jnp.concatenate([br, bi], axis=1), g_ref)
    return p00 - p11, p01 + p10


def _hyena_kernel(x0_ref, x1_ref, v_ref, w0_ref, w1_ref, wv_ref, b0_ref, b1_ref, bv_ref,
                  hb_ref, k_ref, fa_ref, fk_ref, g_ref, tr_ref, ti_ref, fin_ref, o_ref):
    shape = x0_ref.shape[1:]
    lane = lax.broadcasted_iota(jnp.int32, shape, 2)
    row = lax.broadcasted_iota(jnp.int32, shape, 1)
    tr, ti = tr_ref[...], ti_ref[...]

    def sconv(u_ref, w_ref, b_ref, b):
        u = u_ref[b].astype(F32)
        return (b_ref[...] + w_ref[0] * _shift_prev(u, lane, row) + w_ref[1] * u
                + w_ref[2] * _shift_next(u, lane, row))

    nb = x0_ref.shape[0]
    x0 = [sconv(x0_ref, w0_ref, b0_ref, b) for b in range(nb)]
    vx = [sconv(v_ref, wv_ref, bv_ref, b) * sconv(x1_ref, w1_ref, b1_ref, b) for b in range(nb)]

    kr, ki = _fwd_fft(k_ref[...], fk_ref, g_ref, tr, ti)
    for b0 in range(0, nb, 2):
        pair = vx[b0:b0 + 2]
        xi = pair[1] if len(pair) == 2 else jnp.zeros_like(pair[0])
        sr, si = _fwd_fft(jnp.concatenate([pair[0], xi], axis=1), fa_ref, g_ref, tr, ti)
        yr = sr * kr - si * ki
        yi = sr * ki + si * kr
        p00, p01, p10, p11 = _lane_stage(jnp.concatenate([yr, yi], axis=1), g_ref)
        cr, ci = p00 + p11, p10 - p01
        dr = cr * tr + ci * ti
        di = ci * tr - cr * ti
        d2 = jnp.concatenate([dr, di], axis=1).astype(BF16)
        y = jnp.einsum("cmk,ckr->cmr", fin_ref[...], d2, preferred_element_type=F32)
        half = shape[1]
        for j, yb in enumerate((y[:, :half], y[:, half:])[:len(pair)]):
            b = b0 + j
            res = (yb + vx[b] * hb_ref[...]) * x0[b]
            o_ref[b] = res.reshape(res.shape[0], res.shape[1] * res.shape[2]).astype(o_ref.dtype)


def _hyena(u_t, kern_t, conv_w, conv_b, hyena_bias, out_dtype):
    bsz, c3, l = u_t.shape
    c = c3 // 3
    r = int(round(math.sqrt(2 * l)))
    assert r * r == 2 * l and r % 2 == 0
    half = r // 2
    tc = _tile(c, 16)
    nc = c // tc
    consts = _dft_consts(r, tc)
    u4 = u_t.reshape(bsz, c3, half, r)
    k3 = kern_t.reshape(c, r, r)
    cw = conv_w.reshape(conv_w.shape[0], c3, 1, 1)
    cb = conv_b.reshape(c3, 1, 1)
    hb = hyena_bias.reshape(c, 1, 1)
    stream = lambda s: pl.BlockSpec((bsz, tc, half, r), lambda j, s=s: (0, j + s * nc, 0, 0))
    wspec = lambda s: pl.BlockSpec((conv_w.shape[0], tc, 1, 1), lambda j, s=s: (0, j + s * nc, 0, 0))
    bspec = lambda s: pl.BlockSpec((tc, 1, 1), lambda j, s=s: (j + s * nc, 0, 0))
    cspec = lambda a: pl.BlockSpec(a.shape, lambda j, nd=a.ndim: (0,) * nd)
    return pl.pallas_call(
        _hyena_kernel,
        grid=(nc,),
        in_specs=[stream(0), stream(1), stream(2), wspec(0), wspec(1), wspec(2),
                  bspec(0), bspec(1), bspec(2), bspec(0),
                  pl.BlockSpec((tc, r, r), lambda j: (j, 0, 0))] + [cspec(a) for a in consts],
        out_specs=pl.BlockSpec((bsz, tc, l), lambda j: (0, j, 0)),
        out_shape=jax.ShapeDtypeStruct((bsz, c, l), out_dtype),
        compiler_params=_cparams(("parallel",)),
        name="hyena",
    )(u4, u4, u4, cw, cw, cw, cb, cb, cb, hb, k3, *consts)


def _rep(x, n):
    if n % LANES:
        return x[:, :n]
    return x if n == LANES else jnp.concatenate([x] * (n // LANES), axis=1)


def _attn_kernel(q_ref, k_ref, v_ref, lq1_ref, lk1_ref, lq2_ref, lk2_ref, g_ref, *rest,
                 ncast, hd, tk, lam_init, eps):
    cast_in, o_ref, cast_out = rest[:ncast], rest[ncast], rest[ncast + 1:2 * ncast + 1]
    s_ref, mb_ref, m_ref, l_ref, acc_ref = rest[2 * ncast + 1:]
    for src, dst in zip(cast_in, cast_out):
        c = src.shape[1]
        dst[:, :c] = src[...].astype(dst.dtype)
        if dst.shape[1] > c:
            dst[:, c:] = jnp.zeros((dst.shape[0], dst.shape[1] - c), dst.dtype)
    nkv = k_ref.shape[1] // tk
    tq = q_ref.shape[1]
    hw = v_ref.shape[2]

    def scores(t, slot):
        for j in range(2):
            qj = q_ref[0, :, j * hd:(j + 1) * hd]
            kj = k_ref[0, t * tk:(t + 1) * tk, j * hd:(j + 1) * hd]
            s = lax.dot_general(qj, kj, (((1,), (1,)), ((), ())), preferred_element_type=F32)
            s_ref[slot, j] = s
            mb_ref[slot, j] = jnp.broadcast_to(jnp.max(s, axis=-1, keepdims=True), (tq, LANES))

    def consume(t, slot):
        v = v_ref[0, t * tk:(t + 1) * tk, :]
        for j in range(2):
            m_prev = m_ref[j]
            m_new = jnp.maximum(m_prev, mb_ref[slot, j])
            alpha = jnp.exp2(m_prev - m_new)
            ps = [jnp.exp2(s_ref[slot, j, :, c * LANES:(c + 1) * LANES] - m_new)
                  for c in range(tk // LANES)]
            l_ref[j] = alpha * l_ref[j] + functools.reduce(lambda a, b: a + b, ps)
            p = jnp.concatenate(ps, axis=1).astype(v.dtype)
            acc_ref[j] = _rep(alpha, hw) * acc_ref[j] + jnp.dot(p, v, preferred_element_type=F32)
            m_ref[j] = m_new

    m_ref[...] = jnp.full(m_ref.shape, -jnp.inf, F32)
    l_ref[...] = jnp.zeros(l_ref.shape, F32)
    acc_ref[...] = jnp.zeros(acc_ref.shape, F32)
    scores(0, 0)
    for t in range(nkv):
        if t + 1 < nkv:
            scores(t + 1, (t + 1) % 2)
        consume(t, t % 2)

    lam = (jnp.exp(jnp.sum(lq1_ref[...] * lk1_ref[...], axis=-1, keepdims=True))
           - jnp.exp(jnp.sum(lq2_ref[...] * lk2_ref[...], axis=-1, keepdims=True)) + lam_init)
    l0 = jnp.sum(l_ref[0], axis=-1, keepdims=True)
    l1 = jnp.sum(l_ref[1], axis=-1, keepdims=True)
    o = acc_ref[0] / l0 - lam * (acc_ref[1] / l1)
    o = o * lax.rsqrt(jnp.mean(o * o, axis=-1, keepdims=True) + eps) * g_ref[...]
    o_ref[0] = (o * (1.0 - lam_init)).astype(o_ref.dtype)


def _cast_blocks(shape, steps, full_rows):
    r, c = shape
    for a in range(steps, 0, -1):
        b = steps // a
        if steps % a or r % a or c % b or (full_rows and b > 1):
            continue
        rb, cb = r // a, c // b
        if rb % 16 == 0 and cb % LANES == 0:
            return rb, cb, b
    return None


def _diff_attention(q, k, v, lq1, lk1, lq2, lk2, subln_g, lam_init, cast=()):
    bsz, l, width = q.shape
    hd = lq1.shape[-1]
    hw = 2 * hd
    heads = width // hw
    tq, tk = _tile(l, 512), _tile(l, 1024)
    nq = l // tq
    steps = bsz * heads * nq
    vec = lambda a: a.reshape(1, -1).astype(F32)
    vspec = lambda n: pl.BlockSpec((1, n), lambda b, h, i: (0, 0))
    plans = [_cast_blocks(a.shape, steps, pad > 0) for a, pad in cast]
    riders = [(a, pad, p) for (a, pad), p in zip(cast, plans) if p is not None]

    def cast_spec(p, pad):
        rb, cb, ncb = p
        return pl.BlockSpec((rb, cb + pad), lambda b, h, i: (((b * heads + h) * nq + i) // ncb,
                                                              ((b * heads + h) * nq + i) % ncb))

    cast_in_specs = [cast_spec(p, 0) for _, _, p in riders]
    cast_out_specs = [cast_spec(p, pad) for _, pad, p in riders]
    outs = pl.pallas_call(
        functools.partial(_attn_kernel, ncast=len(riders), hd=hd, tk=tk, lam_init=lam_init,
                          eps=SUBLN_EPS),
        grid=(bsz, heads, nq),
        in_specs=[pl.BlockSpec((1, tq, hw), lambda b, h, i: (b, i, h)),
                  pl.BlockSpec((1, l, hw), lambda b, h, i: (b, 0, h)),
                  pl.BlockSpec((1, l, hw), lambda b, h, i: (b, 0, h)),
                  vspec(hd), vspec(hd), vspec(hd), vspec(hd), vspec(hw)] + cast_in_specs,
        out_specs=[pl.BlockSpec((1, tq, hw), lambda b, h, i: (b, i, h))] + cast_out_specs,
        out_shape=[jax.ShapeDtypeStruct((bsz, l, width), BF16)]
                  + [jax.ShapeDtypeStruct((a.shape[0], a.shape[1] + pad), BF16)
                     for a, pad, _ in riders],
        scratch_shapes=[pltpu.VMEM((2, 2, tq, tk), F32), pltpu.VMEM((2, 2, tq, LANES), F32),
                        pltpu.VMEM((2, tq, LANES), F32), pltpu.VMEM((2, tq, LANES), F32),
                        pltpu.VMEM((2, tq, hw), F32)],
        compiler_params=_cparams(("parallel", "parallel", "arbitrary")),
        name="diff_attn",
    )(q, k, v, vec(lq1), vec(lk1), vec(lq2), vec(lk2), vec(subln_g), *[a for a, _, _ in riders])
    rounded = iter(outs[1:])
    return outs[0], [next(rounded) if p is not None else jnp.pad(a.astype(BF16), ((0, 0), (0, pad)))
                     for (a, pad), p in zip(cast, plans)]


def _outproj_kernel(yh_ref, yd_ref, w1_ref, w2_ref, x_ref, gt_ref, o_ref):
    acc = lax.dot_general(yh_ref[0], w1_ref[...].astype(BF16), (((0,), (0,)), ((), ())),
                          preferred_element_type=F32)
    acc = acc + jnp.dot(yd_ref[0], w2_ref[...].astype(BF16), preferred_element_type=F32)
    o_ref[0] = x_ref[0] + gt_ref[0] * acc


def _outproj(yh_t, yd, w, x, gt):
    bsz, l, d = x.shape
    c, kd = yh_t.shape[1], yd.shape[2]
    assert c == kd and w.shape[0] == c + kd
    tm, tn = _tile(l, 1024), _tile(d, _weight_cols(w))
    return pl.pallas_call(
        _outproj_kernel,
        grid=(bsz, l // tm, d // tn),
        in_specs=[pl.BlockSpec((1, c, tm), lambda b, i, j: (b, 0, i)),
                  pl.BlockSpec((1, tm, kd), lambda b, i, j: (b, i, 0)),
                  pl.BlockSpec((c, tn), lambda b, i, j: (0, j)),
                  pl.BlockSpec((kd, tn), lambda b, i, j: (1, j)),
                  pl.BlockSpec((1, tm, tn), lambda b, i, j: (b, i, j)),
                  pl.BlockSpec((1, 1, tn), lambda b, i, j: (b, 0, j))],
        out_specs=pl.BlockSpec((1, tm, tn), lambda b, i, j: (b, i, j)),
        out_shape=jax.ShapeDtypeStruct((bsz, l, d), F32),
        compiler_params=_cparams(("parallel", "parallel", "arbitrary")),
        name="outproj",
    )(yh_t, yd, w, w, x, gt)


def _gateup_kernel(h_ref, wg_ref, wu_ref, *rest, src_blocks):
    if src_blocks:
        src_ref, o_ref, dst_ref = rest
        step = (pl.program_id(0) * pl.num_programs(1) + pl.program_id(1)) * pl.num_programs(2) \
            + pl.program_id(2)

        @pl.when(step < src_blocks)
        def _():
            dst_ref[...] = src_ref[...].astype(dst_ref.dtype)

        @pl.when(step >= src_blocks)
        def _():
            dst_ref[...] = jnp.zeros(dst_ref.shape, dst_ref.dtype)
    else:
        (o_ref,) = rest
    h = h_ref[0]
    g = jnp.dot(h, wg_ref[...], preferred_element_type=F32)
    u = jnp.dot(h, wu_ref[...], preferred_element_type=F32)
    o_ref[0] = (_silu(g) * u).astype(o_ref.dtype)


def _gateup(h, wg, wu, wd):
    bsz, l, d = h.shape
    n = wg.shape[1]
    tm, tn = _tile(l, 1024), _tile(n, 512)
    grid = (bsz, l // tm, n // tn)
    steps = grid[0] * grid[1] * grid[2]
    wspec = pl.BlockSpec((d, tn), lambda b, i, j: (0, j))
    in_specs = [pl.BlockSpec((1, tm, d), lambda b, i, j: (b, i, 0)), wspec, wspec]
    out_specs = [pl.BlockSpec((1, tm, tn), lambda b, i, j: (b, i, j))]
    out_shape = [jax.ShapeDtypeStruct((bsz, l, n), BF16)]
    args = [h, wg, wu]
    rb = n // steps
    rides = n % steps == 0 and rb % 16 == 0 and wd.shape[0] % rb == 0
    src_blocks = wd.shape[0] // rb if rides else 0
    if rides:
        lin = lambda b, i, j: (b * grid[1] + i) * grid[2] + j
        in_specs.append(pl.BlockSpec((rb, wd.shape[1]),
                                     lambda b, i, j: (jnp.minimum(lin(b, i, j), src_blocks - 1), 0)))
        out_specs.append(pl.BlockSpec((rb, wd.shape[1]), lambda b, i, j: (lin(b, i, j), 0)))
        out_shape.append(jax.ShapeDtypeStruct((n, wd.shape[1]), BF16))
        args.append(wd)
    outs = pl.pallas_call(
        functools.partial(_gateup_kernel, src_blocks=src_blocks),
        grid=grid,
        in_specs=in_specs,
        out_specs=out_specs,
        out_shape=out_shape,
        compiler_params=_cparams(("parallel", "parallel", "arbitrary")),
        name="gateup",
    )(*args)
    if rides:
        return outs[0], outs[1]
    return outs[0], jnp.pad(wd.astype(BF16), ((0, n - wd.shape[0]), (0, 0)))


def _down_kernel(g_ref, w_ref, x_ref, gt_ref, gf_ref, o_ref, *, final_eps):
    k = pl.program_id(2)
    d = o_ref.shape[2]
    tn = _tile(d, 1024)
    chunks = [slice(c * tn, (c + 1) * tn) for c in range(d // tn)]
    part = lambda cols: jnp.dot(g_ref[0], w_ref[:, cols], preferred_element_type=F32)

    @pl.when(k == 0)
    def _():
        for cols in chunks:
            o_ref[0, :, cols] = part(cols)

    @pl.when(k > 0)
    def _():
        for cols in chunks:
            o_ref[0, :, cols] += part(cols)

    @pl.when(k == pl.num_programs(2) - 1)
    def _():
        def finish(r, carry):
            rows = pl.ds(pl.multiple_of(r * EPILOGUE_ROWS, EPILOGUE_ROWS), EPILOGUE_ROWS)
            x2 = x_ref[0, rows, :] + gt_ref[0] * o_ref[0, rows, :]
            if final_eps is not None:
                ms = jnp.mean(x2 * x2, axis=-1, keepdims=True)
                x2 = x2 * lax.rsqrt(ms + final_eps) * gf_ref[...]
            o_ref[0, rows, :] = x2
            return carry

        lax.fori_loop(0, o_ref.shape[1] // EPILOGUE_ROWS, finish, 0)


def _down(g, w, x, gt, g_final, final_eps):
    bsz, l, d = x.shape
    kdim = g.shape[2]
    tm, tk = _tile(l, 512), _tile(kdim, 1024)
    return pl.pallas_call(
        functools.partial(_down_kernel, final_eps=final_eps),
        grid=(bsz, l // tm, kdim // tk),
        in_specs=[pl.BlockSpec((1, tm, tk), lambda b, i, k: (b, i, k)),
                  pl.BlockSpec((tk, d), lambda b, i, k: (k, 0)),
                  pl.BlockSpec((1, tm, d), lambda b, i, k: (b, i, 0)),
                  pl.BlockSpec((1, 1, d), lambda b, i, k: (b, 0, 0)),
                  pl.BlockSpec((1, d), lambda b, i, k: (0, 0))],
        out_specs=pl.BlockSpec((1, tm, d), lambda b, i, k: (b, i, 0)),
        out_shape=jax.ShapeDtypeStruct((bsz, l, d), F32),
        compiler_params=_cparams(("parallel", "parallel", "arbitrary")),
        name="down",
    )(g, w, x, gt, g_final.reshape(1, d))


def kernel(x, c, positions, w_ada, b_ada, g_mix, g_ffn, w_in, conv_w, conv_b, f_w1, f_b1, f_w2, f_b2, f_w3, f_b3, f_w4, f_freq, hyena_bias, lambda_q1, lambda_k1, lambda_q2, lambda_k2, subln_g, w_out, w_gate, w_up, w_down, g_final):
    bsz, l, d = x.shape
    depth = w_ada.shape[0]
    ch = hyena_bias.shape[-1]
    hd = lambda_q1.shape[-1]
    qk = (w_in.shape[-1] - 3 * ch - (d - ch)) // 2
    cos, sin = _rope_tables(positions, hd)
    q_scale = hd ** -0.5 * math.log2(math.e)
    for i in range(depth):
        lam_init = 0.8 - 0.6 * math.exp(-0.3 * i)
        mod = _ada(c, w_ada[i], b_ada[i])
        sh1, sc1, gt1, sh2, sc2, gt2 = [mod[:, None, j * d:(j + 1) * d] for j in range(N_MOD)]
        h = _norm_mod(x, g_mix[i], sc1, sh1, NORM_EPS)
        w = w_in[i].astype(BF16)
        o1, o2, o3 = 3 * ch, 3 * ch + qk, 3 * ch + 2 * qk
        u_t = _proj_t(h, w, 0, o1, F32)
        k = _proj(h, w, o2, qk, rope=(cos, sin))
        q = _proj(h, w, o1, qk, rope=(cos, sin), out_scale=q_scale)
        v = _proj(h, w, o3, w.shape[1] - o3)
        kern_t = _hyena_kernel_taps(l, ch, f_w1[i], f_b1[i], f_w2[i], f_b2[i], f_w3[i], f_b3[i],
                                    f_w4[i], f_freq[i])
        y_hy = _hyena(u_t, kern_t, conv_w[i], conv_b[i], hyena_bias[i], BF16)
        hpad = -w_gate.shape[-1] % 1024
        y_da, (wo, wg, wu) = _diff_attention(
            q, k, v, lambda_q1[i], lambda_k1[i], lambda_q2[i], lambda_k2[i], subln_g[i], lam_init,
            cast=((w_out[i], 0), (w_gate[i], hpad), (w_up[i], hpad)))
        x = _outproj(y_hy, y_da, wo, x, gt1)
        h = _norm_mod(x, g_ffn[i], sc2, sh2, NORM_EPS)
        g, wd = _gateup(h, wg, wu, w_down[i])
        last = i == depth - 1
        x = _down(g, wd, x, gt2, g_final, NORM_EPS if last else None)
    return x
```

```python
import functools
import math

import numpy as np
import jax
import jax.numpy as jnp
from jax import lax
from jax.experimental import pallas as pl
from jax.experimental.pallas import tpu as pltpu

F32 = jnp.float32
BF16 = jnp.bfloat16

NORM_EPS = 1e-6
SUBLN_EPS = 1e-5
ROPE_THETA = 10000.0
FAST_DECAY_PCT = 0.3
SLOW_DECAY_PCT = 1.5
DECAY_TARGET = 1e-2
N_MOD = 6

V7X_VMEM_LIMIT_BYTES = 56 * 1024 * 1024
LANES = 128
EPILOGUE_ROWS = 16
ADA_TILE_BYTES = 13 * 512 * 1024


def _cparams(sem):
    return pltpu.CompilerParams(dimension_semantics=sem, vmem_limit_bytes=V7X_VMEM_LIMIT_BYTES)


def _tile(n, pref):
    t = min(n, pref)
    while n % t:
        t -= 1
    return t


def _weight_cols(w):
    return 1024 if w.dtype == BF16 else 512


def _silu(x):
    return x * (1.0 / (1.0 + jnp.exp(-x)))


def _split_bf16(a):
    hi = a.astype(BF16)
    lo = (a - hi.astype(F32)).astype(BF16)
    return hi, lo


def _dot3(a, b):
    ah, al = _split_bf16(a)
    bh, bl = _split_bf16(b)
    d = functools.partial(jnp.dot, preferred_element_type=F32)
    return d(ah, bh) + (d(ah, bl) + d(al, bh))


def _ada_kernel(c_ref, w_ref, b_ref, o_ref):
    o_ref[...] = _dot3(_silu(c_ref[...]), w_ref[...]) + b_ref[...]


def _norm_mod_kernel(x_ref, g_ref, sc_ref, sh_ref, o_ref, *, eps):
    x = x_ref[0]
    y = x * lax.rsqrt(jnp.mean(x * x, axis=-1, keepdims=True) + eps) * g_ref[...]
    o_ref[0] = (y * (1.0 + sc_ref[0]) + sh_ref[0]).astype(o_ref.dtype)


def _norm_mod(x, g, sc, sh, eps):
    bsz, l, d = x.shape
    tr = _tile(l, 512)
    return pl.pallas_call(
        functools.partial(_norm_mod_kernel, eps=eps),
        grid=(bsz, l // tr),
        in_specs=[pl.BlockSpec((1, tr, d), lambda b, i: (b, i, 0)),
                  pl.BlockSpec((1, d), lambda b, i: (0, 0)),
                  pl.BlockSpec((1, 1, d), lambda b, i: (b, 0, 0)),
                  pl.BlockSpec((1, 1, d), lambda b, i: (b, 0, 0))],
        out_specs=pl.BlockSpec((1, tr, d), lambda b, i: (b, i, 0)),
        out_shape=jax.ShapeDtypeStruct((bsz, l, d), BF16),
        compiler_params=_cparams(("parallel", "parallel")),
        name="norm_mod",
    )(x, g.reshape(1, d), sc, sh)


def _rope_kernel(pos_ref, inv_ref, sgn_ref, cos_ref, sin_ref):
    ang = pos_ref[0].astype(F32) * inv_ref[...]
    cos_ref[0] = jnp.cos(ang)
    sin_ref[0] = jnp.sin(ang) * sgn_ref[...]


def _rope_tables(positions, dim):
    bsz, l = positions.shape
    half = dim // 2
    inv = 1.0 / (ROPE_THETA ** (jnp.arange(0, dim, 2, dtype=F32) / dim))
    inv = jnp.concatenate([inv, inv]).reshape(1, dim)
    sgn = jnp.asarray(np.concatenate([-np.ones(half), np.ones(half)]).reshape(1, dim), F32)
    tr = _tile(l, 1024)
    shp = jax.ShapeDtypeStruct((bsz, l, dim), F32)
    return pl.pallas_call(
        _rope_kernel,
        grid=(bsz, l // tr),
        in_specs=[pl.BlockSpec((1, tr, 1), lambda b, i: (b, i, 0)),
                  pl.BlockSpec((1, dim), lambda b, i: (0, 0)),
                  pl.BlockSpec((1, dim), lambda b, i: (0, 0))],
        out_specs=[pl.BlockSpec((1, tr, dim), lambda b, i: (b, i, 0))] * 2,
        out_shape=[shp, shp],
        compiler_params=_cparams(("parallel", "parallel")),
        name="rope_tables",
    )(positions.reshape(bsz, l, 1), inv, sgn)


def _proj_t_kernel(w_ref, h_ref, o_ref):
    o_ref[0] = lax.dot_general(w_ref[...].astype(BF16), h_ref[0], (((0,), (1,)), ((), ())),
                               preferred_element_type=F32).astype(o_ref.dtype)


def _proj_t(h, w, col0, n, out_dtype):
    bsz, l, d = h.shape
    tm, tn = _tile(l, 1024), _tile(math.gcd(n, col0) if col0 else n, _weight_cols(w))
    j0 = col0 // tn
    return pl.pallas_call(
        _proj_t_kernel,
        grid=(bsz, l // tm, n // tn),
        in_specs=[pl.BlockSpec((d, tn), lambda b, i, j: (0, j + j0)),
                  pl.BlockSpec((1, tm, d), lambda b, i, j: (b, i, 0))],
        out_specs=pl.BlockSpec((1, tn, tm), lambda b, i, j: (b, j, i)),
        out_shape=jax.ShapeDtypeStruct((bsz, n, l), out_dtype),
        compiler_params=_cparams(("parallel", "parallel", "arbitrary")),
        name="proj_t",
    )(w, h)


def _proj_kernel(h_ref, w_ref, *rest, rope_dim):
    acc = jnp.dot(h_ref[0], w_ref[...].astype(BF16), preferred_element_type=F32)
    if rope_dim:
        cos_ref, sin_ref, scale_ref, o_ref = rest
        cos, sin = cos_ref[0], sin_ref[0]
        for g in range(acc.shape[1] // rope_dim):
            cols = slice(g * rope_dim, (g + 1) * rope_dim)
            xg = acc[:, cols]
            yg = xg * cos + pltpu.roll(xg, rope_dim // 2, 1) * sin
            o_ref[0, :, cols] = (yg * scale_ref[:, cols]).astype(o_ref.dtype)
    else:
        (o_ref,) = rest
        o_ref[0] = acc.astype(o_ref.dtype)


def _proj(h, w, col0, n, rope=None, col_scale=None):
    bsz, l, d = h.shape
    tm, tn = _tile(l, 1024), _tile(math.gcd(n, col0) if col0 else n, _weight_cols(w))
    j0 = col0 // tn
    in_specs = [pl.BlockSpec((1, tm, d), lambda b, i, j: (b, i, 0)),
                pl.BlockSpec((d, tn), lambda b, i, j: (0, j + j0))]
    args = [h, w]
    rope_dim = 0
    if rope is not None:
        rope_dim = rope[0].shape[-1]
        in_specs += [pl.BlockSpec((1, tm, rope_dim), lambda b, i, j: (b, i, 0))] * 2
        in_specs += [pl.BlockSpec((1, tn), lambda b, i, j: (0, j))]
        args += [*rope, col_scale]
    return pl.pallas_call(
        functools.partial(_proj_kernel, rope_dim=rope_dim),
        grid=(bsz, l // tm, n // tn),
        in_specs=in_specs,
        out_specs=pl.BlockSpec((1, tm, tn), lambda b, i, j: (b, i, j)),
        out_shape=jax.ShapeDtypeStruct((bsz, l, n), BF16),
        compiler_params=_cparams(("parallel", "parallel", "arbitrary")),
        name="proj_rope" if rope_dim else "proj",
    )(*args)


def _filt_kernel(z_ref, t_ref, w1_ref, b1_ref, w2_ref, b2_ref, w3_ref, b3_ref, fr_ref,
                 w4_ref, ad_ref, o_ref):
    fr = fr_ref[...]
    h = jnp.sin(fr * (_dot3(w1_ref[...], z_ref[...]) + b1_ref[...]))
    h = jnp.sin(fr * (_dot3(w2_ref[...], h) + b2_ref[...]))
    h = jnp.sin(fr * (_dot3(w3_ref[...], h) + b3_ref[...]))
    t = t_ref[...]
    decay = jnp.exp(-(ad_ref[...] * t[0:1, :]))
    o_ref[...] = _dot3(w4_ref[0], h) * decay * t[1:2, :]


def _ada_filt_kernel(c_ref, wa_ref, ba_ref, *rest):
    mod_ref, kern_ref = rest[-2:]
    _ada_kernel(c_ref, wa_ref, ba_ref, mod_ref)
    _filt_kernel(*rest[:-2], kern_ref)


def _ada_and_taps(cvec, w_ada, b_ada, l, c, f_w1, f_b1, f_w2, f_b2, f_w3, f_b3, f_w4, f_freq):
    n = 2 * l
    emb, hid = f_w1.shape
    bands = (emb - 1) // 2
    t = jnp.linspace(0.0, 1.0, l, dtype=F32)[:, None]
    w = 2.0 * math.pi * jnp.arange(l, dtype=F32)[:, None] / l
    f = jnp.linspace(1e-4, bands - 1, bands, dtype=F32)[None, :]
    z = jnp.concatenate([t, jnp.cos(f * w), -jnp.sin(f * w)], axis=-1)
    cols = lambda a: jnp.concatenate([a, a[:1], a[:0:-1]], axis=0).T
    embp = -(-emb // 8) * 8
    z = jnp.pad(cols(z), ((0, embp - emb), (0, 0)))
    mask = jnp.ones((1, n), F32).at[0, l].set(0.0)
    tm = jnp.concatenate([cols(t), mask], axis=0)
    min_decay = math.log(DECAY_TARGET) / SLOW_DECAY_PCT
    max_decay = math.log(DECAY_TARGET) / FAST_DECAY_PCT
    ad = jnp.abs(jnp.linspace(min_decay, max_decay, c, dtype=F32)).reshape(c, 1)
    w1t = jnp.pad(f_w1.T, ((0, 0), (0, embp - emb)))
    w4t = f_w4.T.reshape(2, c, hid)
    col = lambda a: a.reshape(hid, 1)
    bsz, d = cvec.shape
    n_mod = w_ada.shape[1]
    common = math.gcd(n_mod // LANES, n // LANES)
    nt = next((s for s in range(2, common + 1, 2)
               if common % s == 0 and (n_mod // s) * d * 4 <= ADA_TILE_BYTES), common)
    tn, tn_mod = n // nt, n_mod // nt
    rows = 8
    cp = jnp.zeros((rows, d), F32).at[:bsz].set(cvec)
    full = lambda shape: pl.BlockSpec(shape, lambda i: (0,) * len(shape))
    mod, kern = pl.pallas_call(
        _ada_filt_kernel,
        grid=(nt,),
        in_specs=[full((rows, d)),
                  pl.BlockSpec((d, tn_mod), lambda i: (0, i)),
                  pl.BlockSpec((1, tn_mod), lambda i: (0, i)),
                  pl.BlockSpec((embp, tn), lambda i: (0, i)),
                  pl.BlockSpec((2, tn), lambda i: (0, i)),
                  full((hid, embp)), full((hid, 1)), full((hid, hid)), full((hid, 1)),
                  full((hid, hid)), full((hid, 1)), full((hid, 1)),
                  pl.BlockSpec((1, c, hid), lambda i: ((2 * i) // nt, 0, 0)),
                  full((c, 1))],
        out_specs=[pl.BlockSpec((rows, tn_mod), lambda i: (0, i)),
                   pl.BlockSpec((c, tn), lambda i: (0, i))],
        out_shape=[jax.ShapeDtypeStruct((rows, n_mod), F32), jax.ShapeDtypeStruct((c, n), F32)],
        compiler_params=_cparams(("parallel",)),
        name="ada_filter",
    )(cp, w_ada, b_ada.reshape(1, n_mod), z, tm, w1t, col(f_b1), f_w2.T, col(f_b2), f_w3.T,
      col(f_b3), col(f_freq), w4t, ad)
    return mod[:bsz], kern


def _dft_consts(r, tc):
    half = r // 2
    idx = np.arange(r)
    ang = -2.0 * np.pi * np.outer(idx, idx) / r
    fr, fi = np.cos(ang), np.sin(ang)
    angt = -2.0 * np.pi * np.outer(idx, idx) / (r * r)
    fa = np.block([[fr[:, :half], -fi[:, :half]], [fi[:, :half], fr[:, :half]]])
    fk = np.concatenate([fr, fi], axis=0)
    g = np.concatenate([fr, fi], axis=1)
    fin = np.block([[fr[:half], fi[:half]], [-fi[:half], fr[:half]]]) / float(r * r)
    mx = lambda a: jnp.asarray(a, F32).astype(BF16)
    bc = lambda a: jnp.broadcast_to(mx(a)[None], (tc,) + a.shape)
    return (bc(fa), bc(fk), mx(g), jnp.asarray(np.cos(angt), F32),
            jnp.asarray(np.sin(angt), F32), bc(fin))


def _shift_prev(u, lane, row):
    nr, nl = u.shape[-2], u.shape[-1]
    a = pltpu.roll(u, 1, u.ndim - 1)
    b = pltpu.roll(a, 1, u.ndim - 2)
    p = jnp.where(lane == 0, b, a)
    return jnp.where((lane == 0) & (row == 0), 0.0, p)


def _shift_next(u, lane, row):
    nr, nl = u.shape[-2], u.shape[-1]
    a = pltpu.roll(u, nl - 1, u.ndim - 1)
    b = pltpu.roll(a, nr - 1, u.ndim - 2)
    p = jnp.where(lane == nl - 1, b, a)
    return jnp.where((lane == nl - 1) & (row == nr - 1), 0.0, p)


def _lane_stage(y2, g_ref):
    tc, r2, r = y2.shape
    p = jnp.dot(y2.reshape(tc * r2, r).astype(BF16), g_ref[...], preferred_element_type=F32)
    p = p.reshape(tc, r2, r2)
    return p[:, :r, :r], p[:, :r, r:], p[:, r:, :r], p[:, r:, r:]


def _fwd_fft(x2, f1_ref, g_ref, tr, ti):
    r = tr.shape[0]
    a = jnp.einsum("cmk,ckr->cmr", f1_ref[...], x2.astype(BF16), preferred_element_type=F32)
    ar, ai = a[:, :r], a[:, r:]
    br = ar * tr - ai * ti
    bi = ar * ti + ai * tr
    p00, p01, p10, p11 = _lane_stage(jnp.concatenate([br, bi], axis=1), g_ref)
    return p00 - p11, p01 + p10


def _hyena_kernel(x0_ref, x1_ref, v_ref, w0_ref, w1_ref, wv_ref, b0_ref, b1_ref, bv_ref,
                  hb_ref, k_ref, fa_ref, fk_ref, g_ref, tr_ref, ti_ref, fin_ref, o_ref):
    shape = x0_ref.shape[1:]
    lane = lax.broadcasted_iota(jnp.int32, shape, 2)
    row = lax.broadcasted_iota(jnp.int32, shape, 1)
    tr, ti = tr_ref[...], ti_ref[...]

    def sconv(u_ref, w_ref, b_ref, b):
        u = u_ref[b].astype(F32)
        return (b_ref[...] + w_ref[0] * _shift_prev(u, lane, row) + w_ref[1] * u
                + w_ref[2] * _shift_next(u, lane, row))

    nb = x0_ref.shape[0]
    x0 = [sconv(x0_ref, w0_ref, b0_ref, b) for b in range(nb)]
    vx = [sconv(v_ref, wv_ref, bv_ref, b) * sconv(x1_ref, w1_ref, b1_ref, b) for b in range(nb)]

    kr, ki = _fwd_fft(k_ref[...], fk_ref, g_ref, tr, ti)
    for b0 in range(0, nb, 2):
        pair = vx[b0:b0 + 2]
        xi = pair[1] if len(pair) == 2 else jnp.zeros_like(pair[0])
        sr, si = _fwd_fft(jnp.concatenate([pair[0], xi], axis=1), fa_ref, g_ref, tr, ti)
        yr = sr * kr - si * ki
        yi = sr * ki + si * kr
        p00, p01, p10, p11 = _lane_stage(jnp.concatenate([yr, yi], axis=1), g_ref)
        cr, ci = p00 + p11, p10 - p01
        dr = cr * tr + ci * ti
        di = ci * tr - cr * ti
        d2 = jnp.concatenate([dr, di], axis=1).astype(BF16)
        y = jnp.einsum("cmk,ckr->cmr", fin_ref[...], d2, preferred_element_type=F32)
        half = shape[1]
        for j, yb in enumerate((y[:, :half], y[:, half:])[:len(pair)]):
            b = b0 + j
            res = (yb + vx[b] * hb_ref[...]) * x0[b]
            o_ref[b] = res.reshape(res.shape[0], res.shape[1] * res.shape[2]).astype(o_ref.dtype)


def _hyena(u_t, kern_t, conv_w, conv_b, hyena_bias, out_dtype):
    bsz, c3, l = u_t.shape
    c = c3 // 3
    r = int(round(math.sqrt(2 * l)))
    assert r * r == 2 * l and r % 2 == 0
    half = r // 2
    tc = _tile(c, 16)
    nc = c // tc
    consts = _dft_consts(r, tc)
    u4 = u_t.reshape(bsz, c3, half, r)
    k3 = kern_t.reshape(c, r, r)
    cw = conv_w.reshape(conv_w.shape[0], c3, 1, 1)
    cb = conv_b.reshape(c3, 1, 1)
    hb = hyena_bias.reshape(c, 1, 1)
    stream = lambda s: pl.BlockSpec((bsz, tc, half, r), lambda j, s=s: (0, j + s * nc, 0, 0))
    wspec = lambda s: pl.BlockSpec((conv_w.shape[0], tc, 1, 1), lambda j, s=s: (0, j + s * nc, 0, 0))
    bspec = lambda s: pl.BlockSpec((tc, 1, 1), lambda j, s=s: (j + s * nc, 0, 0))
    cspec = lambda a: pl.BlockSpec(a.shape, lambda j, nd=a.ndim: (0,) * nd)
    return pl.pallas_call(
        _hyena_kernel,
        grid=(nc,),
        in_specs=[stream(0), stream(1), stream(2), wspec(0), wspec(1), wspec(2),
                  bspec(0), bspec(1), bspec(2), bspec(0),
                  pl.BlockSpec((tc, r, r), lambda j: (j, 0, 0))] + [cspec(a) for a in consts],
        out_specs=pl.BlockSpec((bsz, tc, l), lambda j: (0, j, 0)),
        out_shape=jax.ShapeDtypeStruct((bsz, c, l), out_dtype),
        compiler_params=_cparams(("parallel",)),
        name="hyena",
    )(u4, u4, u4, cw, cw, cw, cb, cb, cb, hb, k3, *consts)


def _rep(x, n):
    if n % LANES:
        return x[:, :n]
    return x if n == LANES else jnp.concatenate([x] * (n // LANES), axis=1)


def _attn_kernel(q_ref, k_ref, v_ref, lq1_ref, lk1_ref, lq2_ref, lk2_ref, g_ref, *rest,
                 ncast, hd, tk, lam_init, eps):
    cast_in, o_ref, cast_out = rest[:ncast], rest[ncast], rest[ncast + 1:2 * ncast + 1]
    s_ref, mb_ref, m_ref, l_ref, acc_ref = rest[2 * ncast + 1:]
    for src, dst in zip(cast_in, cast_out):
        c = src.shape[1]
        dst[:, :c] = src[...].astype(dst.dtype)
        if dst.shape[1] > c:
            dst[:, c:] = jnp.zeros((dst.shape[0], dst.shape[1] - c), dst.dtype)
    nkv = k_ref.shape[1] // tk
    tq = q_ref.shape[1]
    hw = v_ref.shape[2]

    def scores(t, slot):
        for j in range(2):
            qj = q_ref[0, :, j * hd:(j + 1) * hd]
            kj = k_ref[0, t * tk:(t + 1) * tk, j * hd:(j + 1) * hd]
            s = lax.dot_general(qj, kj, (((1,), (1,)), ((), ())), preferred_element_type=F32)
            s_ref[slot, j] = s
            mb_ref[slot, j] = jnp.broadcast_to(jnp.max(s, axis=-1, keepdims=True), (tq, LANES))

    def consume(t, slot):
        v = v_ref[0, t * tk:(t + 1) * tk, :]
        for j in range(2):
            m_prev = m_ref[j]
            m_new = jnp.maximum(m_prev, mb_ref[slot, j])
            alpha = jnp.exp2(m_prev - m_new)
            ps = [jnp.exp2(s_ref[slot, j, :, c * LANES:(c + 1) * LANES] - m_new)
                  for c in range(tk // LANES)]
            l_ref[j] = alpha * l_ref[j] + functools.reduce(lambda a, b: a + b, ps)
            p = jnp.concatenate(ps, axis=1).astype(v.dtype)
            acc_ref[j] = _rep(alpha, hw) * acc_ref[j] + jnp.dot(p, v, preferred_element_type=F32)
            m_ref[j] = m_new

    m_ref[...] = jnp.full(m_ref.shape, -jnp.inf, F32)
    l_ref[...] = jnp.zeros(l_ref.shape, F32)
    acc_ref[...] = jnp.zeros(acc_ref.shape, F32)
    scores(0, 0)
    for t in range(nkv):
        if t + 1 < nkv:
            scores(t + 1, (t + 1) % 2)
        consume(t, t % 2)

    lam = (jnp.exp(jnp.sum(lq1_ref[...] * lk1_ref[...], axis=-1, keepdims=True))
           - jnp.exp(jnp.sum(lq2_ref[...] * lk2_ref[...], axis=-1, keepdims=True)) + lam_init)
    l0 = jnp.sum(l_ref[0], axis=-1, keepdims=True)
    l1 = jnp.sum(l_ref[1], axis=-1, keepdims=True)
    o = acc_ref[0] / l0 - lam * (acc_ref[1] / l1)
    o = o * lax.rsqrt(jnp.mean(o * o, axis=-1, keepdims=True) + eps) * g_ref[...]
    o_ref[0] = (o * (1.0 - lam_init)).astype(o_ref.dtype)


def _cast_blocks(shape, steps, full_rows):
    r, c = shape
    for a in range(steps, 0, -1):
        b = steps // a
        if steps % a or r % a or c % b or (full_rows and b > 1):
            continue
        rb, cb = r // a, c // b
        if rb % 16 == 0 and cb % LANES == 0:
            return rb, cb, b
    return None


def _diff_attention(qk, v, lq1, lk1, lq2, lk2, subln_g, lam_init, cast=()):
    bsz, l, width = v.shape
    hd = lq1.shape[-1]
    hw = 2 * hd
    heads = width // hw
    tq, tk = _tile(l, 512), _tile(l, 1024)
    nq = l // tq
    steps = bsz * heads * nq
    vec = lambda a: a.reshape(1, -1).astype(F32)
    vspec = lambda n: pl.BlockSpec((1, n), lambda b, h, i: (0, 0))
    plans = [_cast_blocks(a.shape, steps, pad > 0) for a, pad in cast]
    riders = [(a, pad, p) for (a, pad), p in zip(cast, plans) if p is not None]

    def cast_spec(p, pad):
        rb, cb, ncb = p
        return pl.BlockSpec((rb, cb + pad), lambda b, h, i: (((b * heads + h) * nq + i) // ncb,
                                                              ((b * heads + h) * nq + i) % ncb))

    cast_in_specs = [cast_spec(p, 0) for _, _, p in riders]
    cast_out_specs = [cast_spec(p, pad) for _, pad, p in riders]
    outs = pl.pallas_call(
        functools.partial(_attn_kernel, ncast=len(riders), hd=hd, tk=tk, lam_init=lam_init,
                          eps=SUBLN_EPS),
        grid=(bsz, heads, nq),
        in_specs=[pl.BlockSpec((1, tq, hw), lambda b, h, i: (b, i, h)),
                  pl.BlockSpec((1, l, hw), lambda b, h, i: (b, 0, heads + h)),
                  pl.BlockSpec((1, l, hw), lambda b, h, i: (b, 0, h)),
                  vspec(hd), vspec(hd), vspec(hd), vspec(hd), vspec(hw)] + cast_in_specs,
        out_specs=[pl.BlockSpec((1, tq, hw), lambda b, h, i: (b, i, h))] + cast_out_specs,
        out_shape=[jax.ShapeDtypeStruct((bsz, l, width), BF16)]
                  + [jax.ShapeDtypeStruct((a.shape[0], a.shape[1] + pad), BF16)
                     for a, pad, _ in riders],
        scratch_shapes=[pltpu.VMEM((2, 2, tq, tk), F32), pltpu.VMEM((2, 2, tq, LANES), F32),
                        pltpu.VMEM((2, tq, LANES), F32), pltpu.VMEM((2, tq, LANES), F32),
                        pltpu.VMEM((2, tq, hw), F32)],
        compiler_params=_cparams(("parallel", "parallel", "arbitrary")),
        name="diff_attn",
    )(qk, qk, v, vec(lq1), vec(lk1), vec(lq2), vec(lk2), vec(subln_g), *[a for a, _, _ in riders])
    rounded = iter(outs[1:])
    return outs[0], [next(rounded) if p is not None else jnp.pad(a.astype(BF16), ((0, 0), (0, pad)))
                     for (a, pad), p in zip(cast, plans)]


def _outproj_kernel(yh_ref, yd_ref, w1_ref, w2_ref, x_ref, gt_ref, o_ref):
    acc = lax.dot_general(yh_ref[0], w1_ref[...].astype(BF16), (((0,), (0,)), ((), ())),
                          preferred_element_type=F32)
    acc = acc + jnp.dot(yd_ref[0], w2_ref[...].astype(BF16), preferred_element_type=F32)
    o_ref[0] = x_ref[0] + gt_ref[0] * acc


def _outproj(yh_t, yd, w, x, gt):
    bsz, l, d = x.shape
    c, kd = yh_t.shape[1], yd.shape[2]
    assert c == kd and w.shape[0] == c + kd
    tm, tn = _tile(l, 1024), _tile(d, _weight_cols(w))
    return pl.pallas_call(
        _outproj_kernel,
        grid=(bsz, l // tm, d // tn),
        in_specs=[pl.BlockSpec((1, c, tm), lambda b, i, j: (b, 0, i)),
                  pl.BlockSpec((1, tm, kd), lambda b, i, j: (b, i, 0)),
                  pl.BlockSpec((c, tn), lambda b, i, j: (0, j)),
                  pl.BlockSpec((kd, tn), lambda b, i, j: (1, j)),
                  pl.BlockSpec((1, tm, tn), lambda b, i, j: (b, i, j)),
                  pl.BlockSpec((1, 1, tn), lambda b, i, j: (b, 0, j))],
        out_specs=pl.BlockSpec((1, tm, tn), lambda b, i, j: (b, i, j)),
        out_shape=jax.ShapeDtypeStruct((bsz, l, d), F32),
        compiler_params=_cparams(("parallel", "parallel", "arbitrary")),
        name="outproj",
    )(yh_t, yd, w, w, x, gt)


def _gateup_kernel(h_ref, wg_ref, wu_ref, *rest, src_blocks):
    if src_blocks:
        src_ref, o_ref, dst_ref = rest
        step = (pl.program_id(0) * pl.num_programs(1) + pl.program_id(1)) * pl.num_programs(2) \
            + pl.program_id(2)

        @pl.when(step < src_blocks)
        def _():
            dst_ref[...] = src_ref[...].astype(dst_ref.dtype)

        @pl.when(step >= src_blocks)
        def _():
            dst_ref[...] = jnp.zeros(dst_ref.shape, dst_ref.dtype)
    else:
        (o_ref,) = rest
    h = h_ref[0]
    g = jnp.dot(h, wg_ref[...], preferred_element_type=F32)
    u = jnp.dot(h, wu_ref[...], preferred_element_type=F32)
    o_ref[0] = (_silu(g) * u).astype(o_ref.dtype)


def _gateup(h, wg, wu, wd):
    bsz, l, d = h.shape
    n = wg.shape[1]
    tm, tn = _tile(l, 1024), _tile(n, 512)
    grid = (bsz, l // tm, n // tn)
    steps = grid[0] * grid[1] * grid[2]
    wspec = pl.BlockSpec((d, tn), lambda b, i, j: (0, j))
    in_specs = [pl.BlockSpec((1, tm, d), lambda b, i, j: (b, i, 0)), wspec, wspec]
    out_specs = [pl.BlockSpec((1, tm, tn), lambda b, i, j: (b, i, j))]
    out_shape = [jax.ShapeDtypeStruct((bsz, l, n), BF16)]
    args = [h, wg, wu]
    rb = n // steps
    rides = n % steps == 0 and rb % 16 == 0 and wd.shape[0] % rb == 0
    src_blocks = wd.shape[0] // rb if rides else 0
    if rides:
        lin = lambda b, i, j: (b * grid[1] + i) * grid[2] + j
        in_specs.append(pl.BlockSpec((rb, wd.shape[1]),
                                     lambda b, i, j: (jnp.minimum(lin(b, i, j), src_blocks - 1), 0)))
        out_specs.append(pl.BlockSpec((rb, wd.shape[1]), lambda b, i, j: (lin(b, i, j), 0)))
        out_shape.append(jax.ShapeDtypeStruct((n, wd.shape[1]), BF16))
        args.append(wd)
    outs = pl.pallas_call(
        functools.partial(_gateup_kernel, src_blocks=src_blocks),
        grid=grid,
        in_specs=in_specs,
        out_specs=out_specs,
        out_shape=out_shape,
        compiler_params=_cparams(("parallel", "parallel", "arbitrary")),
        name="gateup",
    )(*args)
    if rides:
        return outs[0], outs[1]
    return outs[0], jnp.pad(wd.astype(BF16), ((0, n - wd.shape[0]), (0, 0)))


def _down_kernel(g_ref, w_ref, x_ref, gt_ref, gf_ref, o_ref, *, final_eps):
    k = pl.program_id(2)
    d = o_ref.shape[2]
    tn = _tile(d, 1024)
    chunks = [slice(c * tn, (c + 1) * tn) for c in range(d // tn)]
    part = lambda cols: jnp.dot(g_ref[0], w_ref[:, cols], preferred_element_type=F32)

    @pl.when(k == 0)
    def _():
        for cols in chunks:
            o_ref[0, :, cols] = part(cols)

    @pl.when(k > 0)
    def _():
        for cols in chunks:
            o_ref[0, :, cols] += part(cols)

    @pl.when(k == pl.num_programs(2) - 1)
    def _():
        def finish(r, carry):
            rows = pl.ds(pl.multiple_of(r * EPILOGUE_ROWS, EPILOGUE_ROWS), EPILOGUE_ROWS)
            x2 = x_ref[0, rows, :] + gt_ref[0] * o_ref[0, rows, :]
            if final_eps is not None:
                ms = jnp.mean(x2 * x2, axis=-1, keepdims=True)
                x2 = x2 * lax.rsqrt(ms + final_eps) * gf_ref[...]
            o_ref[0, rows, :] = x2
            return carry

        lax.fori_loop(0, o_ref.shape[1] // EPILOGUE_ROWS, finish, 0)


def _down(g, w, x, gt, g_final, final_eps):
    bsz, l, d = x.shape
    kdim = g.shape[2]
    tm, tk = _tile(l, 512), _tile(kdim, 1024)
    return pl.pallas_call(
        functools.partial(_down_kernel, final_eps=final_eps),
        grid=(bsz, l // tm, kdim // tk),
        in_specs=[pl.BlockSpec((1, tm, tk), lambda b, i, k: (b, i, k)),
                  pl.BlockSpec((tk, d), lambda b, i, k: (k, 0)),
                  pl.BlockSpec((1, tm, d), lambda b, i, k: (b, i, 0)),
                  pl.BlockSpec((1, 1, d), lambda b, i, k: (b, 0, 0)),
                  pl.BlockSpec((1, d), lambda b, i, k: (0, 0))],
        out_specs=pl.BlockSpec((1, tm, d), lambda b, i, k: (b, i, 0)),
        out_shape=jax.ShapeDtypeStruct((bsz, l, d), F32),
        compiler_params=_cparams(("parallel", "parallel", "arbitrary")),
        name="down",
    )(g, w, x, gt, g_final.reshape(1, d))


def kernel(x, c, positions, w_ada, b_ada, g_mix, g_ffn, w_in, conv_w, conv_b, f_w1, f_b1, f_w2, f_b2, f_w3, f_b3, f_w4, f_freq, hyena_bias, lambda_q1, lambda_k1, lambda_q2, lambda_k2, subln_g, w_out, w_gate, w_up, w_down, g_final):
    bsz, l, d = x.shape
    depth = w_ada.shape[0]
    ch = hyena_bias.shape[-1]
    hd = lambda_q1.shape[-1]
    qk = (w_in.shape[-1] - 3 * ch - (d - ch)) // 2
    cos, sin = _rope_tables(positions, hd)
    q_scale = hd ** -0.5 * math.log2(math.e)
    for i in range(depth):
        lam_init = 0.8 - 0.6 * math.exp(-0.3 * i)
        mod, kern_t = _ada_and_taps(c, w_ada[i], b_ada[i], l, ch, f_w1[i], f_b1[i], f_w2[i], f_b2[i],
                                    f_w3[i], f_b3[i], f_w4[i], f_freq[i])
        sh1, sc1, gt1, sh2, sc2, gt2 = [mod[:, None, j * d:(j + 1) * d] for j in range(N_MOD)]
        h = _norm_mod(x, g_mix[i], sc1, sh1, NORM_EPS)
        w = w_in[i].astype(BF16)
        o1, o3 = 3 * ch, 3 * ch + 2 * qk
        u_t = _proj_t(h, w, 0, o1, F32)
        qk_scale = jnp.concatenate([jnp.full((1, qk), q_scale, F32), jnp.ones((1, qk), F32)], axis=1)
        q_k = _proj(h, w, o1, 2 * qk, rope=(cos, sin), col_scale=qk_scale)
        v = _proj(h, w, o3, w.shape[1] - o3)
        y_hy = _hyena(u_t, kern_t, conv_w[i], conv_b[i], hyena_bias[i], BF16)
        hpad = -w_gate.shape[-1] % 1024
        y_da, (wo, wg, wu) = _diff_attention(
            q_k, v, lambda_q1[i], lambda_k1[i], lambda_q2[i], lambda_k2[i], subln_g[i], lam_init,
            cast=((w_out[i], 0), (w_gate[i], hpad), (w_up[i], hpad)))
        x = _outproj(y_hy, y_da, wo, x, gt1)
        h = _norm_mod(x, g_ffn[i], sc2, sh2, NORM_EPS)
        g, wd = _gateup(h, wg, wu, w_down[i])
        last = i == depth - 1
        x = _down(g, wd, x, gt2, g_final, NORM_EPS if last else None)
    return x
```

```python
import functools
import math

import numpy as np
import jax
import jax.numpy as jnp
from jax import lax
from jax.experimental import pallas as pl
from jax.experimental.pallas import tpu as pltpu

F32 = jnp.float32
BF16 = jnp.bfloat16

NORM_EPS = 1e-6
SUBLN_EPS = 1e-5
ROPE_THETA = 10000.0
FAST_DECAY_PCT = 0.3
SLOW_DECAY_PCT = 1.5
DECAY_TARGET = 1e-2
N_MOD = 6

V7X_VMEM_LIMIT_BYTES = 56 * 1024 * 1024
LANES = 128
EPILOGUE_ROWS = 16
ADA_TILE_BYTES = 13 * 512 * 1024


def _cparams(sem):
    return pltpu.CompilerParams(dimension_semantics=sem, vmem_limit_bytes=V7X_VMEM_LIMIT_BYTES)


def _tile(n, pref):
    t = min(n, pref)
    while n % t:
        t -= 1
    return t


def _weight_cols(w):
    return 1024 if w.dtype == BF16 else 512


def _silu(x):
    return x * (1.0 / (1.0 + jnp.exp(-x)))


def _split_bf16(a):
    hi = a.astype(BF16)
    lo = (a - hi.astype(F32)).astype(BF16)
    return hi, lo


def _dot3(a, b):
    ah, al = _split_bf16(a)
    bh, bl = _split_bf16(b)
    d = functools.partial(jnp.dot, preferred_element_type=F32)
    return d(ah, bh) + (d(ah, bl) + d(al, bh))


def _ada_kernel(c_ref, w_ref, b_ref, o_ref):
    o_ref[...] = _dot3(_silu(c_ref[...]), w_ref[...]) + b_ref[...]


def _norm_mod_kernel(x_ref, g_ref, sc_ref, sh_ref, o_ref, *, eps):
    x = x_ref[0]
    y = x * lax.rsqrt(jnp.mean(x * x, axis=-1, keepdims=True) + eps) * g_ref[...]
    o_ref[0] = (y * (1.0 + sc_ref[0]) + sh_ref[0]).astype(o_ref.dtype)


def _norm_mod(x, g, sc, sh, eps):
    bsz, l, d = x.shape
    tr = _tile(l, 512)
    return pl.pallas_call(
        functools.partial(_norm_mod_kernel, eps=eps),
        grid=(bsz, l // tr),
        in_specs=[pl.BlockSpec((1, tr, d), lambda b, i: (b, i, 0)),
                  pl.BlockSpec((1, d), lambda b, i: (0, 0)),
                  pl.BlockSpec((1, 1, d), lambda b, i: (b, 0, 0)),
                  pl.BlockSpec((1, 1, d), lambda b, i: (b, 0, 0))],
        out_specs=pl.BlockSpec((1, tr, d), lambda b, i: (b, i, 0)),
        out_shape=jax.ShapeDtypeStruct((bsz, l, d), BF16),
        compiler_params=_cparams(("parallel", "parallel")),
        name="norm_mod",
    )(x, g.reshape(1, d), sc, sh)


def _rope_kernel(pos_ref, inv_ref, sgn_ref, cos_ref, sin_ref):
    ang = pos_ref[0].astype(F32) * inv_ref[...]
    cos_ref[0] = jnp.cos(ang)
    sin_ref[0] = jnp.sin(ang) * sgn_ref[...]


def _rope_tables(positions, dim):
    bsz, l = positions.shape
    half = dim // 2
    inv = 1.0 / (ROPE_THETA ** (jnp.arange(0, dim, 2, dtype=F32) / dim))
    inv = jnp.concatenate([inv, inv]).reshape(1, dim)
    sgn = jnp.asarray(np.concatenate([-np.ones(half), np.ones(half)]).reshape(1, dim), F32)
    tr = _tile(l, 1024)
    shp = jax.ShapeDtypeStruct((bsz, l, dim), F32)
    return pl.pallas_call(
        _rope_kernel,
        grid=(bsz, l // tr),
        in_specs=[pl.BlockSpec((1, tr, 1), lambda b, i: (b, i, 0)),
                  pl.BlockSpec((1, dim), lambda b, i: (0, 0)),
                  pl.BlockSpec((1, dim), lambda b, i: (0, 0))],
        out_specs=[pl.BlockSpec((1, tr, dim), lambda b, i: (b, i, 0))] * 2,
        out_shape=[shp, shp],
        compiler_params=_cparams(("parallel", "parallel")),
        name="rope_tables",
    )(positions.reshape(bsz, l, 1), inv, sgn)


def _proj_t_kernel(w_ref, h_ref, o_ref):
    res = lax.dot_general(w_ref[...].astype(BF16), h_ref[0], (((0,), (1,)), ((), ())),
                          preferred_element_type=F32)
    o_ref[0] = res.reshape(o_ref.shape[1:]).astype(o_ref.dtype)


def _proj_t(h, w, col0, n, r, out_dtype):
    bsz, l, d = h.shape
    tm, tn = _tile(l, 1024), _tile(math.gcd(n, col0) if col0 else n, _weight_cols(w))
    assert tm % (8 * r) == 0
    j0 = col0 // tn
    return pl.pallas_call(
        _proj_t_kernel,
        grid=(bsz, l // tm, n // tn),
        in_specs=[pl.BlockSpec((d, tn), lambda b, i, j: (0, j + j0)),
                  pl.BlockSpec((1, tm, d), lambda b, i, j: (b, i, 0))],
        out_specs=pl.BlockSpec((1, tn, tm // r, r), lambda b, i, j: (b, j, i, 0)),
        out_shape=jax.ShapeDtypeStruct((bsz, n, l // r, r), out_dtype),
        compiler_params=_cparams(("parallel", "parallel", "arbitrary")),
        name="proj_t",
    )(w, h)


def _proj_kernel(h_ref, w_ref, *rest, rope_dim):
    acc = jnp.dot(h_ref[0], w_ref[...].astype(BF16), preferred_element_type=F32)
    if rope_dim:
        cos_ref, sin_ref, scale_ref, o_ref = rest
        cos, sin = cos_ref[0], sin_ref[0]
        for g in range(acc.shape[1] // rope_dim):
            cols = slice(g * rope_dim, (g + 1) * rope_dim)
            xg = acc[:, cols]
            yg = xg * cos + pltpu.roll(xg, rope_dim // 2, 1) * sin
            o_ref[0, :, cols] = (yg * scale_ref[:, cols]).astype(o_ref.dtype)
    else:
        (o_ref,) = rest
        o_ref[0] = acc.astype(o_ref.dtype)


def _proj(h, w, col0, n, rope=None, col_scale=None):
    bsz, l, d = h.shape
    tm, tn = _tile(l, 1024), _tile(math.gcd(n, col0) if col0 else n, _weight_cols(w))
    j0 = col0 // tn
    in_specs = [pl.BlockSpec((1, tm, d), lambda b, i, j: (b, i, 0)),
                pl.BlockSpec((d, tn), lambda b, i, j: (0, j + j0))]
    args = [h, w]
    rope_dim = 0
    if rope is not None:
        rope_dim = rope[0].shape[-1]
        in_specs += [pl.BlockSpec((1, tm, rope_dim), lambda b, i, j: (b, i, 0))] * 2
        in_specs += [pl.BlockSpec((1, tn), lambda b, i, j: (0, j))]
        args += [*rope, col_scale]
    return pl.pallas_call(
        functools.partial(_proj_kernel, rope_dim=rope_dim),
        grid=(bsz, l // tm, n // tn),
        in_specs=in_specs,
        out_specs=pl.BlockSpec((1, tm, tn), lambda b, i, j: (b, i, j)),
        out_shape=jax.ShapeDtypeStruct((bsz, l, n), BF16),
        compiler_params=_cparams(("parallel", "parallel", "arbitrary")),
        name="proj_rope" if rope_dim else "proj",
    )(*args)


def _filt_kernel(z_ref, t_ref, w1_ref, b1_ref, w2_ref, b2_ref, w3_ref, b3_ref, fr_ref,
                 w4_ref, ad_ref, o_ref):
    fr = fr_ref[...]
    h = jnp.sin(fr * (_dot3(w1_ref[...], z_ref[...]) + b1_ref[...]))
    h = jnp.sin(fr * (_dot3(w2_ref[...], h) + b2_ref[...]))
    h = jnp.sin(fr * (_dot3(w3_ref[...], h) + b3_ref[...]))
    t = t_ref[...]
    decay = jnp.exp(-(ad_ref[...] * t[0:1, :]))
    o_ref[...] = (_dot3(w4_ref[0], h) * decay * t[1:2, :]).reshape(o_ref.shape)


def _ada_filt_kernel(c_ref, wa_ref, ba_ref, *rest, stride):
    mod_ref, kern_ref = rest[-2:]
    _ada_kernel(c_ref, wa_ref, ba_ref, mod_ref)

    @pl.when(pl.program_id(0) % stride == 0)
    def _():
        _filt_kernel(*rest[:-2], kern_ref)


def _ada_and_taps(cvec, w_ada, b_ada, l, c, f_w1, f_b1, f_w2, f_b2, f_w3, f_b3, f_w4, f_freq):
    n = 2 * l
    emb, hid = f_w1.shape
    bands = (emb - 1) // 2
    t = jnp.linspace(0.0, 1.0, l, dtype=F32)[:, None]
    w = 2.0 * math.pi * jnp.arange(l, dtype=F32)[:, None] / l
    f = jnp.linspace(1e-4, bands - 1, bands, dtype=F32)[None, :]
    z = jnp.concatenate([t, jnp.cos(f * w), -jnp.sin(f * w)], axis=-1)
    cols = lambda a: jnp.concatenate([a, a[:1], a[:0:-1]], axis=0).T
    embp = -(-emb // 8) * 8
    z = jnp.pad(cols(z), ((0, embp - emb), (0, 0)))
    mask = jnp.ones((1, n), F32).at[0, l].set(0.0)
    tm = jnp.concatenate([cols(t), mask], axis=0)
    min_decay = math.log(DECAY_TARGET) / SLOW_DECAY_PCT
    max_decay = math.log(DECAY_TARGET) / FAST_DECAY_PCT
    ad = jnp.abs(jnp.linspace(min_decay, max_decay, c, dtype=F32)).reshape(c, 1)
    w1t = jnp.pad(f_w1.T, ((0, 0), (0, embp - emb)))
    w4t = f_w4.T.reshape(2, c, hid)
    col = lambda a: a.reshape(hid, 1)
    bsz, d = cvec.shape
    n_mod = w_ada.shape[1]
    r = int(round(math.sqrt(n)))
    assert r * r == n
    common = math.gcd(n_mod // LANES, n // LANES)
    fits = lambda s: common % s == 0 and s % 2 == 0 and (n_mod // s) * d * 4 <= ADA_TILE_BYTES
    nf = n // (8 * r)
    nt = next((s for s in range(nf, common + 1, nf) if fits(s)), 0) if nf and nf % 2 == 0 else 0
    folded = nt > 0
    if not folded:
        nt = next((s for s in range(2, common + 1, 2) if fits(s)), common)
        nf = nt
    stride = nt // nf
    tn, tn_mod = n // nf, n_mod // nt
    rows = 8
    cp = jnp.zeros((rows, d), F32).at[:bsz].set(cvec)
    full = lambda shape: pl.BlockSpec(shape, lambda i: (0,) * len(shape))
    if folded:
        kern_spec = pl.BlockSpec((c, 8, r), lambda i: (0, i // stride, 0))
        kern_shape = jax.ShapeDtypeStruct((c, r, r), F32)
    else:
        kern_spec = pl.BlockSpec((c, tn), lambda i: (0, i // stride))
        kern_shape = jax.ShapeDtypeStruct((c, n), F32)
    mod, kern = pl.pallas_call(
        functools.partial(_ada_filt_kernel, stride=stride),
        grid=(nt,),
        in_specs=[full((rows, d)),
                  pl.BlockSpec((d, tn_mod), lambda i: (0, i)),
                  pl.BlockSpec((1, tn_mod), lambda i: (0, i)),
                  pl.BlockSpec((embp, tn), lambda i: (0, i // stride)),
                  pl.BlockSpec((2, tn), lambda i: (0, i // stride)),
                  full((hid, embp)), full((hid, 1)), full((hid, hid)), full((hid, 1)),
                  full((hid, hid)), full((hid, 1)), full((hid, 1)),
                  pl.BlockSpec((1, c, hid), lambda i: ((2 * (i // stride)) // nf, 0, 0)),
                  full((c, 1))],
        out_specs=[pl.BlockSpec((rows, tn_mod), lambda i: (0, i)), kern_spec],
        out_shape=[jax.ShapeDtypeStruct((rows, n_mod), F32), kern_shape],
        compiler_params=_cparams(("arbitrary",)),
        name="ada_filter",
    )(cp, w_ada, b_ada.reshape(1, n_mod), z, tm, w1t, col(f_b1), f_w2.T, col(f_b2), f_w3.T,
      col(f_b3), col(f_freq), w4t, ad)
    return mod[:bsz], kern.reshape(c, r, r)


def _dft_consts(r, tc):
    half = r // 2
    idx = np.arange(r)
    ang = -2.0 * np.pi * np.outer(idx, idx) / r
    fr, fi = np.cos(ang), np.sin(ang)
    angt = -2.0 * np.pi * np.outer(idx, idx) / (r * r)
    fa = np.block([[fr[:, :half], -fi[:, :half]], [fi[:, :half], fr[:, :half]]])
    fk = np.concatenate([fr, fi], axis=0)
    g = np.concatenate([fr, fi], axis=1)
    fin = np.block([[fr[:half], fi[:half]], [-fi[:half], fr[:half]]]) / float(r * r)
    mx = lambda a: jnp.asarray(a, F32).astype(BF16)
    bc = lambda a: jnp.broadcast_to(mx(a)[None], (tc,) + a.shape)
    return (bc(fa), bc(fk), mx(g), jnp.asarray(np.cos(angt), F32),
            jnp.asarray(np.sin(angt), F32), bc(fin))


def _shift_prev(u, lane, row):
    nr, nl = u.shape[-2], u.shape[-1]
    a = pltpu.roll(u, 1, u.ndim - 1)
    b = pltpu.roll(a, 1, u.ndim - 2)
    p = jnp.where(lane == 0, b, a)
    return jnp.where((lane == 0) & (row == 0), 0.0, p)


def _shift_next(u, lane, row):
    nr, nl = u.shape[-2], u.shape[-1]
    a = pltpu.roll(u, nl - 1, u.ndim - 1)
    b = pltpu.roll(a, nr - 1, u.ndim - 2)
    p = jnp.where(lane == nl - 1, b, a)
    return jnp.where((lane == nl - 1) & (row == nr - 1), 0.0, p)


def _lane_stage(y2, g_ref):
    tc, r2, r = y2.shape
    p = jnp.dot(y2.reshape(tc * r2, r).astype(BF16), g_ref[...], preferred_element_type=F32)
    p = p.reshape(tc, r2, r2)
    return p[:, :r, :r], p[:, :r, r:], p[:, r:, :r], p[:, r:, r:]


def _fwd_fft(x2, f1_ref, g_ref, tr, ti):
    r = tr.shape[0]
    a = jnp.einsum("cmk,ckr->cmr", f1_ref[...], x2.astype(BF16), preferred_element_type=F32)
    ar, ai = a[:, :r], a[:, r:]
    br = ar * tr - ai * ti
    bi = ar * ti + ai * tr
    p00, p01, p10, p11 = _lane_stage(jnp.concatenate([br, bi], axis=1), g_ref)
    return p00 - p11, p01 + p10


def _hyena_kernel(x0_ref, x1_ref, v_ref, w0_ref, w1_ref, wv_ref, b0_ref, b1_ref, bv_ref,
                  hb_ref, k_ref, fa_ref, fk_ref, g_ref, tr_ref, ti_ref, fin_ref, o_ref):
    shape = x0_ref.shape[1:]
    lane = lax.broadcasted_iota(jnp.int32, shape, 2)
    row = lax.broadcasted_iota(jnp.int32, shape, 1)
    tr, ti = tr_ref[...], ti_ref[...]

    def sconv(u_ref, w_ref, b_ref, b):
        u = u_ref[b].astype(F32)
        return (b_ref[...] + w_ref[0] * _shift_prev(u, lane, row) + w_ref[1] * u
                + w_ref[2] * _shift_next(u, lane, row))

    nb = x0_ref.shape[0]
    x0 = [sconv(x0_ref, w0_ref, b0_ref, b) for b in range(nb)]
    vx = [sconv(v_ref, wv_ref, bv_ref, b) * sconv(x1_ref, w1_ref, b1_ref, b) for b in range(nb)]

    kr, ki = _fwd_fft(k_ref[...], fk_ref, g_ref, tr, ti)
    for b0 in range(0, nb, 2):
        pair = vx[b0:b0 + 2]
        xi = pair[1] if len(pair) == 2 else jnp.zeros_like(pair[0])
        sr, si = _fwd_fft(jnp.concatenate([pair[0], xi], axis=1), fa_ref, g_ref, tr, ti)
        yr = sr * kr - si * ki
        yi = sr * ki + si * kr
        p00, p01, p10, p11 = _lane_stage(jnp.concatenate([yr, yi], axis=1), g_ref)
        cr, ci = p00 + p11, p10 - p01
        dr = cr * tr + ci * ti
        di = ci * tr - cr * ti
        d2 = jnp.concatenate([dr, di], axis=1).astype(BF16)
        y = jnp.einsum("cmk,ckr->cmr", fin_ref[...], d2, preferred_element_type=F32)
        half = shape[1]
        for j, yb in enumerate((y[:, :half], y[:, half:])[:len(pair)]):
            b = b0 + j
            res = (yb + vx[b] * hb_ref[...]) * x0[b]
            o_ref[b] = res.reshape(res.shape[0], res.shape[1] * res.shape[2]).astype(o_ref.dtype)


def _hyena(u4, k3, conv_w, conv_b, hyena_bias, out_dtype):
    bsz, c3, half, r = u4.shape
    c = c3 // 3
    l = half * r
    assert r == 2 * half and k3.shape == (c, r, r)
    tc = _tile(c, 16)
    nc = c // tc
    consts = _dft_consts(r, tc)
    cw = conv_w.reshape(conv_w.shape[0], c3, 1, 1)
    cb = conv_b.reshape(c3, 1, 1)
    hb = hyena_bias.reshape(c, 1, 1)
    stream = lambda s: pl.BlockSpec((bsz, tc, half, r), lambda j, s=s: (0, j + s * nc, 0, 0))
    wspec = lambda s: pl.BlockSpec((conv_w.shape[0], tc, 1, 1), lambda j, s=s: (0, j + s * nc, 0, 0))
    bspec = lambda s: pl.BlockSpec((tc, 1, 1), lambda j, s=s: (j + s * nc, 0, 0))
    cspec = lambda a: pl.BlockSpec(a.shape, lambda j, nd=a.ndim: (0,) * nd)
    return pl.pallas_call(
        _hyena_kernel,
        grid=(nc,),
        in_specs=[stream(0), stream(1), stream(2), wspec(0), wspec(1), wspec(2),
                  bspec(0), bspec(1), bspec(2), bspec(0),
                  pl.BlockSpec((tc, r, r), lambda j: (j, 0, 0))] + [cspec(a) for a in consts],
        out_specs=pl.BlockSpec((bsz, tc, l), lambda j: (0, j, 0)),
        out_shape=jax.ShapeDtypeStruct((bsz, c, l), out_dtype),
        compiler_params=_cparams(("parallel",)),
        name="hyena",
    )(u4, u4, u4, cw, cw, cw, cb, cb, cb, hb, k3, *consts)


def _rep(x, n):
    if n % LANES:
        return x[:, :n]
    return x if n == LANES else jnp.concatenate([x] * (n // LANES), axis=1)


def _attn_kernel(q_ref, k_ref, v_ref, lq1_ref, lk1_ref, lq2_ref, lk2_ref, g_ref, *rest,
                 ncast, hd, tk, lam_init, eps):
    cast_in, o_ref, cast_out = rest[:ncast], rest[ncast], rest[ncast + 1:2 * ncast + 1]
    s_ref, mb_ref, m_ref, l_ref, acc_ref = rest[2 * ncast + 1:]
    for src, dst in zip(cast_in, cast_out):
        c = src.shape[1]
        dst[:, :c] = src[...].astype(dst.dtype)
        if dst.shape[1] > c:
            dst[:, c:] = jnp.zeros((dst.shape[0], dst.shape[1] - c), dst.dtype)
    nkv = k_ref.shape[1] // tk
    tq = q_ref.shape[1]
    hw = v_ref.shape[2]

    def scores(t, slot):
        for j in range(2):
            qj = q_ref[0, :, j * hd:(j + 1) * hd]
            kj = k_ref[0, t * tk:(t + 1) * tk, j * hd:(j + 1) * hd]
            s = lax.dot_general(qj, kj, (((1,), (1,)), ((), ())), preferred_element_type=F32)
            s_ref[slot, j] = s
            mb_ref[slot, j] = jnp.broadcast_to(jnp.max(s, axis=-1, keepdims=True), (tq, LANES))

    def consume(t, slot):
        v = v_ref[0, t * tk:(t + 1) * tk, :]
        for j in range(2):
            m_prev = m_ref[j]
            m_new = jnp.maximum(m_prev, mb_ref[slot, j])
            alpha = jnp.exp2(m_prev - m_new)
            ps = [jnp.exp2(s_ref[slot, j, :, c * LANES:(c + 1) * LANES] - m_new)
                  for c in range(tk // LANES)]
            l_ref[j] = alpha * l_ref[j] + functools.reduce(lambda a, b: a + b, ps)
            p = jnp.concatenate(ps, axis=1).astype(v.dtype)
            acc_ref[j] = _rep(alpha, hw) * acc_ref[j] + jnp.dot(p, v, preferred_element_type=F32)
            m_ref[j] = m_new

    m_ref[...] = jnp.full(m_ref.shape, -jnp.inf, F32)
    l_ref[...] = jnp.zeros(l_ref.shape, F32)
    acc_ref[...] = jnp.zeros(acc_ref.shape, F32)
    scores(0, 0)
    for t in range(nkv):
        if t + 1 < nkv:
            scores(t + 1, (t + 1) % 2)
        consume(t, t % 2)

    lam = (jnp.exp(jnp.sum(lq1_ref[...] * lk1_ref[...], axis=-1, keepdims=True))
           - jnp.exp(jnp.sum(lq2_ref[...] * lk2_ref[...], axis=-1, keepdims=True)) + lam_init)
    l0 = jnp.sum(l_ref[0], axis=-1, keepdims=True)
    l1 = jnp.sum(l_ref[1], axis=-1, keepdims=True)
    o = acc_ref[0] / l0 - lam * (acc_ref[1] / l1)
    o = o * lax.rsqrt(jnp.mean(o * o, axis=-1, keepdims=True) + eps) * g_ref[...]
    o_ref[0] = (o * (1.0 - lam_init)).astype(o_ref.dtype)


def _cast_blocks(shape, steps, full_rows):
    r, c = shape
    for a in range(steps, 0, -1):
        b = steps // a
        if steps % a or r % a or c % b or (full_rows and b > 1):
            continue
        rb, cb = r // a, c // b
        if rb % 16 == 0 and cb % LANES == 0:
            return rb, cb, b
    return None


def _diff_attention(qk, v, lq1, lk1, lq2, lk2, subln_g, lam_init, cast=()):
    bsz, l, width = v.shape
    hd = lq1.shape[-1]
    hw = 2 * hd
    heads = width // hw
    tq, tk = _tile(l, 512), _tile(l, 1024)
    nq = l // tq
    steps = bsz * heads * nq
    vec = lambda a: a.reshape(1, -1).astype(F32)
    vspec = lambda n: pl.BlockSpec((1, n), lambda b, h, i: (0, 0))
    plans = [_cast_blocks(a.shape, steps, pad > 0) for a, pad in cast]
    riders = [(a, pad, p) for (a, pad), p in zip(cast, plans) if p is not None]

    def cast_spec(p, pad):
        rb, cb, ncb = p
        return pl.BlockSpec((rb, cb + pad), lambda b, h, i: (((b * heads + h) * nq + i) // ncb,
                                                              ((b * heads + h) * nq + i) % ncb))

    cast_in_specs = [cast_spec(p, 0) for _, _, p in riders]
    cast_out_specs = [cast_spec(p, pad) for _, pad, p in riders]
    outs = pl.pallas_call(
        functools.partial(_attn_kernel, ncast=len(riders), hd=hd, tk=tk, lam_init=lam_init,
                          eps=SUBLN_EPS),
        grid=(bsz, heads, nq),
        in_specs=[pl.BlockSpec((1, tq, hw), lambda b, h, i: (b, i, h)),
                  pl.BlockSpec((1, l, hw), lambda b, h, i: (b, 0, heads + h)),
                  pl.BlockSpec((1, l, hw), lambda b, h, i: (b, 0, h)),
                  vspec(hd), vspec(hd), vspec(hd), vspec(hd), vspec(hw)] + cast_in_specs,
        out_specs=[pl.BlockSpec((1, tq, hw), lambda b, h, i: (b, i, h))] + cast_out_specs,
        out_shape=[jax.ShapeDtypeStruct((bsz, l, width), BF16)]
                  + [jax.ShapeDtypeStruct((a.shape[0], a.shape[1] + pad), BF16)
                     for a, pad, _ in riders],
        scratch_shapes=[pltpu.VMEM((2, 2, tq, tk), F32), pltpu.VMEM((2, 2, tq, LANES), F32),
                        pltpu.VMEM((2, tq, LANES), F32), pltpu.VMEM((2, tq, LANES), F32),
                        pltpu.VMEM((2, tq, hw), F32)],
        compiler_params=_cparams(("parallel", "parallel", "arbitrary")),
        name="diff_attn",
    )(qk, qk, v, vec(lq1), vec(lk1), vec(lq2), vec(lk2), vec(subln_g), *[a for a, _, _ in riders])
    rounded = iter(outs[1:])
    return outs[0], [next(rounded) if p is not None else jnp.pad(a.astype(BF16), ((0, 0), (0, pad)))
                     for (a, pad), p in zip(cast, plans)]


def _outproj_kernel(yh_ref, yd_ref, w1_ref, w2_ref, x_ref, gt_ref, o_ref):
    acc = lax.dot_general(yh_ref[0], w1_ref[...].astype(BF16), (((0,), (0,)), ((), ())),
                          preferred_element_type=F32)
    acc = acc + jnp.dot(yd_ref[0], w2_ref[...].astype(BF16), preferred_element_type=F32)
    o_ref[0] = x_ref[0] + gt_ref[0] * acc


def _outproj(yh_t, yd, w, x, gt):
    bsz, l, d = x.shape
    c, kd = yh_t.shape[1], yd.shape[2]
    assert c == kd and w.shape[0] == c + kd
    tm, tn = _tile(l, 1024), _tile(d, _weight_cols(w))
    return pl.pallas_call(
        _outproj_kernel,
        grid=(bsz, l // tm, d // tn),
        in_specs=[pl.BlockSpec((1, c, tm), lambda b, i, j: (b, 0, i)),
                  pl.BlockSpec((1, tm, kd), lambda b, i, j: (b, i, 0)),
                  pl.BlockSpec((c, tn), lambda b, i, j: (0, j)),
                  pl.BlockSpec((kd, tn), lambda b, i, j: (1, j)),
                  pl.BlockSpec((1, tm, tn), lambda b, i, j: (b, i, j)),
                  pl.BlockSpec((1, 1, tn), lambda b, i, j: (b, 0, j))],
        out_specs=pl.BlockSpec((1, tm, tn), lambda b, i, j: (b, i, j)),
        out_shape=jax.ShapeDtypeStruct((bsz, l, d), F32),
        compiler_params=_cparams(("parallel", "parallel", "arbitrary")),
        name="outproj",
    )(yh_t, yd, w, w, x, gt)


def _gateup_kernel(h_ref, wg_ref, wu_ref, *rest, src_blocks):
    if src_blocks:
        src_ref, o_ref, dst_ref = rest
        step = (pl.program_id(0) * pl.num_programs(1) + pl.program_id(1)) * pl.num_programs(2) \
            + pl.program_id(2)

        @pl.when(step < src_blocks)
        def _():
            dst_ref[...] = src_ref[...].astype(dst_ref.dtype)

        @pl.when(step >= src_blocks)
        def _():
            dst_ref[...] = jnp.zeros(dst_ref.shape, dst_ref.dtype)
    else:
        (o_ref,) = rest
    h = h_ref[0]
    g = jnp.dot(h, wg_ref[...], preferred_element_type=F32)
    u = jnp.dot(h, wu_ref[...], preferred_element_type=F32)
    o_ref[0] = (_silu(g) * u).astype(o_ref.dtype)


def _gateup(h, wg, wu, wd):
    bsz, l, d = h.shape
    n = wg.shape[1]
    tm, tn = _tile(l, 1024), _tile(n, 512)
    grid = (bsz, l // tm, n // tn)
    steps = grid[0] * grid[1] * grid[2]
    wspec = pl.BlockSpec((d, tn), lambda b, i, j: (0, j))
    in_specs = [pl.BlockSpec((1, tm, d), lambda b, i, j: (b, i, 0)), wspec, wspec]
    out_specs = [pl.BlockSpec((1, tm, tn), lambda b, i, j: (b, i, j))]
    out_shape = [jax.ShapeDtypeStruct((bsz, l, n), BF16)]
    args = [h, wg, wu]
    rb = n // steps
    rides = n % steps == 0 and rb % 16 == 0 and wd.shape[0] % rb == 0
    src_blocks = wd.shape[0] // rb if rides else 0
    if rides:
        lin = lambda b, i, j: (b * grid[1] + i) * grid[2] + j
        in_specs.append(pl.BlockSpec((rb, wd.shape[1]),
                                     lambda b, i, j: (jnp.minimum(lin(b, i, j), src_blocks - 1), 0)))
        out_specs.append(pl.BlockSpec((rb, wd.shape[1]), lambda b, i, j: (lin(b, i, j), 0)))
        out_shape.append(jax.ShapeDtypeStruct((n, wd.shape[1]), BF16))
        args.append(wd)
    outs = pl.pallas_call(
        functools.partial(_gateup_kernel, src_blocks=src_blocks),
        grid=grid,
        in_specs=in_specs,
        out_specs=out_specs,
        out_shape=out_shape,
        compiler_params=_cparams(("parallel", "parallel", "arbitrary")),
        name="gateup",
    )(*args)
    if rides:
        return outs[0], outs[1]
    return outs[0], jnp.pad(wd.astype(BF16), ((0, n - wd.shape[0]), (0, 0)))


def _down_kernel(g_ref, w_ref, x_ref, gt_ref, gf_ref, o_ref, *, final_eps):
    k = pl.program_id(2)
    d = o_ref.shape[2]
    tn = _tile(d, 1024)
    chunks = [slice(c * tn, (c + 1) * tn) for c in range(d // tn)]
    part = lambda cols: jnp.dot(g_ref[0], w_ref[:, cols], preferred_element_type=F32)

    @pl.when(k == 0)
    def _():
        for cols in chunks:
            o_ref[0, :, cols] = part(cols)

    @pl.when(k > 0)
    def _():
        for cols in chunks:
            o_ref[0, :, cols] += part(cols)

    @pl.when(k == pl.num_programs(2) - 1)
    def _():
        def finish(r, carry):
            rows = pl.ds(pl.multiple_of(r * EPILOGUE_ROWS, EPILOGUE_ROWS), EPILOGUE_ROWS)
            x2 = x_ref[0, rows, :] + gt_ref[0] * o_ref[0, rows, :]
            if final_eps is not None:
                ms = jnp.mean(x2 * x2, axis=-1, keepdims=True)
                x2 = x2 * lax.rsqrt(ms + final_eps) * gf_ref[...]
            o_ref[0, rows, :] = x2
            return carry

        lax.fori_loop(0, o_ref.shape[1] // EPILOGUE_ROWS, finish, 0)


def _down(g, w, x, gt, g_final, final_eps):
    bsz, l, d = x.shape
    kdim = g.shape[2]
    tm, tk = _tile(l, 512), _tile(kdim, 1024)
    return pl.pallas_call(
        functools.partial(_down_kernel, final_eps=final_eps),
        grid=(bsz, l // tm, kdim // tk),
        in_specs=[pl.BlockSpec((1, tm, tk), lambda b, i, k: (b, i, k)),
                  pl.BlockSpec((tk, d), lambda b, i, k: (k, 0)),
                  pl.BlockSpec((1, tm, d), lambda b, i, k: (b, i, 0)),
                  pl.BlockSpec((1, 1, d), lambda b, i, k: (b, 0, 0)),
                  pl.BlockSpec((1, d), lambda b, i, k: (0, 0))],
        out_specs=pl.BlockSpec((1, tm, d), lambda b, i, k: (b, i, 0)),
        out_shape=jax.ShapeDtypeStruct((bsz, l, d), F32),
        compiler_params=_cparams(("parallel", "parallel", "arbitrary")),
        name="down",
    )(g, w, x, gt, g_final.reshape(1, d))


def kernel(x, c, positions, w_ada, b_ada, g_mix, g_ffn, w_in, conv_w, conv_b, f_w1, f_b1, f_w2, f_b2, f_w3, f_b3, f_w4, f_freq, hyena_bias, lambda_q1, lambda_k1, lambda_q2, lambda_k2, subln_g, w_out, w_gate, w_up, w_down, g_final):
    bsz, l, d = x.shape
    depth = w_ada.shape[0]
    ch = hyena_bias.shape[-1]
    hd = lambda_q1.shape[-1]
    qk = (w_in.shape[-1] - 3 * ch - (d - ch)) // 2
    cos, sin = _rope_tables(positions, hd)
    q_scale = hd ** -0.5 * math.log2(math.e)
    for i in range(depth):
        lam_init = 0.8 - 0.6 * math.exp(-0.3 * i)
        mod, kern_t = _ada_and_taps(c, w_ada[i], b_ada[i], l, ch, f_w1[i], f_b1[i], f_w2[i], f_b2[i],
                                    f_w3[i], f_b3[i], f_w4[i], f_freq[i])
        sh1, sc1, gt1, sh2, sc2, gt2 = [mod[:, None, j * d:(j + 1) * d] for j in range(N_MOD)]
        h = _norm_mod(x, g_mix[i], sc1, sh1, NORM_EPS)
        w = w_in[i].astype(BF16)
        o1, o3 = 3 * ch, 3 * ch + 2 * qk
        u_t = _proj_t(h, w, 0, o1, kern_t.shape[-1], F32)
        qk_scale = jnp.concatenate([jnp.full((1, qk), q_scale, F32), jnp.ones((1, qk), F32)], axis=1)
        q_k = _proj(h, w, o1, 2 * qk, rope=(cos, sin), col_scale=qk_scale)
        v = _proj(h, w, o3, w.shape[1] - o3)
        y_hy = _hyena(u_t, kern_t, conv_w[i], conv_b[i], hyena_bias[i], BF16)
        hpad = -w_gate.shape[-1] % 1024
        y_da, (wo, wg, wu) = _diff_attention(
            q_k, v, lambda_q1[i], lambda_k1[i], lambda_q2[i], lambda_k2[i], subln_g[i], lam_init,
            cast=((w_out[i], 0), (w_gate[i], hpad), (w_up[i], hpad)))
        x = _outproj(y_hy, y_da, wo, x, gt1)
        h = _norm_mod(x, g_ffn[i], sc2, sh2, NORM_EPS)
        g, wd = _gateup(h, wg, wu, w_down[i])
        last = i == depth - 1
        x = _down(g, wd, x, gt2, g_final, NORM_EPS if last else None)
    return x
```

```python
import functools
import math

import numpy as np
import jax
import jax.numpy as jnp
from jax import lax
from jax.experimental import pallas as pl
from jax.experimental.pallas import tpu as pltpu

F32 = jnp.float32
BF16 = jnp.bfloat16

NORM_EPS = 1e-6
SUBLN_EPS = 1e-5
ROPE_THETA = 10000.0
FAST_DECAY_PCT = 0.3
SLOW_DECAY_PCT = 1.5
DECAY_TARGET = 1e-2
N_MOD = 6

V7X_VMEM_LIMIT_BYTES = 56 * 1024 * 1024
LANES = 128
EPILOGUE_ROWS = 16
ADA_TILE_BYTES = 13 * 512 * 1024


def _cparams(sem):
    return pltpu.CompilerParams(dimension_semantics=sem, vmem_limit_bytes=V7X_VMEM_LIMIT_BYTES)


def _tile(n, pref):
    t = min(n, pref)
    while n % t:
        t -= 1
    return t


def _weight_cols(w):
    return 1024 if w.dtype == BF16 else 512


def _silu(x):
    return x * (1.0 / (1.0 + jnp.exp(-x)))


def _split_bf16(a):
    hi = a.astype(BF16)
    lo = (a - hi.astype(F32)).astype(BF16)
    return hi, lo


def _dot3(a, b):
    ah, al = _split_bf16(a)
    bh, bl = _split_bf16(b)
    d = functools.partial(jnp.dot, preferred_element_type=F32)
    return d(ah, bh) + (d(ah, bl) + d(al, bh))


def _ada_kernel(c_ref, w_ref, b_ref, o_ref):
    o_ref[...] = _dot3(_silu(c_ref[...]), w_ref[...]) + b_ref[...]


def _norm_mod_kernel(x_ref, g_ref, sc_ref, sh_ref, o_ref, *, eps):
    x = x_ref[0]
    y = x * lax.rsqrt(jnp.mean(x * x, axis=-1, keepdims=True) + eps) * g_ref[...]
    o_ref[0] = (y * (1.0 + sc_ref[0]) + sh_ref[0]).astype(o_ref.dtype)


def _norm_mod(x, g, sc, sh, eps):
    bsz, l, d = x.shape
    tr = _tile(l, 512)
    return pl.pallas_call(
        functools.partial(_norm_mod_kernel, eps=eps),
        grid=(bsz, l // tr),
        in_specs=[pl.BlockSpec((1, tr, d), lambda b, i: (b, i, 0)),
                  pl.BlockSpec((1, d), lambda b, i: (0, 0)),
                  pl.BlockSpec((1, 1, d), lambda b, i: (b, 0, 0)),
                  pl.BlockSpec((1, 1, d), lambda b, i: (b, 0, 0))],
        out_specs=pl.BlockSpec((1, tr, d), lambda b, i: (b, i, 0)),
        out_shape=jax.ShapeDtypeStruct((bsz, l, d), BF16),
        compiler_params=_cparams(("parallel", "parallel")),
        name="norm_mod",
    )(x, g.reshape(1, d), sc, sh)


def _rope_kernel(pos_ref, inv_ref, sgn_ref, cos_ref, sin_ref):
    ang = pos_ref[0].astype(F32) * inv_ref[...]
    cos_ref[0] = jnp.cos(ang)
    sin_ref[0] = jnp.sin(ang) * sgn_ref[...]


def _rope_tables(positions, dim):
    bsz, l = positions.shape
    half = dim // 2
    inv = 1.0 / (ROPE_THETA ** (jnp.arange(0, dim, 2, dtype=F32) / dim))
    inv = jnp.concatenate([inv, inv]).reshape(1, dim)
    sgn = jnp.asarray(np.concatenate([-np.ones(half), np.ones(half)]).reshape(1, dim), F32)
    tr = _tile(l, 1024)
    shp = jax.ShapeDtypeStruct((bsz, l, dim), F32)
    return pl.pallas_call(
        _rope_kernel,
        grid=(bsz, l // tr),
        in_specs=[pl.BlockSpec((1, tr, 1), lambda b, i: (b, i, 0)),
                  pl.BlockSpec((1, dim), lambda b, i: (0, 0)),
                  pl.BlockSpec((1, dim), lambda b, i: (0, 0))],
        out_specs=[pl.BlockSpec((1, tr, dim), lambda b, i: (b, i, 0))] * 2,
        out_shape=[shp, shp],
        compiler_params=_cparams(("parallel", "parallel")),
        name="rope_tables",
    )(positions.reshape(bsz, l, 1), inv, sgn)


def _proj_t_kernel(w_ref, h_ref, o_ref):
    res = lax.dot_general(w_ref[...].astype(BF16), h_ref[0], (((0,), (1,)), ((), ())),
                          preferred_element_type=F32)
    o_ref[0] = res.reshape(o_ref.shape[1:]).astype(o_ref.dtype)


def _proj_t(h, w, col0, n, r, out_dtype):
    bsz, l, d = h.shape
    tm, tn = _tile(l, 1024), _tile(math.gcd(n, col0) if col0 else n, _weight_cols(w))
    assert tm % (8 * r) == 0
    j0 = col0 // tn
    return pl.pallas_call(
        _proj_t_kernel,
        grid=(bsz, l // tm, n // tn),
        in_specs=[pl.BlockSpec((d, tn), lambda b, i, j: (0, j + j0)),
                  pl.BlockSpec((1, tm, d), lambda b, i, j: (b, i, 0))],
        out_specs=pl.BlockSpec((1, tn, tm // r, r), lambda b, i, j: (b, j, i, 0)),
        out_shape=jax.ShapeDtypeStruct((bsz, n, l // r, r), out_dtype),
        compiler_params=_cparams(("parallel", "parallel", "arbitrary")),
        name="proj_t",
    )(w, h)


def _proj_kernel(h_ref, w_ref, *rest, rope_dim):
    acc = jnp.dot(h_ref[0], w_ref[...].astype(BF16), preferred_element_type=F32)
    if rope_dim:
        cos_ref, sin_ref, scale_ref, o_ref = rest
        cos, sin = cos_ref[0], sin_ref[0]
        for g in range(acc.shape[1] // rope_dim):
            cols = slice(g * rope_dim, (g + 1) * rope_dim)
            xg = acc[:, cols]
            yg = xg * cos + pltpu.roll(xg, rope_dim // 2, 1) * sin
            o_ref[0, :, cols] = (yg * scale_ref[:, cols]).astype(o_ref.dtype)
    else:
        (o_ref,) = rest
        o_ref[0] = acc.astype(o_ref.dtype)


def _proj(h, w, col0, n, rope=None, col_scale=None):
    bsz, l, d = h.shape
    tm, tn = _tile(l, 1024), _tile(math.gcd(n, col0) if col0 else n, _weight_cols(w))
    j0 = col0 // tn
    in_specs = [pl.BlockSpec((1, tm, d), lambda b, i, j: (b, i, 0)),
                pl.BlockSpec((d, tn), lambda b, i, j: (0, j + j0))]
    args = [h, w]
    rope_dim = 0
    if rope is not None:
        rope_dim = rope[0].shape[-1]
        in_specs += [pl.BlockSpec((1, tm, rope_dim), lambda b, i, j: (b, i, 0))] * 2
        in_specs += [pl.BlockSpec((1, tn), lambda b, i, j: (0, j))]
        args += [*rope, col_scale]
    return pl.pallas_call(
        functools.partial(_proj_kernel, rope_dim=rope_dim),
        grid=(bsz, l // tm, n // tn),
        in_specs=in_specs,
        out_specs=pl.BlockSpec((1, tm, tn), lambda b, i, j: (b, i, j)),
        out_shape=jax.ShapeDtypeStruct((bsz, l, n), BF16),
        compiler_params=_cparams(("parallel", "parallel", "arbitrary")),
        name="proj_rope" if rope_dim else "proj",
    )(*args)


def _filt_kernel(z_ref, t_ref, w1_ref, b1_ref, w2_ref, b2_ref, w3_ref, b3_ref, fr_ref,
                 w4_ref, ad_ref, o_ref):
    fr = fr_ref[...]
    h = jnp.sin(fr * (_dot3(w1_ref[...], z_ref[...]) + b1_ref[...]))
    h = jnp.sin(fr * (_dot3(w2_ref[...], h) + b2_ref[...]))
    h = jnp.sin(fr * (_dot3(w3_ref[...], h) + b3_ref[...]))
    t = t_ref[...]
    decay = jnp.exp(-(ad_ref[...] * t[0:1, :]))
    o_ref[...] = (_dot3(w4_ref[0], h) * decay * t[1:2, :]).reshape(o_ref.shape)


def _ada_filt_kernel(c_ref, wa_ref, ba_ref, *rest, stride):
    mod_ref, kern_ref = rest[-2:]
    _ada_kernel(c_ref, wa_ref, ba_ref, mod_ref)

    @pl.when(pl.program_id(0) % stride == 0)
    def _():
        _filt_kernel(*rest[:-2], kern_ref)


def _ada_and_taps(cvec, w_ada, b_ada, l, c, f_w1, f_b1, f_w2, f_b2, f_w3, f_b3, f_w4, f_freq):
    n = 2 * l
    emb, hid = f_w1.shape
    bands = (emb - 1) // 2
    t = jnp.linspace(0.0, 1.0, l, dtype=F32)[:, None]
    w = 2.0 * math.pi * jnp.arange(l, dtype=F32)[:, None] / l
    f = jnp.linspace(1e-4, bands - 1, bands, dtype=F32)[None, :]
    z = jnp.concatenate([t, jnp.cos(f * w), -jnp.sin(f * w)], axis=-1)
    cols = lambda a: jnp.concatenate([a, a[:1], a[:0:-1]], axis=0).T
    embp = -(-emb // 8) * 8
    z = jnp.pad(cols(z), ((0, embp - emb), (0, 0)))
    mask = jnp.ones((1, n), F32).at[0, l].set(0.0)
    tm = jnp.concatenate([cols(t), mask], axis=0)
    min_decay = math.log(DECAY_TARGET) / SLOW_DECAY_PCT
    max_decay = math.log(DECAY_TARGET) / FAST_DECAY_PCT
    ad = jnp.abs(jnp.linspace(min_decay, max_decay, c, dtype=F32)).reshape(c, 1)
    w1t = jnp.pad(f_w1.T, ((0, 0), (0, embp - emb)))
    w4t = f_w4.T.reshape(2, c, hid)
    col = lambda a: a.reshape(hid, 1)
    bsz, d = cvec.shape
    n_mod = w_ada.shape[1]
    r = int(round(math.sqrt(n)))
    assert r * r == n
    common = math.gcd(n_mod // LANES, n // LANES)
    fits = lambda s: common % s == 0 and s % 2 == 0 and (n_mod // s) * d * 4 <= ADA_TILE_BYTES
    nf = n // (8 * r)
    nt = next((s for s in range(nf, common + 1, nf) if fits(s)), 0) if nf and nf % 2 == 0 else 0
    folded = nt > 0
    if not folded:
        nt = next((s for s in range(2, common + 1, 2) if fits(s)), common)
        nf = nt
    stride = nt // nf
    tn, tn_mod = n // nf, n_mod // nt
    rows = 8
    cp = jnp.zeros((rows, d), F32).at[:bsz].set(cvec)
    full = lambda shape: pl.BlockSpec(shape, lambda i: (0,) * len(shape))
    if folded:
        kern_spec = pl.BlockSpec((c, 8, r), lambda i: (0, i // stride, 0))
        kern_shape = jax.ShapeDtypeStruct((c, r, r), F32)
    else:
        kern_spec = pl.BlockSpec((c, tn), lambda i: (0, i // stride))
        kern_shape = jax.ShapeDtypeStruct((c, n), F32)
    mod, kern = pl.pallas_call(
        functools.partial(_ada_filt_kernel, stride=stride),
        grid=(nt,),
        in_specs=[full((rows, d)),
                  pl.BlockSpec((d, tn_mod), lambda i: (0, i)),
                  pl.BlockSpec((1, tn_mod), lambda i: (0, i)),
                  pl.BlockSpec((embp, tn), lambda i: (0, i // stride)),
                  pl.BlockSpec((2, tn), lambda i: (0, i // stride)),
                  full((hid, embp)), full((hid, 1)), full((hid, hid)), full((hid, 1)),
                  full((hid, hid)), full((hid, 1)), full((hid, 1)),
                  pl.BlockSpec((1, c, hid), lambda i: ((2 * (i // stride)) // nf, 0, 0)),
                  full((c, 1))],
        out_specs=[pl.BlockSpec((rows, tn_mod), lambda i: (0, i)), kern_spec],
        out_shape=[jax.ShapeDtypeStruct((rows, n_mod), F32), kern_shape],
        compiler_params=_cparams(("arbitrary",)),
        name="ada_filter",
    )(cp, w_ada, b_ada.reshape(1, n_mod), z, tm, w1t, col(f_b1), f_w2.T, col(f_b2), f_w3.T,
      col(f_b3), col(f_freq), w4t, ad)
    return mod[:bsz], kern.reshape(c, r, r)


def _dft_consts(r, tc):
    half = r // 2
    idx = np.arange(r)
    ang = -2.0 * np.pi * np.outer(idx, idx) / r
    fr, fi = np.cos(ang), np.sin(ang)
    angt = -2.0 * np.pi * np.outer(idx, idx) / (r * r)
    fa = np.block([[fr[:, :half], -fi[:, :half]], [fi[:, :half], fr[:, :half]]])
    fk = np.concatenate([fr, fi], axis=0)
    g = np.concatenate([fr, fi], axis=1)
    fin = np.block([[fr[:half], fi[:half]], [-fi[:half], fr[:half]]]) / float(r * r)
    mx = lambda a: jnp.asarray(a, F32).astype(BF16)
    bc = lambda a: jnp.broadcast_to(mx(a)[None], (tc,) + a.shape)
    return (bc(fa), bc(fk), mx(g), jnp.asarray(np.cos(angt), F32),
            jnp.asarray(np.sin(angt), F32), bc(fin))


def _shift_prev(u, lane, row):
    nr, nl = u.shape[-2], u.shape[-1]
    a = pltpu.roll(u, 1, u.ndim - 1)
    b = pltpu.roll(a, 1, u.ndim - 2)
    p = jnp.where(lane == 0, b, a)
    return jnp.where((lane == 0) & (row == 0), 0.0, p)


def _shift_next(u, lane, row):
    nr, nl = u.shape[-2], u.shape[-1]
    a = pltpu.roll(u, nl - 1, u.ndim - 1)
    b = pltpu.roll(a, nr - 1, u.ndim - 2)
    p = jnp.where(lane == nl - 1, b, a)
    return jnp.where((lane == nl - 1) & (row == nr - 1), 0.0, p)


def _lane_stage(y2, g_ref):
    tc, r2, r = y2.shape
    p = jnp.dot(y2.reshape(tc * r2, r).astype(BF16), g_ref[...], preferred_element_type=F32)
    p = p.reshape(tc, r2, r2)
    return p[:, :r, :r], p[:, :r, r:], p[:, r:, :r], p[:, r:, r:]


def _fwd_fft(x2, f1_ref, g_ref, tr, ti):
    r = tr.shape[0]
    a = jnp.einsum("cmk,ckr->cmr", f1_ref[...], x2.astype(BF16), preferred_element_type=F32)
    ar, ai = a[:, :r], a[:, r:]
    br = ar * tr - ai * ti
    bi = ar * ti + ai * tr
    p00, p01, p10, p11 = _lane_stage(jnp.concatenate([br, bi], axis=1), g_ref)
    return p00 - p11, p01 + p10


def _hyena_kernel(x0_ref, x1_ref, v_ref, w0_ref, w1_ref, wv_ref, b0_ref, b1_ref, bv_ref,
                  hb_ref, k_ref, fa_ref, fk_ref, g_ref, tr_ref, ti_ref, fin_ref, o_ref):
    shape = x0_ref.shape[1:]
    lane = lax.broadcasted_iota(jnp.int32, shape, 2)
    row = lax.broadcasted_iota(jnp.int32, shape, 1)
    tr, ti = tr_ref[...], ti_ref[...]

    def sconv(u_ref, w_ref, b_ref, b):
        u = u_ref[b].astype(F32)
        return (b_ref[...] + w_ref[0] * _shift_prev(u, lane, row) + w_ref[1] * u
                + w_ref[2] * _shift_next(u, lane, row))

    nb = x0_ref.shape[0]
    x0 = [sconv(x0_ref, w0_ref, b0_ref, b) for b in range(nb)]
    vx = [sconv(v_ref, wv_ref, bv_ref, b) * sconv(x1_ref, w1_ref, b1_ref, b) for b in range(nb)]

    kr, ki = _fwd_fft(k_ref[...], fk_ref, g_ref, tr, ti)
    for b0 in range(0, nb, 2):
        pair = vx[b0:b0 + 2]
        xi = pair[1] if len(pair) == 2 else jnp.zeros_like(pair[0])
        sr, si = _fwd_fft(jnp.concatenate([pair[0], xi], axis=1), fa_ref, g_ref, tr, ti)
        yr = sr * kr - si * ki
        yi = sr * ki + si * kr
        p00, p01, p10, p11 = _lane_stage(jnp.concatenate([yr, yi], axis=1), g_ref)
        cr, ci = p00 + p11, p10 - p01
        dr = cr * tr + ci * ti
        di = ci * tr - cr * ti
        d2 = jnp.concatenate([dr, di], axis=1).astype(BF16)
        y = jnp.einsum("cmk,ckr->cmr", fin_ref[...], d2, preferred_element_type=F32)
        half = shape[1]
        for j, yb in enumerate((y[:, :half], y[:, half:])[:len(pair)]):
            b = b0 + j
            res = (yb + vx[b] * hb_ref[...]) * x0[b]
            o_ref[b] = res.reshape(res.shape[0], res.shape[1] * res.shape[2]).astype(o_ref.dtype)


def _hyena(u4, k3, conv_w, conv_b, hyena_bias, out_dtype):
    bsz, c3, half, r = u4.shape
    c = c3 // 3
    l = half * r
    assert r == 2 * half and k3.shape == (c, r, r)
    tc = _tile(c, 16)
    nc = c // tc
    consts = _dft_consts(r, tc)
    cw = conv_w.reshape(conv_w.shape[0], c3, 1, 1)
    cb = conv_b.reshape(c3, 1, 1)
    hb = hyena_bias.reshape(c, 1, 1)
    stream = lambda s: pl.BlockSpec((bsz, tc, half, r), lambda j, s=s: (0, j + s * nc, 0, 0))
    wspec = lambda s: pl.BlockSpec((conv_w.shape[0], tc, 1, 1), lambda j, s=s: (0, j + s * nc, 0, 0))
    bspec = lambda s: pl.BlockSpec((tc, 1, 1), lambda j, s=s: (j + s * nc, 0, 0))
    cspec = lambda a: pl.BlockSpec(a.shape, lambda j, nd=a.ndim: (0,) * nd)
    return pl.pallas_call(
        _hyena_kernel,
        grid=(nc,),
        in_specs=[stream(0), stream(1), stream(2), wspec(0), wspec(1), wspec(2),
                  bspec(0), bspec(1), bspec(2), bspec(0),
                  pl.BlockSpec((tc, r, r), lambda j: (j, 0, 0))] + [cspec(a) for a in consts],
        out_specs=pl.BlockSpec((bsz, tc, l), lambda j: (0, j, 0)),
        out_shape=jax.ShapeDtypeStruct((bsz, c, l), out_dtype),
        compiler_params=_cparams(("parallel",)),
        name="hyena",
    )(u4, u4, u4, cw, cw, cw, cb, cb, cb, hb, k3, *consts)


def _rep(x, n):
    if n % LANES:
        return x[:, :n]
    return x if n == LANES else jnp.concatenate([x] * (n // LANES), axis=1)


def _attn_kernel(q_ref, k_ref, v_ref, lq1_ref, lk1_ref, lq2_ref, lk2_ref, g_ref, *rest,
                 ncast, hd, tk, lam_init, eps):
    cast_in, o_ref, cast_out = rest[:ncast], rest[ncast], rest[ncast + 1:2 * ncast + 1]
    s_ref, mb_ref, m_ref, l_ref, acc_ref = rest[2 * ncast + 1:]
    for src, dst in zip(cast_in, cast_out):
        c = src.shape[1]
        dst[:, :c] = src[...].astype(dst.dtype)
        if dst.shape[1] > c:
            dst[:, c:] = jnp.zeros((dst.shape[0], dst.shape[1] - c), dst.dtype)
    nkv = k_ref.shape[1] // tk
    tq = q_ref.shape[1]
    hw = v_ref.shape[2]

    def scores(t, slot, j):
        qj = q_ref[0, :, j * hd:(j + 1) * hd]
        kj = k_ref[0, t * tk:(t + 1) * tk, j * hd:(j + 1) * hd]
        s = lax.dot_general(qj, kj, (((1,), (1,)), ((), ())), preferred_element_type=F32)
        s_ref[slot, j] = s
        mb_ref[slot, j] = jnp.broadcast_to(jnp.max(s, axis=-1, keepdims=True), (tq, LANES))

    def consume(t, slot, j):
        v = v_ref[0, t * tk:(t + 1) * tk, :]
        m_prev = m_ref[j]
        m_new = jnp.maximum(m_prev, mb_ref[slot, j])
        alpha = jnp.exp2(m_prev - m_new)
        ps = [jnp.exp2(s_ref[slot, j, :, c * LANES:(c + 1) * LANES] - m_new)
              for c in range(tk // LANES)]
        l_ref[j] = alpha * l_ref[j] + functools.reduce(lambda a, b: a + b, ps)
        p = jnp.concatenate(ps, axis=1).astype(v.dtype)
        acc_ref[j] = _rep(alpha, hw) * acc_ref[j] + jnp.dot(p, v, preferred_element_type=F32)
        m_ref[j] = m_new

    m_ref[...] = jnp.full(m_ref.shape, -jnp.inf, F32)
    l_ref[...] = jnp.zeros(l_ref.shape, F32)
    acc_ref[...] = jnp.zeros(acc_ref.shape, F32)
    scores(0, 0, 0)
    scores(0, 0, 1)
    for t in range(nkv):
        for j in range(2):
            if t + 1 < nkv:
                scores(t + 1, (t + 1) % 2, j)
            consume(t, t % 2, j)

    lam = (jnp.exp(jnp.sum(lq1_ref[...] * lk1_ref[...], axis=-1, keepdims=True))
           - jnp.exp(jnp.sum(lq2_ref[...] * lk2_ref[...], axis=-1, keepdims=True)) + lam_init)
    l0 = jnp.sum(l_ref[0], axis=-1, keepdims=True)
    l1 = jnp.sum(l_ref[1], axis=-1, keepdims=True)
    o = acc_ref[0] / l0 - lam * (acc_ref[1] / l1)
    o = o * lax.rsqrt(jnp.mean(o * o, axis=-1, keepdims=True) + eps) * g_ref[...]
    o_ref[0] = (o * (1.0 - lam_init)).astype(o_ref.dtype)


def _cast_blocks(shape, steps, full_rows):
    r, c = shape
    for a in range(steps, 0, -1):
        b = steps // a
        if steps % a or r % a or c % b or (full_rows and b > 1):
            continue
        rb, cb = r // a, c // b
        if rb % 16 == 0 and cb % LANES == 0:
            return rb, cb, b
    return None


def _diff_attention(qk, v, lq1, lk1, lq2, lk2, subln_g, lam_init, cast=()):
    bsz, l, width = v.shape
    hd = lq1.shape[-1]
    hw = 2 * hd
    heads = width // hw
    tq, tk = _tile(l, 512), _tile(l, 1024)
    nq = l // tq
    steps = bsz * heads * nq
    vec = lambda a: a.reshape(1, -1).astype(F32)
    vspec = lambda n: pl.BlockSpec((1, n), lambda b, h, i: (0, 0))
    plans = [_cast_blocks(a.shape, steps, pad > 0) for a, pad in cast]
    riders = [(a, pad, p) for (a, pad), p in zip(cast, plans) if p is not None]

    def cast_spec(p, pad):
        rb, cb, ncb = p
        return pl.BlockSpec((rb, cb + pad), lambda b, h, i: (((b * heads + h) * nq + i) // ncb,
                                                              ((b * heads + h) * nq + i) % ncb))

    cast_in_specs = [cast_spec(p, 0) for _, _, p in riders]
    cast_out_specs = [cast_spec(p, pad) for _, pad, p in riders]
    outs = pl.pallas_call(
        functools.partial(_attn_kernel, ncast=len(riders), hd=hd, tk=tk, lam_init=lam_init,
                          eps=SUBLN_EPS),
        grid=(bsz, heads, nq),
        in_specs=[pl.BlockSpec((1, tq, hw), lambda b, h, i: (b, i, h)),
                  pl.BlockSpec((1, l, hw), lambda b, h, i: (b, 0, heads + h)),
                  pl.BlockSpec((1, l, hw), lambda b, h, i: (b, 0, h)),
                  vspec(hd), vspec(hd), vspec(hd), vspec(hd), vspec(hw)] + cast_in_specs,
        out_specs=[pl.BlockSpec((1, tq, hw), lambda b, h, i: (b, i, h))] + cast_out_specs,
        out_shape=[jax.ShapeDtypeStruct((bsz, l, width), BF16)]
                  + [jax.ShapeDtypeStruct((a.shape[0], a.shape[1] + pad), BF16)
                     for a, pad, _ in riders],
        scratch_shapes=[pltpu.VMEM((2, 2, tq, tk), F32), pltpu.VMEM((2, 2, tq, LANES), F32),
                        pltpu.VMEM((2, tq, LANES), F32), pltpu.VMEM((2, tq, LANES), F32),
                        pltpu.VMEM((2, tq, hw), F32)],
        compiler_params=_cparams(("parallel", "parallel", "arbitrary")),
        name="diff_attn",
    )(qk, qk, v, vec(lq1), vec(lk1), vec(lq2), vec(lk2), vec(subln_g), *[a for a, _, _ in riders])
    rounded = iter(outs[1:])
    return outs[0], [next(rounded) if p is not None else jnp.pad(a.astype(BF16), ((0, 0), (0, pad)))
                     for (a, pad), p in zip(cast, plans)]


def _outproj_kernel(yh_ref, yd_ref, w1_ref, w2_ref, x_ref, gt_ref, o_ref):
    acc = lax.dot_general(yh_ref[0], w1_ref[...].astype(BF16), (((0,), (0,)), ((), ())),
                          preferred_element_type=F32)
    acc = acc + jnp.dot(yd_ref[0], w2_ref[...].astype(BF16), preferred_element_type=F32)
    o_ref[0] = x_ref[0] + gt_ref[0] * acc


def _outproj(yh_t, yd, w, x, gt):
    bsz, l, d = x.shape
    c, kd = yh_t.shape[1], yd.shape[2]
    assert c == kd and w.shape[0] == c + kd
    tm, tn = _tile(l, 1024), _tile(d, _weight_cols(w))
    return pl.pallas_call(
        _outproj_kernel,
        grid=(bsz, l // tm, d // tn),
        in_specs=[pl.BlockSpec((1, c, tm), lambda b, i, j: (b, 0, i)),
                  pl.BlockSpec((1, tm, kd), lambda b, i, j: (b, i, 0)),
                  pl.BlockSpec((c, tn), lambda b, i, j: (0, j)),
                  pl.BlockSpec((kd, tn), lambda b, i, j: (1, j)),
                  pl.BlockSpec((1, tm, tn), lambda b, i, j: (b, i, j)),
                  pl.BlockSpec((1, 1, tn), lambda b, i, j: (b, 0, j))],
        out_specs=pl.BlockSpec((1, tm, tn), lambda b, i, j: (b, i, j)),
        out_shape=jax.ShapeDtypeStruct((bsz, l, d), F32),
        compiler_params=_cparams(("parallel", "parallel", "arbitrary")),
        name="outproj",
    )(yh_t, yd, w, w, x, gt)


def _gateup_kernel(h_ref, wg_ref, wu_ref, *rest, src_blocks):
    if src_blocks:
        src_ref, o_ref, dst_ref = rest
        step = (pl.program_id(0) * pl.num_programs(1) + pl.program_id(1)) * pl.num_programs(2) \
            + pl.program_id(2)

        @pl.when(step < src_blocks)
        def _():
            dst_ref[...] = src_ref[...].astype(dst_ref.dtype)

        @pl.when(step >= src_blocks)
        def _():
            dst_ref[...] = jnp.zeros(dst_ref.shape, dst_ref.dtype)
    else:
        (o_ref,) = rest
    h = h_ref[0]
    g = jnp.dot(h, wg_ref[...], preferred_element_type=F32)
    u = jnp.dot(h, wu_ref[...], preferred_element_type=F32)
    o_ref[0] = (_silu(g) * u).astype(o_ref.dtype)


def _gateup(h, wg, wu, wd):
    bsz, l, d = h.shape
    n = wg.shape[1]
    tm, tn = _tile(l, 1024), _tile(n, 512)
    grid = (bsz, l // tm, n // tn)
    steps = grid[0] * grid[1] * grid[2]
    wspec = pl.BlockSpec((d, tn), lambda b, i, j: (0, j))
    in_specs = [pl.BlockSpec((1, tm, d), lambda b, i, j: (b, i, 0)), wspec, wspec]
    out_specs = [pl.BlockSpec((1, tm, tn), lambda b, i, j: (b, i, j))]
    out_shape = [jax.ShapeDtypeStruct((bsz, l, n), BF16)]
    args = [h, wg, wu]
    rb = n // steps
    rides = n % steps == 0 and rb % 16 == 0 and wd.shape[0] % rb == 0
    src_blocks = wd.shape[0] // rb if rides else 0
    if rides:
        lin = lambda b, i, j: (b * grid[1] + i) * grid[2] + j
        in_specs.append(pl.BlockSpec((rb, wd.shape[1]),
                                     lambda b, i, j: (jnp.minimum(lin(b, i, j), src_blocks - 1), 0)))
        out_specs.append(pl.BlockSpec((rb, wd.shape[1]), lambda b, i, j: (lin(b, i, j), 0)))
        out_shape.append(jax.ShapeDtypeStruct((n, wd.shape[1]), BF16))
        args.append(wd)
    outs = pl.pallas_call(
        functools.partial(_gateup_kernel, src_blocks=src_blocks),
        grid=grid,
        in_specs=in_specs,
        out_specs=out_specs,
        out_shape=out_shape,
        compiler_params=_cparams(("parallel", "parallel", "arbitrary")),
        name="gateup",
    )(*args)
    if rides:
        return outs[0], outs[1]
    return outs[0], jnp.pad(wd.astype(BF16), ((0, n - wd.shape[0]), (0, 0)))


def _down_kernel(g_ref, w_ref, x_ref, gt_ref, gf_ref, o_ref, *, final_eps):
    k = pl.program_id(2)
    d = o_ref.shape[2]
    tn = _tile(d, 1024)
    chunks = [slice(c * tn, (c + 1) * tn) for c in range(d // tn)]
    part = lambda cols: jnp.dot(g_ref[0], w_ref[:, cols], preferred_element_type=F32)

    @pl.when(k == 0)
    def _():
        for cols in chunks:
            o_ref[0, :, cols] = part(cols)

    @pl.when(k > 0)
    def _():
        for cols in chunks:
            o_ref[0, :, cols] += part(cols)

    @pl.when(k == pl.num_programs(2) - 1)
    def _():
        def finish(r, carry):
            rows = pl.ds(pl.multiple_of(r * EPILOGUE_ROWS, EPILOGUE_ROWS), EPILOGUE_ROWS)
            x2 = x_ref[0, rows, :] + gt_ref[0] * o_ref[0, rows, :]
            if final_eps is not None:
                ms = jnp.mean(x2 * x2, axis=-1, keepdims=True)
                x2 = x2 * lax.rsqrt(ms + final_eps) * gf_ref[...]
            o_ref[0, rows, :] = x2
            return carry

        lax.fori_loop(0, o_ref.shape[1] // EPILOGUE_ROWS, finish, 0)


def _down(g, w, x, gt, g_final, final_eps):
    bsz, l, d = x.shape
    kdim = g.shape[2]
    tm, tk = _tile(l, 512), _tile(kdim, 1024)
    return pl.pallas_call(
        functools.partial(_down_kernel, final_eps=final_eps),
        grid=(bsz, l // tm, kdim // tk),
        in_specs=[pl.BlockSpec((1, tm, tk), lambda b, i, k: (b, i, k)),
                  pl.BlockSpec((tk, d), lambda b, i, k: (k, 0)),
                  pl.BlockSpec((1, tm, d), lambda b, i, k: (b, i, 0)),
                  pl.BlockSpec((1, 1, d), lambda b, i, k: (b, 0, 0)),
                  pl.BlockSpec((1, d), lambda b, i, k: (0, 0))],
        out_specs=pl.BlockSpec((1, tm, d), lambda b, i, k: (b, i, 0)),
        out_shape=jax.ShapeDtypeStruct((bsz, l, d), F32),
        compiler_params=_cparams(("parallel", "parallel", "arbitrary")),
        name="down",
    )(g, w, x, gt, g_final.reshape(1, d))


def kernel(x, c, positions, w_ada, b_ada, g_mix, g_ffn, w_in, conv_w, conv_b, f_w1, f_b1, f_w2, f_b2, f_w3, f_b3, f_w4, f_freq, hyena_bias, lambda_q1, lambda_k1, lambda_q2, lambda_k2, subln_g, w_out, w_gate, w_up, w_down, g_final):
    bsz, l, d = x.shape
    depth = w_ada.shape[0]
    ch = hyena_bias.shape[-1]
    hd = lambda_q1.shape[-1]
    qk = (w_in.shape[-1] - 3 * ch - (d - ch)) // 2
    cos, sin = _rope_tables(positions, hd)
    q_scale = hd ** -0.5 * math.log2(math.e)
    for i in range(depth):
        lam_init = 0.8 - 0.6 * math.exp(-0.3 * i)
        mod, kern_t = _ada_and_taps(c, w_ada[i], b_ada[i], l, ch, f_w1[i], f_b1[i], f_w2[i], f_b2[i],
                                    f_w3[i], f_b3[i], f_w4[i], f_freq[i])
        sh1, sc1, gt1, sh2, sc2, gt2 = [mod[:, None, j * d:(j + 1) * d] for j in range(N_MOD)]
        h = _norm_mod(x, g_mix[i], sc1, sh1, NORM_EPS)
        w = w_in[i].astype(BF16)
        o1, o3 = 3 * ch, 3 * ch + 2 * qk
        u_t = _proj_t(h, w, 0, o1, kern_t.shape[-1], F32)
        qk_scale = jnp.concatenate([jnp.full((1, qk), q_scale, F32), jnp.ones((1, qk), F32)], axis=1)
        q_k = _proj(h, w, o1, 2 * qk, rope=(cos, sin), col_scale=qk_scale)
        v = _proj(h, w, o3, w.shape[1] - o3)
        y_hy = _hyena(u_t, kern_t, conv_w[i], conv_b[i], hyena_bias[i], BF16)
        hpad = -w_gate.shape[-1] % 1024
        y_da, (wo, wg, wu) = _diff_attention(
            q_k, v, lambda_q1[i], lambda_k1[i], lambda_q2[i], lambda_k2[i], subln_g[i], lam_init,
            cast=((w_out[i], 0), (w_gate[i], hpad), (w_up[i], hpad)))
        x = _outproj(y_hy, y_da, wo, x, gt1)
        h = _norm_mod(x, g_ffn[i], sc2, sh2, NORM_EPS)
        g, wd = _gateup(h, wg, wu, w_down[i])
        last = i == depth - 1
        x = _down(g, wd, x, gt2, g_final, NORM_EPS if last else None)
    return x
```

```python
import functools
import math

import numpy as np
import jax
import jax.numpy as jnp
from jax import lax
from jax.experimental import pallas as pl
from jax.experimental.pallas import tpu as pltpu

F32 = jnp.float32
BF16 = jnp.bfloat16

NORM_EPS = 1e-6
SUBLN_EPS = 1e-5
ROPE_THETA = 10000.0
FAST_DECAY_PCT = 0.3
SLOW_DECAY_PCT = 1.5
DECAY_TARGET = 1e-2
N_MOD = 6

V7X_VMEM_LIMIT_BYTES = 56 * 1024 * 1024
LANES = 128
EPILOGUE_ROWS = 16
ADA_TILE_BYTES = 13 * 512 * 1024


def _cparams(sem):
    return pltpu.CompilerParams(dimension_semantics=sem, vmem_limit_bytes=V7X_VMEM_LIMIT_BYTES)


def _tile(n, pref):
    t = min(n, pref)
    while n % t:
        t -= 1
    return t


def _weight_cols(w):
    return 1024 if w.dtype == BF16 else 512


def _silu(x):
    return x * (1.0 / (1.0 + jnp.exp(-x)))


def _split_bf16(a):
    hi = a.astype(BF16)
    lo = (a - hi.astype(F32)).astype(BF16)
    return hi, lo


def _dot3(a, b):
    ah, al = _split_bf16(a)
    bh, bl = _split_bf16(b)
    d = functools.partial(jnp.dot, preferred_element_type=F32)
    return d(ah, bh) + (d(ah, bl) + d(al, bh))


def _ada_kernel(c_ref, w_ref, b_ref, o_ref):
    o_ref[...] = jnp.dot(_silu(c_ref[...]).astype(BF16), w_ref[...].astype(BF16),
                         preferred_element_type=F32) + b_ref[...]


def _norm_mod_kernel(x_ref, g_ref, sc_ref, sh_ref, o_ref, *, eps):
    x = x_ref[0]
    y = x * lax.rsqrt(jnp.mean(x * x, axis=-1, keepdims=True) + eps) * g_ref[...]
    o_ref[0] = (y * (1.0 + sc_ref[0]) + sh_ref[0]).astype(o_ref.dtype)


def _norm_mod(x, g, sc, sh, eps):
    bsz, l, d = x.shape
    tr = _tile(l, 512)
    return pl.pallas_call(
        functools.partial(_norm_mod_kernel, eps=eps),
        grid=(bsz, l // tr),
        in_specs=[pl.BlockSpec((1, tr, d), lambda b, i: (b, i, 0)),
                  pl.BlockSpec((1, d), lambda b, i: (0, 0)),
                  pl.BlockSpec((1, 1, d), lambda b, i: (b, 0, 0)),
                  pl.BlockSpec((1, 1, d), lambda b, i: (b, 0, 0))],
        out_specs=pl.BlockSpec((1, tr, d), lambda b, i: (b, i, 0)),
        out_shape=jax.ShapeDtypeStruct((bsz, l, d), BF16),
        compiler_params=_cparams(("parallel", "parallel")),
        name="norm_mod",
    )(x, g.reshape(1, d), sc, sh)


def _rope_kernel(pos_ref, inv_ref, sgn_ref, cos_ref, sin_ref):
    ang = pos_ref[0].astype(F32) * inv_ref[...]
    cos_ref[0] = jnp.cos(ang)
    sin_ref[0] = jnp.sin(ang) * sgn_ref[...]


def _rope_tables(positions, dim):
    bsz, l = positions.shape
    half = dim // 2
    inv = 1.0 / (ROPE_THETA ** (jnp.arange(0, dim, 2, dtype=F32) / dim))
    inv = jnp.concatenate([inv, inv]).reshape(1, dim)
    sgn = jnp.asarray(np.concatenate([-np.ones(half), np.ones(half)]).reshape(1, dim), F32)
    tr = _tile(l, 1024)
    shp = jax.ShapeDtypeStruct((bsz, l, dim), F32)
    return pl.pallas_call(
        _rope_kernel,
        grid=(bsz, l // tr),
        in_specs=[pl.BlockSpec((1, tr, 1), lambda b, i: (b, i, 0)),
                  pl.BlockSpec((1, dim), lambda b, i: (0, 0)),
                  pl.BlockSpec((1, dim), lambda b, i: (0, 0))],
        out_specs=[pl.BlockSpec((1, tr, dim), lambda b, i: (b, i, 0))] * 2,
        out_shape=[shp, shp],
        compiler_params=_cparams(("parallel", "parallel")),
        name="rope_tables",
    )(positions.reshape(bsz, l, 1), inv, sgn)


def _proj_t_kernel(w_ref, h_ref, o_ref):
    res = lax.dot_general(w_ref[...].astype(BF16), h_ref[0], (((0,), (1,)), ((), ())),
                          preferred_element_type=F32)
    o_ref[0] = res.reshape(o_ref.shape[1:]).astype(o_ref.dtype)


def _proj_t(h, w, col0, n, r, out_dtype):
    bsz, l, d = h.shape
    tm, tn = _tile(l, 1024), _tile(math.gcd(n, col0) if col0 else n, _weight_cols(w))
    assert tm % (8 * r) == 0
    j0 = col0 // tn
    return pl.pallas_call(
        _proj_t_kernel,
        grid=(bsz, l // tm, n // tn),
        in_specs=[pl.BlockSpec((d, tn), lambda b, i, j: (0, j + j0)),
                  pl.BlockSpec((1, tm, d), lambda b, i, j: (b, i, 0))],
        out_specs=pl.BlockSpec((1, tn, tm // r, r), lambda b, i, j: (b, j, i, 0)),
        out_shape=jax.ShapeDtypeStruct((bsz, n, l // r, r), out_dtype),
        compiler_params=_cparams(("parallel", "parallel", "arbitrary")),
        name="proj_t",
    )(w, h)


def _proj_kernel(h_ref, w_ref, *rest, rope_dim):
    acc = jnp.dot(h_ref[0], w_ref[...].astype(BF16), preferred_element_type=F32)
    if rope_dim:
        cos_ref, sin_ref, scale_ref, o_ref = rest
        cos, sin = cos_ref[0], sin_ref[0]
        for g in range(acc.shape[1] // rope_dim):
            cols = slice(g * rope_dim, (g + 1) * rope_dim)
            xg = acc[:, cols]
            yg = xg * cos + pltpu.roll(xg, rope_dim // 2, 1) * sin
            o_ref[0, :, cols] = (yg * scale_ref[:, cols]).astype(o_ref.dtype)
    else:
        (o_ref,) = rest
        o_ref[0] = acc.astype(o_ref.dtype)


def _proj(h, w, col0, n, rope=None, col_scale=None):
    bsz, l, d = h.shape
    tm, tn = _tile(l, 1024), _tile(math.gcd(n, col0) if col0 else n, _weight_cols(w))
    j0 = col0 // tn
    in_specs = [pl.BlockSpec((1, tm, d), lambda b, i, j: (b, i, 0)),
                pl.BlockSpec((d, tn), lambda b, i, j: (0, j + j0))]
    args = [h, w]
    rope_dim = 0
    if rope is not None:
        rope_dim = rope[0].shape[-1]
        in_specs += [pl.BlockSpec((1, tm, rope_dim), lambda b, i, j: (b, i, 0))] * 2
        in_specs += [pl.BlockSpec((1, tn), lambda b, i, j: (0, j))]
        args += [*rope, col_scale]
    return pl.pallas_call(
        functools.partial(_proj_kernel, rope_dim=rope_dim),
        grid=(bsz, l // tm, n // tn),
        in_specs=in_specs,
        out_specs=pl.BlockSpec((1, tm, tn), lambda b, i, j: (b, i, j)),
        out_shape=jax.ShapeDtypeStruct((bsz, l, n), BF16),
        compiler_params=_cparams(("parallel", "parallel", "arbitrary")),
        name="proj_rope" if rope_dim else "proj",
    )(*args)


def _filt_kernel(z_ref, t_ref, w1_ref, b1_ref, w2_ref, b2_ref, w3_ref, b3_ref, fr_ref,
                 w4_ref, ad_ref, o_ref):
    fr = fr_ref[...]
    h = jnp.sin(fr * (_dot3(w1_ref[...], z_ref[...]) + b1_ref[...]))
    h = jnp.sin(fr * (_dot3(w2_ref[...], h) + b2_ref[...]))
    h = jnp.sin(fr * (_dot3(w3_ref[...], h) + b3_ref[...]))
    t = t_ref[...]
    decay = jnp.exp(-(ad_ref[...] * t[0:1, :]))
    o_ref[...] = (_dot3(w4_ref[0], h) * decay * t[1:2, :]).reshape(o_ref.shape)


def _ada_filt_kernel(c_ref, wa_ref, ba_ref, *rest, stride):
    mod_ref, kern_ref = rest[-2:]
    _ada_kernel(c_ref, wa_ref, ba_ref, mod_ref)

    @pl.when(pl.program_id(0) % stride == 0)
    def _():
        _filt_kernel(*rest[:-2], kern_ref)


def _ada_and_taps(cvec, w_ada, b_ada, l, c, f_w1, f_b1, f_w2, f_b2, f_w3, f_b3, f_w4, f_freq):
    n = 2 * l
    emb, hid = f_w1.shape
    bands = (emb - 1) // 2
    t = jnp.linspace(0.0, 1.0, l, dtype=F32)[:, None]
    w = 2.0 * math.pi * jnp.arange(l, dtype=F32)[:, None] / l
    f = jnp.linspace(1e-4, bands - 1, bands, dtype=F32)[None, :]
    z = jnp.concatenate([t, jnp.cos(f * w), -jnp.sin(f * w)], axis=-1)
    cols = lambda a: jnp.concatenate([a, a[:1], a[:0:-1]], axis=0).T
    embp = -(-emb // 8) * 8
    z = jnp.pad(cols(z), ((0, embp - emb), (0, 0)))
    mask = jnp.ones((1, n), F32).at[0, l].set(0.0)
    tm = jnp.concatenate([cols(t), mask], axis=0)
    min_decay = math.log(DECAY_TARGET) / SLOW_DECAY_PCT
    max_decay = math.log(DECAY_TARGET) / FAST_DECAY_PCT
    ad = jnp.abs(jnp.linspace(min_decay, max_decay, c, dtype=F32)).reshape(c, 1)
    w1t = jnp.pad(f_w1.T, ((0, 0), (0, embp - emb)))
    w4t = f_w4.T.reshape(2, c, hid)
    col = lambda a: a.reshape(hid, 1)
    bsz, d = cvec.shape
    n_mod = w_ada.shape[1]
    r = int(round(math.sqrt(n)))
    assert r * r == n
    common = math.gcd(n_mod // LANES, n // LANES)
    fits = lambda s: common % s == 0 and s % 2 == 0 and (n_mod // s) * d * 4 <= ADA_TILE_BYTES
    nf = n // (8 * r)
    nt = next((s for s in range(nf, common + 1, nf) if fits(s)), 0) if nf and nf % 2 == 0 else 0
    folded = nt > 0
    if not folded:
        nt = next((s for s in range(2, common + 1, 2) if fits(s)), common)
        nf = nt
    stride = nt // nf
    tn, tn_mod = n // nf, n_mod // nt
    rows = 8
    cp = jnp.zeros((rows, d), F32).at[:bsz].set(cvec)
    full = lambda shape: pl.BlockSpec(shape, lambda i: (0,) * len(shape))
    if folded:
        kern_spec = pl.BlockSpec((c, 8, r), lambda i: (0, i // stride, 0))
        kern_shape = jax.ShapeDtypeStruct((c, r, r), F32)
    else:
        kern_spec = pl.BlockSpec((c, tn), lambda i: (0, i // stride))
        kern_shape = jax.ShapeDtypeStruct((c, n), F32)
    mod, kern = pl.pallas_call(
        functools.partial(_ada_filt_kernel, stride=stride),
        grid=(nt,),
        in_specs=[full((rows, d)),
                  pl.BlockSpec((d, tn_mod), lambda i: (0, i)),
                  pl.BlockSpec((1, tn_mod), lambda i: (0, i)),
                  pl.BlockSpec((embp, tn), lambda i: (0, i // stride)),
                  pl.BlockSpec((2, tn), lambda i: (0, i // stride)),
                  full((hid, embp)), full((hid, 1)), full((hid, hid)), full((hid, 1)),
                  full((hid, hid)), full((hid, 1)), full((hid, 1)),
                  pl.BlockSpec((1, c, hid), lambda i: ((2 * (i // stride)) // nf, 0, 0)),
                  full((c, 1))],
        out_specs=[pl.BlockSpec((rows, tn_mod), lambda i: (0, i)), kern_spec],
        out_shape=[jax.ShapeDtypeStruct((rows, n_mod), F32), kern_shape],
        compiler_params=_cparams(("arbitrary",)),
        name="ada_filter",
    )(cp, w_ada, b_ada.reshape(1, n_mod), z, tm, w1t, col(f_b1), f_w2.T, col(f_b2), f_w3.T,
      col(f_b3), col(f_freq), w4t, ad)
    return mod[:bsz], kern.reshape(c, r, r)


def _dft_consts(r, tc):
    half = r // 2
    idx = np.arange(r)
    ang = -2.0 * np.pi * np.outer(idx, idx) / r
    fr, fi = np.cos(ang), np.sin(ang)
    angt = -2.0 * np.pi * np.outer(idx, idx) / (r * r)
    fa = np.block([[fr[:, :half], -fi[:, :half]], [fi[:, :half], fr[:, :half]]])
    fk = np.concatenate([fr, fi], axis=0)
    g = np.concatenate([fr, fi], axis=1)
    fin = np.block([[fr[:half], fi[:half]], [-fi[:half], fr[:half]]]) / float(r * r)
    mx = lambda a: jnp.asarray(a, F32).astype(BF16)
    bc = lambda a: jnp.broadcast_to(mx(a)[None], (tc,) + a.shape)
    return (bc(fa), bc(fk), mx(g), jnp.asarray(np.cos(angt), F32),
            jnp.asarray(np.sin(angt), F32), bc(fin))


def _shift_prev(u, lane, row):
    nr, nl = u.shape[-2], u.shape[-1]
    a = pltpu.roll(u, 1, u.ndim - 1)
    b = pltpu.roll(a, 1, u.ndim - 2)
    p = jnp.where(lane == 0, b, a)
    return jnp.where((lane == 0) & (row == 0), 0.0, p)


def _shift_next(u, lane, row):
    nr, nl = u.shape[-2], u.shape[-1]
    a = pltpu.roll(u, nl - 1, u.ndim - 1)
    b = pltpu.roll(a, nr - 1, u.ndim - 2)
    p = jnp.where(lane == nl - 1, b, a)
    return jnp.where((lane == nl - 1) & (row == nr - 1), 0.0, p)


def _lane_stage(y2, g_ref):
    tc, r2, r = y2.shape
    p = jnp.dot(y2.reshape(tc * r2, r).astype(BF16), g_ref[...], preferred_element_type=F32)
    p = p.reshape(tc, r2, r2)
    return p[:, :r, :r], p[:, :r, r:], p[:, r:, :r], p[:, r:, r:]


def _fwd_fft(x2, f1_ref, g_ref, tr, ti):
    r = tr.shape[0]
    a = jnp.einsum("cmk,ckr->cmr", f1_ref[...], x2.astype(BF16), preferred_element_type=F32)
    ar, ai = a[:, :r], a[:, r:]
    br = ar * tr - ai * ti
    bi = ar * ti + ai * tr
    p00, p01, p10, p11 = _lane_stage(jnp.concatenate([br, bi], axis=1), g_ref)
    return p00 - p11, p01 + p10


def _hyena_kernel(x0_ref, x1_ref, v_ref, w0_ref, w1_ref, wv_ref, b0_ref, b1_ref, bv_ref,
                  hb_ref, k_ref, fa_ref, fk_ref, g_ref, tr_ref, ti_ref, fin_ref, o_ref):
    shape = x0_ref.shape[1:]
    lane = lax.broadcasted_iota(jnp.int32, shape, 2)
    row = lax.broadcasted_iota(jnp.int32, shape, 1)
    tr, ti = tr_ref[...], ti_ref[...]

    def sconv(u_ref, w_ref, b_ref, b):
        u = u_ref[b].astype(F32)
        return (b_ref[...] + w_ref[0] * _shift_prev(u, lane, row) + w_ref[1] * u
                + w_ref[2] * _shift_next(u, lane, row))

    nb = x0_ref.shape[0]
    x0 = [sconv(x0_ref, w0_ref, b0_ref, b) for b in range(nb)]
    vx = [sconv(v_ref, wv_ref, bv_ref, b) * sconv(x1_ref, w1_ref, b1_ref, b) for b in range(nb)]

    kr, ki = _fwd_fft(k_ref[...], fk_ref, g_ref, tr, ti)
    for b0 in range(0, nb, 2):
        pair = vx[b0:b0 + 2]
        xi = pair[1] if len(pair) == 2 else jnp.zeros_like(pair[0])
        sr, si = _fwd_fft(jnp.concatenate([pair[0], xi], axis=1), fa_ref, g_ref, tr, ti)
        yr = sr * kr - si * ki
        yi = sr * ki + si * kr
        p00, p01, p10, p11 = _lane_stage(jnp.concatenate([yr, yi], axis=1), g_ref)
        cr, ci = p00 + p11, p10 - p01
        dr = cr * tr + ci * ti
        di = ci * tr - cr * ti
        d2 = jnp.concatenate([dr, di], axis=1).astype(BF16)
        y = jnp.einsum("cmk,ckr->cmr", fin_ref[...], d2, preferred_element_type=F32)
        half = shape[1]
        for j, yb in enumerate((y[:, :half], y[:, half:])[:len(pair)]):
            b = b0 + j
            res = (yb + vx[b] * hb_ref[...]) * x0[b]
            o_ref[b] = res.reshape(res.shape[0], res.shape[1] * res.shape[2]).astype(o_ref.dtype)


def _hyena(u4, k3, conv_w, conv_b, hyena_bias, out_dtype):
    bsz, c3, half, r = u4.shape
    c = c3 // 3
    l = half * r
    assert r == 2 * half and k3.shape == (c, r, r)
    tc = _tile(c, 16)
    nc = c // tc
    consts = _dft_consts(r, tc)
    cw = conv_w.reshape(conv_w.shape[0], c3, 1, 1)
    cb = conv_b.reshape(c3, 1, 1)
    hb = hyena_bias.reshape(c, 1, 1)
    stream = lambda s: pl.BlockSpec((bsz, tc, half, r), lambda j, s=s: (0, j + s * nc, 0, 0))
    wspec = lambda s: pl.BlockSpec((conv_w.shape[0], tc, 1, 1), lambda j, s=s: (0, j + s * nc, 0, 0))
    bspec = lambda s: pl.BlockSpec((tc, 1, 1), lambda j, s=s: (j + s * nc, 0, 0))
    cspec = lambda a: pl.BlockSpec(a.shape, lambda j, nd=a.ndim: (0,) * nd)
    return pl.pallas_call(
        _hyena_kernel,
        grid=(nc,),
        in_specs=[stream(0), stream(1), stream(2), wspec(0), wspec(1), wspec(2),
                  bspec(0), bspec(1), bspec(2), bspec(0),
                  pl.BlockSpec((tc, r, r), lambda j: (j, 0, 0))] + [cspec(a) for a in consts],
        out_specs=pl.BlockSpec((bsz, tc, l), lambda j: (0, j, 0)),
        out_shape=jax.ShapeDtypeStruct((bsz, c, l), out_dtype),
        compiler_params=_cparams(("parallel",)),
        name="hyena",
    )(u4, u4, u4, cw, cw, cw, cb, cb, cb, hb, k3, *consts)


def _rep(x, n):
    if n % LANES:
        return x[:, :n]
    return x if n == LANES else jnp.concatenate([x] * (n // LANES), axis=1)


def _attn_kernel(q_ref, k_ref, v_ref, lq1_ref, lk1_ref, lq2_ref, lk2_ref, g_ref, *rest,
                 ncast, hd, tk, lam_init, eps):
    cast_in, o_ref, cast_out = rest[:ncast], rest[ncast], rest[ncast + 1:2 * ncast + 1]
    s_ref, mb_ref, m_ref, l_ref, acc_ref = rest[2 * ncast + 1:]
    for src, dst in zip(cast_in, cast_out):
        c = src.shape[1]
        dst[:, :c] = src[...].astype(dst.dtype)
        if dst.shape[1] > c:
            dst[:, c:] = jnp.zeros((dst.shape[0], dst.shape[1] - c), dst.dtype)
    nkv = k_ref.shape[1] // tk
    tq = q_ref.shape[1]
    hw = v_ref.shape[2]

    def scores(t, slot, j):
        qj = q_ref[0, :, j * hd:(j + 1) * hd]
        kj = k_ref[0, t * tk:(t + 1) * tk, j * hd:(j + 1) * hd]
        s = lax.dot_general(qj, kj, (((1,), (1,)), ((), ())), preferred_element_type=F32)
        s_ref[slot, j] = s
        mb_ref[slot, j] = jnp.broadcast_to(jnp.max(s, axis=-1, keepdims=True), (tq, LANES))

    def consume(t, slot, j):
        v = v_ref[0, t * tk:(t + 1) * tk, :]
        m_prev = m_ref[j]
        m_new = jnp.maximum(m_prev, mb_ref[slot, j])
        alpha = jnp.exp2(m_prev - m_new)
        ps = [jnp.exp2(s_ref[slot, j, :, c * LANES:(c + 1) * LANES] - m_new)
              for c in range(tk // LANES)]
        l_ref[j] = alpha * l_ref[j] + functools.reduce(lambda a, b: a + b, ps)
        p = jnp.concatenate(ps, axis=1).astype(v.dtype)
        acc_ref[j] = _rep(alpha, hw) * acc_ref[j] + jnp.dot(p, v, preferred_element_type=F32)
        m_ref[j] = m_new

    m_ref[...] = jnp.full(m_ref.shape, -jnp.inf, F32)
    l_ref[...] = jnp.zeros(l_ref.shape, F32)
    acc_ref[...] = jnp.zeros(acc_ref.shape, F32)
    scores(0, 0, 0)
    scores(0, 0, 1)
    for t in range(nkv):
        for j in range(2):
            if t + 1 < nkv:
                scores(t + 1, (t + 1) % 2, j)
            consume(t, t % 2, j)

    lam = (jnp.exp(jnp.sum(lq1_ref[...] * lk1_ref[...], axis=-1, keepdims=True))
           - jnp.exp(jnp.sum(lq2_ref[...] * lk2_ref[...], axis=-1, keepdims=True)) + lam_init)
    l0 = jnp.sum(l_ref[0], axis=-1, keepdims=True)
    l1 = jnp.sum(l_ref[1], axis=-1, keepdims=True)
    o = acc_ref[0] / l0 - lam * (acc_ref[1] / l1)
    o = o * lax.rsqrt(jnp.mean(o * o, axis=-1, keepdims=True) + eps) * g_ref[...]
    o_ref[0] = (o * (1.0 - lam_init)).astype(o_ref.dtype)


def _cast_blocks(shape, steps, full_rows):
    r, c = shape
    for a in range(steps, 0, -1):
        b = steps // a
        if steps % a or r % a or c % b or (full_rows and b > 1):
            continue
        rb, cb = r // a, c // b
        if rb % 16 == 0 and cb % LANES == 0:
            return rb, cb, b
    return None


def _diff_attention(qk, v, lq1, lk1, lq2, lk2, subln_g, lam_init, cast=()):
    bsz, l, width = v.shape
    hd = lq1.shape[-1]
    hw = 2 * hd
    heads = width // hw
    tq, tk = _tile(l, 512), _tile(l, 1024)
    nq = l // tq
    steps = bsz * heads * nq
    vec = lambda a: a.reshape(1, -1).astype(F32)
    vspec = lambda n: pl.BlockSpec((1, n), lambda b, h, i: (0, 0))
    plans = [_cast_blocks(a.shape, steps, pad > 0) for a, pad in cast]
    riders = [(a, pad, p) for (a, pad), p in zip(cast, plans) if p is not None]

    def cast_spec(p, pad):
        rb, cb, ncb = p
        return pl.BlockSpec((rb, cb + pad), lambda b, h, i: (((b * heads + h) * nq + i) // ncb,
                                                              ((b * heads + h) * nq + i) % ncb))

    cast_in_specs = [cast_spec(p, 0) for _, _, p in riders]
    cast_out_specs = [cast_spec(p, pad) for _, pad, p in riders]
    outs = pl.pallas_call(
        functools.partial(_attn_kernel, ncast=len(riders), hd=hd, tk=tk, lam_init=lam_init,
                          eps=SUBLN_EPS),
        grid=(bsz, heads, nq),
        in_specs=[pl.BlockSpec((1, tq, hw), lambda b, h, i: (b, i, h)),
                  pl.BlockSpec((1, l, hw), lambda b, h, i: (b, 0, heads + h)),
                  pl.BlockSpec((1, l, hw), lambda b, h, i: (b, 0, h)),
                  vspec(hd), vspec(hd), vspec(hd), vspec(hd), vspec(hw)] + cast_in_specs,
        out_specs=[pl.BlockSpec((1, tq, hw), lambda b, h, i: (b, i, h))] + cast_out_specs,
        out_shape=[jax.ShapeDtypeStruct((bsz, l, width), BF16)]
                  + [jax.ShapeDtypeStruct((a.shape[0], a.shape[1] + pad), BF16)
                     for a, pad, _ in riders],
        scratch_shapes=[pltpu.VMEM((2, 2, tq, tk), F32), pltpu.VMEM((2, 2, tq, LANES), F32),
                        pltpu.VMEM((2, tq, LANES), F32), pltpu.VMEM((2, tq, LANES), F32),
                        pltpu.VMEM((2, tq, hw), F32)],
        compiler_params=_cparams(("parallel", "parallel", "arbitrary")),
        name="diff_attn",
    )(qk, qk, v, vec(lq1), vec(lk1), vec(lq2), vec(lk2), vec(subln_g), *[a for a, _, _ in riders])
    rounded = iter(outs[1:])
    return outs[0], [next(rounded) if p is not None else jnp.pad(a.astype(BF16), ((0, 0), (0, pad)))
                     for (a, pad), p in zip(cast, plans)]


def _outproj_kernel(yh_ref, yd_ref, w1_ref, w2_ref, x_ref, gt_ref, o_ref):
    acc = lax.dot_general(yh_ref[0], w1_ref[...].astype(BF16), (((0,), (0,)), ((), ())),
                          preferred_element_type=F32)
    acc = acc + jnp.dot(yd_ref[0], w2_ref[...].astype(BF16), preferred_element_type=F32)
    o_ref[0] = x_ref[0] + gt_ref[0] * acc


def _outproj(yh_t, yd, w, x, gt):
    bsz, l, d = x.shape
    c, kd = yh_t.shape[1], yd.shape[2]
    assert c == kd and w.shape[0] == c + kd
    tm, tn = _tile(l, 1024), _tile(d, _weight_cols(w))
    return pl.pallas_call(
        _outproj_kernel,
        grid=(bsz, l // tm, d // tn),
        in_specs=[pl.BlockSpec((1, c, tm), lambda b, i, j: (b, 0, i)),
                  pl.BlockSpec((1, tm, kd), lambda b, i, j: (b, i, 0)),
                  pl.BlockSpec((c, tn), lambda b, i, j: (0, j)),
                  pl.BlockSpec((kd, tn), lambda b, i, j: (1, j)),
                  pl.BlockSpec((1, tm, tn), lambda b, i, j: (b, i, j)),
                  pl.BlockSpec((1, 1, tn), lambda b, i, j: (b, 0, j))],
        out_specs=pl.BlockSpec((1, tm, tn), lambda b, i, j: (b, i, j)),
        out_shape=jax.ShapeDtypeStruct((bsz, l, d), F32),
        compiler_params=_cparams(("parallel", "parallel", "arbitrary")),
        name="outproj",
    )(yh_t, yd, w, w, x, gt)


def _gateup_kernel(h_ref, wg_ref, wu_ref, *rest, src_blocks):
    if src_blocks:
        src_ref, o_ref, dst_ref = rest
        step = (pl.program_id(0) * pl.num_programs(1) + pl.program_id(1)) * pl.num_programs(2) \
            + pl.program_id(2)

        @pl.when(step < src_blocks)
        def _():
            dst_ref[...] = src_ref[...].astype(dst_ref.dtype)

        @pl.when(step >= src_blocks)
        def _():
            dst_ref[...] = jnp.zeros(dst_ref.shape, dst_ref.dtype)
    else:
        (o_ref,) = rest
    h = h_ref[0]
    g = jnp.dot(h, wg_ref[...], preferred_element_type=F32)
    u = jnp.dot(h, wu_ref[...], preferred_element_type=F32)
    o_ref[0] = (_silu(g) * u).astype(o_ref.dtype)


def _gateup(h, wg, wu, wd):
    bsz, l, d = h.shape
    n = wg.shape[1]
    tm, tn = _tile(l, 1024), _tile(n, 512)
    grid = (bsz, l // tm, n // tn)
    steps = grid[0] * grid[1] * grid[2]
    wspec = pl.BlockSpec((d, tn), lambda b, i, j: (0, j))
    in_specs = [pl.BlockSpec((1, tm, d), lambda b, i, j: (b, i, 0)), wspec, wspec]
    out_specs = [pl.BlockSpec((1, tm, tn), lambda b, i, j: (b, i, j))]
    out_shape = [jax.ShapeDtypeStruct((bsz, l, n), BF16)]
    args = [h, wg, wu]
    rb = n // steps
    rides = n % steps == 0 and rb % 16 == 0 and wd.shape[0] % rb == 0
    src_blocks = wd.shape[0] // rb if rides else 0
    if rides:
        lin = lambda b, i, j: (b * grid[1] + i) * grid[2] + j
        in_specs.append(pl.BlockSpec((rb, wd.shape[1]),
                                     lambda b, i, j: (jnp.minimum(lin(b, i, j), src_blocks - 1), 0)))
        out_specs.append(pl.BlockSpec((rb, wd.shape[1]), lambda b, i, j: (lin(b, i, j), 0)))
        out_shape.append(jax.ShapeDtypeStruct((n, wd.shape[1]), BF16))
        args.append(wd)
    outs = pl.pallas_call(
        functools.partial(_gateup_kernel, src_blocks=src_blocks),
        grid=grid,
        in_specs=in_specs,
        out_specs=out_specs,
        out_shape=out_shape,
        compiler_params=_cparams(("parallel", "parallel", "arbitrary")),
        name="gateup",
    )(*args)
    if rides:
        return outs[0], outs[1]
    return outs[0], jnp.pad(wd.astype(BF16), ((0, n - wd.shape[0]), (0, 0)))


def _down_kernel(g_ref, w_ref, x_ref, gt_ref, gf_ref, o_ref, *, final_eps):
    k = pl.program_id(2)
    d = o_ref.shape[2]
    tn = _tile(d, 1024)
    chunks = [slice(c * tn, (c + 1) * tn) for c in range(d // tn)]
    part = lambda cols: jnp.dot(g_ref[0], w_ref[:, cols], preferred_element_type=F32)

    @pl.when(k == 0)
    def _():
        for cols in chunks:
            o_ref[0, :, cols] = part(cols)

    @pl.when(k > 0)
    def _():
        for cols in chunks:
            o_ref[0, :, cols] += part(cols)

    @pl.when(k == pl.num_programs(2) - 1)
    def _():
        def finish(r, carry):
            rows = pl.ds(pl.multiple_of(r * EPILOGUE_ROWS, EPILOGUE_ROWS), EPILOGUE_ROWS)
            x2 = x_ref[0, rows, :] + gt_ref[0] * o_ref[0, rows, :]
            if final_eps is not None:
                ms = jnp.mean(x2 * x2, axis=-1, keepdims=True)
                x2 = x2 * lax.rsqrt(ms + final_eps) * gf_ref[...]
            o_ref[0, rows, :] = x2
            return carry

        lax.fori_loop(0, o_ref.shape[1] // EPILOGUE_ROWS, finish, 0)


def _down(g, w, x, gt, g_final, final_eps):
    bsz, l, d = x.shape
    kdim = g.shape[2]
    tm, tk = _tile(l, 512), _tile(kdim, 1024)
    return pl.pallas_call(
        functools.partial(_down_kernel, final_eps=final_eps),
        grid=(bsz, l // tm, kdim // tk),
        in_specs=[pl.BlockSpec((1, tm, tk), lambda b, i, k: (b, i, k)),
                  pl.BlockSpec((tk, d), lambda b, i, k: (k, 0)),
                  pl.BlockSpec((1, tm, d), lambda b, i, k: (b, i, 0)),
                  pl.BlockSpec((1, 1, d), lambda b, i, k: (b, 0, 0)),
                  pl.BlockSpec((1, d), lambda b, i, k: (0, 0))],
        out_specs=pl.BlockSpec((1, tm, d), lambda b, i, k: (b, i, 0)),
        out_shape=jax.ShapeDtypeStruct((bsz, l, d), F32),
        compiler_params=_cparams(("parallel", "parallel", "arbitrary")),
        name="down",
    )(g, w, x, gt, g_final.reshape(1, d))


def kernel(x, c, positions, w_ada, b_ada, g_mix, g_ffn, w_in, conv_w, conv_b, f_w1, f_b1, f_w2, f_b2, f_w3, f_b3, f_w4, f_freq, hyena_bias, lambda_q1, lambda_k1, lambda_q2, lambda_k2, subln_g, w_out, w_gate, w_up, w_down, g_final):
    bsz, l, d = x.shape
    depth = w_ada.shape[0]
    ch = hyena_bias.shape[-1]
    hd = lambda_q1.shape[-1]
    qk = (w_in.shape[-1] - 3 * ch - (d - ch)) // 2
    cos, sin = _rope_tables(positions, hd)
    q_scale = hd ** -0.5 * math.log2(math.e)
    for i in range(depth):
        lam_init = 0.8 - 0.6 * math.exp(-0.3 * i)
        mod, kern_t = _ada_and_taps(c, w_ada[i], b_ada[i], l, ch, f_w1[i], f_b1[i], f_w2[i], f_b2[i],
                                    f_w3[i], f_b3[i], f_w4[i], f_freq[i])
        sh1, sc1, gt1, sh2, sc2, gt2 = [mod[:, None, j * d:(j + 1) * d] for j in range(N_MOD)]
        h = _norm_mod(x, g_mix[i], sc1, sh1, NORM_EPS)
        w = w_in[i].astype(BF16)
        o1, o3 = 3 * ch, 3 * ch + 2 * qk
        u_t = _proj_t(h, w, 0, o1, kern_t.shape[-1], F32)
        qk_scale = jnp.concatenate([jnp.full((1, qk), q_scale, F32), jnp.ones((1, qk), F32)], axis=1)
        q_k = _proj(h, w, o1, 2 * qk, rope=(cos, sin), col_scale=qk_scale)
        v = _proj(h, w, o3, w.shape[1] - o3)
        y_hy = _hyena(u_t, kern_t, conv_w[i], conv_b[i], hyena_bias[i], BF16)
        hpad = -w_gate.shape[-1] % 1024
        y_da, (wo, wg, wu) = _diff_attention(
            q_k, v, lambda_q1[i], lambda_k1[i], lambda_q2[i], lambda_k2[i], subln_g[i], lam_init,
            cast=((w_out[i], 0), (w_gate[i], hpad), (w_up[i], hpad)))
        x = _outproj(y_hy, y_da, wo, x, gt1)
        h = _norm_mod(x, g_ffn[i], sc2, sh2, NORM_EPS)
        g, wd = _gateup(h, wg, wu, w_down[i])
        last = i == depth - 1
        x = _down(g, wd, x, gt2, g_final, NORM_EPS if last else None)
    return x
```

```python
import functools
import math

import numpy as np
import jax
import jax.numpy as jnp
from jax import lax
from jax.experimental import pallas as pl
from jax.experimental.pallas import tpu as pltpu

F32 = jnp.float32
BF16 = jnp.bfloat16

NORM_EPS = 1e-6
SUBLN_EPS = 1e-5
ROPE_THETA = 10000.0
FAST_DECAY_PCT = 0.3
SLOW_DECAY_PCT = 1.5
DECAY_TARGET = 1e-2
N_MOD = 6

V7X_VMEM_LIMIT_BYTES = 56 * 1024 * 1024
LANES = 128
EPILOGUE_ROWS = 16
ADA_TILE_BYTES = 13 * 512 * 1024


def _cparams(sem):
    return pltpu.CompilerParams(dimension_semantics=sem, vmem_limit_bytes=V7X_VMEM_LIMIT_BYTES)


def _tile(n, pref):
    t = min(n, pref)
    while n % t:
        t -= 1
    return t


def _weight_cols(w):
    return 1024 if w.dtype == BF16 else 512


def _silu(x):
    return x * (1.0 / (1.0 + jnp.exp(-x)))


def _split_bf16(a):
    hi = a.astype(BF16)
    lo = (a - hi.astype(F32)).astype(BF16)
    return hi, lo


def _dot3(a, b):
    ah, al = _split_bf16(a)
    bh, bl = _split_bf16(b)
    d = functools.partial(jnp.dot, preferred_element_type=F32)
    return d(ah, bh) + (d(ah, bl) + d(al, bh))


def _ada_kernel(c_ref, w_ref, b_ref, o_ref):
    o_ref[...] = _dot3(_silu(c_ref[...]), w_ref[...]) + b_ref[...]


def _norm_mod_kernel(x_ref, g_ref, sc_ref, sh_ref, o_ref, *, eps):
    x = x_ref[0]
    y = x * lax.rsqrt(jnp.mean(x * x, axis=-1, keepdims=True) + eps) * g_ref[...]
    o_ref[0] = (y * (1.0 + sc_ref[0]) + sh_ref[0]).astype(o_ref.dtype)


def _norm_mod(x, g, sc, sh, eps):
    bsz, l, d = x.shape
    tr = _tile(l, 512)
    return pl.pallas_call(
        functools.partial(_norm_mod_kernel, eps=eps),
        grid=(bsz, l // tr),
        in_specs=[pl.BlockSpec((1, tr, d), lambda b, i: (b, i, 0)),
                  pl.BlockSpec((1, d), lambda b, i: (0, 0)),
                  pl.BlockSpec((1, 1, d), lambda b, i: (b, 0, 0)),
                  pl.BlockSpec((1, 1, d), lambda b, i: (b, 0, 0))],
        out_specs=pl.BlockSpec((1, tr, d), lambda b, i: (b, i, 0)),
        out_shape=jax.ShapeDtypeStruct((bsz, l, d), BF16),
        compiler_params=_cparams(("parallel", "parallel")),
        name="norm_mod",
    )(x, g.reshape(1, d), sc, sh)


def _rope_kernel(pos_ref, inv_ref, sgn_ref, cos_ref, sin_ref):
    ang = pos_ref[0].astype(F32) * inv_ref[...]
    cos_ref[0] = jnp.cos(ang)
    sin_ref[0] = jnp.sin(ang) * sgn_ref[...]


def _rope_tables(positions, dim):
    bsz, l = positions.shape
    half = dim // 2
    inv = 1.0 / (ROPE_THETA ** (jnp.arange(0, dim, 2, dtype=F32) / dim))
    inv = jnp.concatenate([inv, inv]).reshape(1, dim)
    sgn = jnp.asarray(np.concatenate([-np.ones(half), np.ones(half)]).reshape(1, dim), F32)
    tr = _tile(l, 1024)
    shp = jax.ShapeDtypeStruct((bsz, l, dim), F32)
    return pl.pallas_call(
        _rope_kernel,
        grid=(bsz, l // tr),
        in_specs=[pl.BlockSpec((1, tr, 1), lambda b, i: (b, i, 0)),
                  pl.BlockSpec((1, dim), lambda b, i: (0, 0)),
                  pl.BlockSpec((1, dim), lambda b, i: (0, 0))],
        out_specs=[pl.BlockSpec((1, tr, dim), lambda b, i: (b, i, 0))] * 2,
        out_shape=[shp, shp],
        compiler_params=_cparams(("parallel", "parallel")),
        name="rope_tables",
    )(positions.reshape(bsz, l, 1), inv, sgn)


def _proj_t_kernel(w_ref, h_ref, o_ref):
    res = lax.dot_general(w_ref[...].astype(BF16), h_ref[0], (((0,), (1,)), ((), ())),
                          preferred_element_type=F32)
    o_ref[0] = res.reshape(o_ref.shape[1:]).astype(o_ref.dtype)


def _proj_t(h, w, col0, n, r, out_dtype):
    bsz, l, d = h.shape
    tm, tn = _tile(l, 1024), _tile(math.gcd(n, col0) if col0 else n, _weight_cols(w))
    assert tm % (8 * r) == 0
    j0 = col0 // tn
    return pl.pallas_call(
        _proj_t_kernel,
        grid=(bsz, l // tm, n // tn),
        in_specs=[pl.BlockSpec((d, tn), lambda b, i, j: (0, j + j0)),
                  pl.BlockSpec((1, tm, d), lambda b, i, j: (b, i, 0))],
        out_specs=pl.BlockSpec((1, tn, tm // r, r), lambda b, i, j: (b, j, i, 0)),
        out_shape=jax.ShapeDtypeStruct((bsz, n, l // r, r), out_dtype),
        compiler_params=_cparams(("parallel", "parallel", "arbitrary")),
        name="proj_t",
    )(w, h)


def _proj_kernel(h_ref, w_ref, *rest, rope_dim):
    acc = jnp.dot(h_ref[0], w_ref[...].astype(BF16), preferred_element_type=F32)
    if rope_dim:
        cos_ref, sin_ref, scale_ref, o_ref = rest
        cos, sin = cos_ref[0], sin_ref[0]
        for g in range(acc.shape[1] // rope_dim):
            cols = slice(g * rope_dim, (g + 1) * rope_dim)
            xg = acc[:, cols]
            yg = xg * cos + pltpu.roll(xg, rope_dim // 2, 1) * sin
            o_ref[0, :, cols] = (yg * scale_ref[:, cols]).astype(o_ref.dtype)
    else:
        (o_ref,) = rest
        o_ref[0] = acc.astype(o_ref.dtype)


def _proj(h, w, col0, n, rope=None, col_scale=None):
    bsz, l, d = h.shape
    tm, tn = _tile(l, 1024), _tile(math.gcd(n, col0) if col0 else n, _weight_cols(w))
    j0 = col0 // tn
    in_specs = [pl.BlockSpec((1, tm, d), lambda b, i, j: (b, i, 0)),
                pl.BlockSpec((d, tn), lambda b, i, j: (0, j + j0))]
    args = [h, w]
    rope_dim = 0
    if rope is not None:
        rope_dim = rope[0].shape[-1]
        in_specs += [pl.BlockSpec((1, tm, rope_dim), lambda b, i, j: (b, i, 0))] * 2
        in_specs += [pl.BlockSpec((1, tn), lambda b, i, j: (0, j))]
        args += [*rope, col_scale]
    return pl.pallas_call(
        functools.partial(_proj_kernel, rope_dim=rope_dim),
        grid=(bsz, l // tm, n // tn),
        in_specs=in_specs,
        out_specs=pl.BlockSpec((1, tm, tn), lambda b, i, j: (b, i, j)),
        out_shape=jax.ShapeDtypeStruct((bsz, l, n), BF16),
        compiler_params=_cparams(("parallel", "parallel", "arbitrary")),
        name="proj_rope" if rope_dim else "proj",
    )(*args)


def _filt_kernel(z_ref, t_ref, w1_ref, b1_ref, w2_ref, b2_ref, w3_ref, b3_ref, fr_ref,
                 w4_ref, ad_ref, o_ref):
    fr = fr_ref[...]
    h = jnp.sin(fr * (_dot3(w1_ref[...], z_ref[...]) + b1_ref[...]))
    h = jnp.sin(fr * (_dot3(w2_ref[...], h) + b2_ref[...]))
    h = jnp.sin(fr * (_dot3(w3_ref[...], h) + b3_ref[...]))
    t = t_ref[...]
    decay = jnp.exp(-(ad_ref[...] * t[0:1, :]))
    o_ref[...] = (_dot3(w4_ref[0], h) * decay * t[1:2, :]).reshape(o_ref.shape)


def _ada_filt_kernel(c_ref, wa_ref, ba_ref, *rest, stride):
    mod_ref, kern_ref = rest[-2:]
    _ada_kernel(c_ref, wa_ref, ba_ref, mod_ref)

    @pl.when(pl.program_id(0) % stride == 0)
    def _():
        _filt_kernel(*rest[:-2], kern_ref)


def _ada_and_taps(cvec, w_ada, b_ada, l, c, f_w1, f_b1, f_w2, f_b2, f_w3, f_b3, f_w4, f_freq):
    n = 2 * l
    emb, hid = f_w1.shape
    bands = (emb - 1) // 2
    t = jnp.linspace(0.0, 1.0, l, dtype=F32)[:, None]
    w = 2.0 * math.pi * jnp.arange(l, dtype=F32)[:, None] / l
    f = jnp.linspace(1e-4, bands - 1, bands, dtype=F32)[None, :]
    z = jnp.concatenate([t, jnp.cos(f * w), -jnp.sin(f * w)], axis=-1)
    cols = lambda a: jnp.concatenate([a, a[:1], a[:0:-1]], axis=0).T
    embp = -(-emb // 8) * 8
    z = jnp.pad(cols(z), ((0, embp - emb), (0, 0)))
    mask = jnp.ones((1, n), F32).at[0, l].set(0.0)
    tm = jnp.concatenate([cols(t), mask], axis=0)
    min_decay = math.log(DECAY_TARGET) / SLOW_DECAY_PCT
    max_decay = math.log(DECAY_TARGET) / FAST_DECAY_PCT
    ad = jnp.abs(jnp.linspace(min_decay, max_decay, c, dtype=F32)).reshape(c, 1)
    w1t = jnp.pad(f_w1.T, ((0, 0), (0, embp - emb)))
    w4t = f_w4.T.reshape(2, c, hid)
    col = lambda a: a.reshape(hid, 1)
    bsz, d = cvec.shape
    n_mod = w_ada.shape[1]
    r = int(round(math.sqrt(n)))
    assert r * r == n
    common = math.gcd(n_mod // LANES, n // LANES)
    fits = lambda s: common % s == 0 and s % 2 == 0 and (n_mod // s) * d * 4 <= ADA_TILE_BYTES
    nf = n // (8 * r)
    nt = next((s for s in range(nf, common + 1, nf) if fits(s)), 0) if nf and nf % 2 == 0 else 0
    folded = nt > 0
    if not folded:
        nt = next((s for s in range(2, common + 1, 2) if fits(s)), common)
        nf = nt
    stride = nt // nf
    tn, tn_mod = n // nf, n_mod // nt
    rows = 8
    cp = jnp.zeros((rows, d), F32).at[:bsz].set(cvec)
    full = lambda shape: pl.BlockSpec(shape, lambda i: (0,) * len(shape))
    if folded:
        kern_spec = pl.BlockSpec((c, 8, r), lambda i: (0, i // stride, 0))
        kern_shape = jax.ShapeDtypeStruct((c, r, r), F32)
    else:
        kern_spec = pl.BlockSpec((c, tn), lambda i: (0, i // stride))
        kern_shape = jax.ShapeDtypeStruct((c, n), F32)
    mod, kern = pl.pallas_call(
        functools.partial(_ada_filt_kernel, stride=stride),
        grid=(nt,),
        in_specs=[full((rows, d)),
                  pl.BlockSpec((d, tn_mod), lambda i: (0, i)),
                  pl.BlockSpec((1, tn_mod), lambda i: (0, i)),
                  pl.BlockSpec((embp, tn), lambda i: (0, i // stride)),
                  pl.BlockSpec((2, tn), lambda i: (0, i // stride)),
                  full((hid, embp)), full((hid, 1)), full((hid, hid)), full((hid, 1)),
                  full((hid, hid)), full((hid, 1)), full((hid, 1)),
                  pl.BlockSpec((1, c, hid), lambda i: ((2 * (i // stride)) // nf, 0, 0)),
                  full((c, 1))],
        out_specs=[pl.BlockSpec((rows, tn_mod), lambda i: (0, i)), kern_spec],
        out_shape=[jax.ShapeDtypeStruct((rows, n_mod), F32), kern_shape],
        compiler_params=_cparams(("arbitrary",)),
        name="ada_filter",
    )(cp, w_ada, b_ada.reshape(1, n_mod), z, tm, w1t, col(f_b1), f_w2.T, col(f_b2), f_w3.T,
      col(f_b3), col(f_freq), w4t, ad)
    return mod[:bsz], kern.reshape(c, r, r)


def _dft_consts(r, tc):
    half = r // 2
    idx = np.arange(r)
    ang = -2.0 * np.pi * np.outer(idx, idx) / r
    fr, fi = np.cos(ang), np.sin(ang)
    angt = -2.0 * np.pi * np.outer(idx, idx) / (r * r)
    fa = np.block([[fr[:, :half], -fi[:, :half]], [fi[:, :half], fr[:, :half]]])
    fk = np.concatenate([fr, fi], axis=0)
    g = np.concatenate([fr, fi], axis=1)
    fin = np.block([[fr[:half], fi[:half]], [-fi[:half], fr[:half]]]) / float(r * r)
    mx = lambda a: jnp.asarray(a, F32).astype(BF16)
    bc = lambda a: jnp.broadcast_to(mx(a)[None], (tc,) + a.shape)
    return (bc(fa), bc(fk), mx(g), jnp.asarray(np.cos(angt), F32),
            jnp.asarray(np.sin(angt), F32), bc(fin))


def _shift_prev(u, lane, row):
    nr, nl = u.shape[-2], u.shape[-1]
    a = pltpu.roll(u, 1, u.ndim - 1)
    b = pltpu.roll(a, 1, u.ndim - 2)
    p = jnp.where(lane == 0, b, a)
    return jnp.where((lane == 0) & (row == 0), 0.0, p)


def _shift_next(u, lane, row):
    nr, nl = u.shape[-2], u.shape[-1]
    a = pltpu.roll(u, nl - 1, u.ndim - 1)
    b = pltpu.roll(a, nr - 1, u.ndim - 2)
    p = jnp.where(lane == nl - 1, b, a)
    return jnp.where((lane == nl - 1) & (row == nr - 1), 0.0, p)


def _lane_stage(y2, g_ref):
    tc, r2, r = y2.shape
    p = jnp.dot(y2.reshape(tc * r2, r).astype(BF16), g_ref[...], preferred_element_type=F32)
    p = p.reshape(tc, r2, r2)
    return p[:, :r, :r], p[:, :r, r:], p[:, r:, :r], p[:, r:, r:]


def _fwd_fft(x2, f1_ref, g_ref, tr, ti):
    r = tr.shape[0]
    a = jnp.einsum("cmk,ckr->cmr", f1_ref[...], x2.astype(BF16), preferred_element_type=F32)
    ar, ai = a[:, :r], a[:, r:]
    br = ar * tr - ai * ti
    bi = ar * ti + ai * tr
    p00, p01, p10, p11 = _lane_stage(jnp.concatenate([br, bi], axis=1), g_ref)
    return p00 - p11, p01 + p10


def _hyena_kernel(x0_ref, x1_ref, v_ref, w0_ref, w1_ref, wv_ref, b0_ref, b1_ref, bv_ref,
                  hb_ref, k_ref, fa_ref, fk_ref, g_ref, tr_ref, ti_ref, fin_ref, o_ref):
    shape = x0_ref.shape[1:]
    lane = lax.broadcasted_iota(jnp.int32, shape, 2)
    row = lax.broadcasted_iota(jnp.int32, shape, 1)
    tr, ti = tr_ref[...], ti_ref[...]

    def sconv(u_ref, w_ref, b_ref, b):
        u = u_ref[b].astype(F32)
        return (b_ref[...] + w_ref[0] * _shift_prev(u, lane, row) + w_ref[1] * u
                + w_ref[2] * _shift_next(u, lane, row))

    nb = x0_ref.shape[0]
    x0 = [sconv(x0_ref, w0_ref, b0_ref, b) for b in range(nb)]
    vx = [sconv(v_ref, wv_ref, bv_ref, b) * sconv(x1_ref, w1_ref, b1_ref, b) for b in range(nb)]

    kr, ki = _fwd_fft(k_ref[...], fk_ref, g_ref, tr, ti)
    for b0 in range(0, nb, 2):
        pair = vx[b0:b0 + 2]
        xi = pair[1] if len(pair) == 2 else jnp.zeros_like(pair[0])
        sr, si = _fwd_fft(jnp.concatenate([pair[0], xi], axis=1), fa_ref, g_ref, tr, ti)
        yr = sr * kr - si * ki
        yi = sr * ki + si * kr
        p00, p01, p10, p11 = _lane_stage(jnp.concatenate([yr, yi], axis=1), g_ref)
        cr, ci = p00 + p11, p10 - p01
        dr = cr * tr + ci * ti
        di = ci * tr - cr * ti
        d2 = jnp.concatenate([dr, di], axis=1).astype(BF16)
        y = jnp.einsum("cmk,ckr->cmr", fin_ref[...], d2, preferred_element_type=F32)
        half = shape[1]
        for j, yb in enumerate((y[:, :half], y[:, half:])[:len(pair)]):
            b = b0 + j
            res = (yb + vx[b] * hb_ref[...]) * x0[b]
            o_ref[b] = res.reshape(res.shape[0], res.shape[1] * res.shape[2]).astype(o_ref.dtype)


def _hyena(u4, k3, conv_w, conv_b, hyena_bias, out_dtype):
    bsz, c3, half, r = u4.shape
    c = c3 // 3
    l = half * r
    assert r == 2 * half and k3.shape == (c, r, r)
    tc = _tile(c, 16)
    nc = c // tc
    consts = _dft_consts(r, tc)
    cw = conv_w.reshape(conv_w.shape[0], c3, 1, 1)
    cb = conv_b.reshape(c3, 1, 1)
    hb = hyena_bias.reshape(c, 1, 1)
    stream = lambda s: pl.BlockSpec((bsz, tc, half, r), lambda j, s=s: (0, j + s * nc, 0, 0))
    wspec = lambda s: pl.BlockSpec((conv_w.shape[0], tc, 1, 1), lambda j, s=s: (0, j + s * nc, 0, 0))
    bspec = lambda s: pl.BlockSpec((tc, 1, 1), lambda j, s=s: (j + s * nc, 0, 0))
    cspec = lambda a: pl.BlockSpec(a.shape, lambda j, nd=a.ndim: (0,) * nd)
    return pl.pallas_call(
        _hyena_kernel,
        grid=(nc,),
        in_specs=[stream(0), stream(1), stream(2), wspec(0), wspec(1), wspec(2),
                  bspec(0), bspec(1), bspec(2), bspec(0),
                  pl.BlockSpec((tc, r, r), lambda j: (j, 0, 0))] + [cspec(a) for a in consts],
        out_specs=pl.BlockSpec((bsz, tc, l), lambda j: (0, j, 0)),
        out_shape=jax.ShapeDtypeStruct((bsz, c, l), out_dtype),
        compiler_params=_cparams(("parallel",)),
        name="hyena",
    )(u4, u4, u4, cw, cw, cw, cb, cb, cb, hb, k3, *consts)


def _rep(x, n):
    if n % LANES:
        return x[:, :n]
    return x if n == LANES else jnp.concatenate([x] * (n // LANES), axis=1)


def _attn_kernel(q_ref, k_ref, v_ref, lq1_ref, lk1_ref, lq2_ref, lk2_ref, g_ref, *rest,
                 ncast, hd, tk, lam_init, eps):
    cast_in, o_ref, cast_out = rest[:ncast], rest[ncast], rest[ncast + 1:2 * ncast + 1]
    s_ref, mb_ref, m_ref, l_ref, acc_ref = rest[2 * ncast + 1:]
    for src, dst in zip(cast_in, cast_out):
        c = src.shape[1]
        dst[:, :c] = src[...].astype(dst.dtype)
        if dst.shape[1] > c:
            dst[:, c:] = jnp.zeros((dst.shape[0], dst.shape[1] - c), dst.dtype)
    nkv = k_ref.shape[1] // tk
    tq = q_ref.shape[1]
    hw = v_ref.shape[2]

    def scores(t, slot, j):
        qj = q_ref[0, :, j * hd:(j + 1) * hd]
        kj = k_ref[0, t * tk:(t + 1) * tk, j * hd:(j + 1) * hd]
        s = lax.dot_general(qj, kj, (((1,), (1,)), ((), ())), preferred_element_type=F32)
        s_ref[slot, j] = s
        mb_ref[slot, j] = jnp.broadcast_to(jnp.max(s, axis=-1, keepdims=True), (tq, LANES))

    def consume(t, slot, j):
        v = v_ref[0, t * tk:(t + 1) * tk, :]
        m_prev = m_ref[j]
        m_new = jnp.maximum(m_prev, mb_ref[slot, j])
        alpha = jnp.exp2(m_prev - m_new)
        ps = [jnp.exp2(s_ref[slot, j, :, c * LANES:(c + 1) * LANES] - m_new)
              for c in range(tk // LANES)]
        l_ref[j] = alpha * l_ref[j] + functools.reduce(lambda a, b: a + b, ps)
        p = jnp.concatenate(ps, axis=1).astype(v.dtype)
        acc_ref[j] = _rep(alpha, hw) * acc_ref[j] + jnp.dot(p, v, preferred_element_type=F32)
        m_ref[j] = m_new

    m_ref[...] = jnp.full(m_ref.shape, -jnp.inf, F32)
    l_ref[...] = jnp.zeros(l_ref.shape, F32)
    acc_ref[...] = jnp.zeros(acc_ref.shape, F32)
    scores(0, 0, 0)
    scores(0, 0, 1)
    for t in range(nkv):
        for j in range(2):
            if t + 1 < nkv:
                scores(t + 1, (t + 1) % 2, j)
            consume(t, t % 2, j)

    lam = (jnp.exp(jnp.sum(lq1_ref[...] * lk1_ref[...], axis=-1, keepdims=True))
           - jnp.exp(jnp.sum(lq2_ref[...] * lk2_ref[...], axis=-1, keepdims=True)) + lam_init)
    l0 = jnp.sum(l_ref[0], axis=-1, keepdims=True)
    l1 = jnp.sum(l_ref[1], axis=-1, keepdims=True)
    o = acc_ref[0] / l0 - lam * (acc_ref[1] / l1)
    o = o * lax.rsqrt(jnp.mean(o * o, axis=-1, keepdims=True) + eps) * g_ref[...]
    o_ref[0] = (o * (1.0 - lam_init)).astype(o_ref.dtype)


def _cast_blocks(shape, steps, full_rows):
    r, c = shape
    for a in range(steps, 0, -1):
        b = steps // a
        if steps % a or r % a or c % b or (full_rows and b > 1):
            continue
        rb, cb = r // a, c // b
        if rb % 16 == 0 and cb % LANES == 0:
            return rb, cb, b
    return None


def _diff_attention(qk, v, lq1, lk1, lq2, lk2, subln_g, lam_init, cast=()):
    bsz, l, width = v.shape
    hd = lq1.shape[-1]
    hw = 2 * hd
    heads = width // hw
    tq, tk = _tile(l, 512), _tile(l, 1024)
    nq = l // tq
    steps = bsz * heads * nq
    vec = lambda a: a.reshape(1, -1).astype(F32)
    vspec = lambda n: pl.BlockSpec((1, n), lambda b, h, i: (0, 0))
    plans = [_cast_blocks(a.shape, steps, pad > 0) for a, pad in cast]
    riders = [(a, pad, p) for (a, pad), p in zip(cast, plans) if p is not None]

    def cast_spec(p, pad):
        rb, cb, ncb = p
        return pl.BlockSpec((rb, cb + pad), lambda b, h, i: (((b * heads + h) * nq + i) // ncb,
                                                              ((b * heads + h) * nq + i) % ncb))

    cast_in_specs = [cast_spec(p, 0) for _, _, p in riders]
    cast_out_specs = [cast_spec(p, pad) for _, pad, p in riders]
    outs = pl.pallas_call(
        functools.partial(_attn_kernel, ncast=len(riders), hd=hd, tk=tk, lam_init=lam_init,
                          eps=SUBLN_EPS),
        grid=(bsz, heads, nq),
        in_specs=[pl.BlockSpec((1, tq, hw), lambda b, h, i: (b, i, h)),
                  pl.BlockSpec((1, l, hw), lambda b, h, i: (b, 0, heads + h)),
                  pl.BlockSpec((1, l, hw), lambda b, h, i: (b, 0, h)),
                  vspec(hd), vspec(hd), vspec(hd), vspec(hd), vspec(hw)] + cast_in_specs,
        out_specs=[pl.BlockSpec((1, tq, hw), lambda b, h, i: (b, i, h))] + cast_out_specs,
        out_shape=[jax.ShapeDtypeStruct((bsz, l, width), BF16)]
                  + [jax.ShapeDtypeStruct((a.shape[0], a.shape[1] + pad), BF16)
                     for a, pad, _ in riders],
        scratch_shapes=[pltpu.VMEM((2, 2, tq, tk), F32), pltpu.VMEM((2, 2, tq, LANES), F32),
                        pltpu.VMEM((2, tq, LANES), F32), pltpu.VMEM((2, tq, LANES), F32),
                        pltpu.VMEM((2, tq, hw), F32)],
        compiler_params=_cparams(("parallel", "parallel", "arbitrary")),
        name="diff_attn",
    )(qk, qk, v, vec(lq1), vec(lk1), vec(lq2), vec(lk2), vec(subln_g), *[a for a, _, _ in riders])
    rounded = iter(outs[1:])
    return outs[0], [next(rounded) if p is not None else jnp.pad(a.astype(BF16), ((0, 0), (0, pad)))
                     for (a, pad), p in zip(cast, plans)]


def _outproj_kernel(yh_ref, yd_ref, w1_ref, w2_ref, x_ref, gt_ref, o_ref):
    acc = lax.dot_general(yh_ref[0], w1_ref[...].astype(BF16), (((0,), (0,)), ((), ())),
                          preferred_element_type=F32)
    acc = acc + jnp.dot(yd_ref[0], w2_ref[...].astype(BF16), preferred_element_type=F32)
    o_ref[0] = x_ref[0] + gt_ref[0] * acc


def _outproj(yh_t, yd, w, x, gt):
    bsz, l, d = x.shape
    c, kd = yh_t.shape[1], yd.shape[2]
    assert c == kd and w.shape[0] == c + kd
    tm, tn = _tile(l, 1024), _tile(d, _weight_cols(w))
    return pl.pallas_call(
        _outproj_kernel,
        grid=(bsz, l // tm, d // tn),
        in_specs=[pl.BlockSpec((1, c, tm), lambda b, i, j: (b, 0, i)),
                  pl.BlockSpec((1, tm, kd), lambda b, i, j: (b, i, 0)),
                  pl.BlockSpec((c, tn), lambda b, i, j: (0, j)),
                  pl.BlockSpec((kd, tn), lambda b, i, j: (1, j)),
                  pl.BlockSpec((1, tm, tn), lambda b, i, j: (b, i, j)),
                  pl.BlockSpec((1, 1, tn), lambda b, i, j: (b, 0, j))],
        out_specs=pl.BlockSpec((1, tm, tn), lambda b, i, j: (b, i, j)),
        out_shape=jax.ShapeDtypeStruct((bsz, l, d), F32),
        compiler_params=_cparams(("parallel", "parallel", "arbitrary")),
        name="outproj",
    )(yh_t, yd, w, w, x, gt)


def _gateup_kernel(h_ref, wg_ref, wu_ref, *rest, src_blocks):
    if src_blocks:
        src_ref, o_ref, dst_ref = rest
        step = (pl.program_id(0) * pl.num_programs(1) + pl.program_id(1)) * pl.num_programs(2) \
            + pl.program_id(2)

        @pl.when(step < src_blocks)
        def _():
            dst_ref[...] = src_ref[...].astype(dst_ref.dtype)

        @pl.when(step >= src_blocks)
        def _():
            dst_ref[...] = jnp.zeros(dst_ref.shape, dst_ref.dtype)
    else:
        (o_ref,) = rest
    h = h_ref[0]
    g = jnp.dot(h, wg_ref[...], preferred_element_type=F32)
    u = jnp.dot(h, wu_ref[...], preferred_element_type=F32)
    o_ref[0] = (_silu(g) * u).astype(o_ref.dtype)


def _gateup(h, wg, wu, wd):
    bsz, l, d = h.shape
    n = wg.shape[1]
    tm, tn = _tile(l, 1024), _tile(n, 512)
    grid = (bsz, l // tm, n // tn)
    steps = grid[0] * grid[1] * grid[2]
    wspec = pl.BlockSpec((d, tn), lambda b, i, j: (0, j))
    in_specs = [pl.BlockSpec((1, tm, d), lambda b, i, j: (b, i, 0)), wspec, wspec]
    out_specs = [pl.BlockSpec((1, tm, tn), lambda b, i, j: (b, i, j))]
    out_shape = [jax.ShapeDtypeStruct((bsz, l, n), BF16)]
    args = [h, wg, wu]
    rb = n // steps
    rides = n % steps == 0 and rb % 16 == 0 and wd.shape[0] % rb == 0
    src_blocks = wd.shape[0] // rb if rides else 0
    if rides:
        lin = lambda b, i, j: (b * grid[1] + i) * grid[2] + j
        in_specs.append(pl.BlockSpec((rb, wd.shape[1]),
                                     lambda b, i, j: (jnp.minimum(lin(b, i, j), src_blocks - 1), 0)))
        out_specs.append(pl.BlockSpec((rb, wd.shape[1]), lambda b, i, j: (lin(b, i, j), 0)))
        out_shape.append(jax.ShapeDtypeStruct((n, wd.shape[1]), BF16))
        args.append(wd)
    outs = pl.pallas_call(
        functools.partial(_gateup_kernel, src_blocks=src_blocks),
        grid=grid,
        in_specs=in_specs,
        out_specs=out_specs,
        out_shape=out_shape,
        compiler_params=_cparams(("parallel", "parallel", "arbitrary")),
        name="gateup",
    )(*args)
    if rides:
        return outs[0], outs[1]
    return outs[0], jnp.pad(wd.astype(BF16), ((0, n - wd.shape[0]), (0, 0)))


def _down_kernel(g_ref, w_ref, x_ref, gt_ref, gf_ref, o_ref, xs_ref, *, final_eps):
    k = pl.program_id(2)
    d = o_ref.shape[2]
    xc = x_ref.shape[2]
    for c in range(d // xc):
        @pl.when(k == c)
        def _():
            xs_ref[:, c * xc:(c + 1) * xc] = x_ref[0]

    tn = _tile(d, 1024)
    chunks = [slice(c * tn, (c + 1) * tn) for c in range(d // tn)]
    part = lambda cols: jnp.dot(g_ref[0], w_ref[:, cols], preferred_element_type=F32)

    @pl.when(k == 0)
    def _():
        for cols in chunks:
            o_ref[0, :, cols] = part(cols)

    @pl.when(k > 0)
    def _():
        for cols in chunks:
            o_ref[0, :, cols] += part(cols)

    @pl.when(k == pl.num_programs(2) - 1)
    def _():
        def finish(r, carry):
            rows = pl.ds(pl.multiple_of(r * EPILOGUE_ROWS, EPILOGUE_ROWS), EPILOGUE_ROWS)
            x2 = xs_ref[rows, :] + gt_ref[0] * o_ref[0, rows, :]
            if final_eps is not None:
                ms = jnp.mean(x2 * x2, axis=-1, keepdims=True)
                x2 = x2 * lax.rsqrt(ms + final_eps) * gf_ref[...]
            o_ref[0, rows, :] = x2
            return carry

        lax.fori_loop(0, o_ref.shape[1] // EPILOGUE_ROWS, finish, 0)


def _down(g, w, x, gt, g_final, final_eps):
    bsz, l, d = x.shape
    kdim = g.shape[2]
    tm, tk = _tile(l, 512), _tile(kdim, 1024)
    nk = kdim // tk
    nxc = max(s for s in range(1, nk + 1) if d % s == 0 and (d // s) % LANES == 0)
    xc = d // nxc
    return pl.pallas_call(
        functools.partial(_down_kernel, final_eps=final_eps),
        grid=(bsz, l // tm, nk),
        in_specs=[pl.BlockSpec((1, tm, tk), lambda b, i, k: (b, i, k)),
                  pl.BlockSpec((tk, d), lambda b, i, k: (k, 0)),
                  pl.BlockSpec((1, tm, xc), lambda b, i, k: (b, i, jnp.minimum(k, nxc - 1))),
                  pl.BlockSpec((1, 1, d), lambda b, i, k: (b, 0, 0)),
                  pl.BlockSpec((1, d), lambda b, i, k: (0, 0))],
        out_specs=pl.BlockSpec((1, tm, d), lambda b, i, k: (b, i, 0)),
        out_shape=jax.ShapeDtypeStruct((bsz, l, d), F32),
        scratch_shapes=[pltpu.VMEM((tm, d), F32)],
        compiler_params=_cparams(("parallel", "parallel", "arbitrary")),
        name="down",
    )(g, w, x, gt, g_final.reshape(1, d))


def kernel(x, c, positions, w_ada, b_ada, g_mix, g_ffn, w_in, conv_w, conv_b, f_w1, f_b1, f_w2, f_b2, f_w3, f_b3, f_w4, f_freq, hyena_bias, lambda_q1, lambda_k1, lambda_q2, lambda_k2, subln_g, w_out, w_gate, w_up, w_down, g_final):
    bsz, l, d = x.shape
    depth = w_ada.shape[0]
    ch = hyena_bias.shape[-1]
    hd = lambda_q1.shape[-1]
    qk = (w_in.shape[-1] - 3 * ch - (d - ch)) // 2
    cos, sin = _rope_tables(positions, hd)
    q_scale = hd ** -0.5 * math.log2(math.e)
    for i in range(depth):
        lam_init = 0.8 - 0.6 * math.exp(-0.3 * i)
        mod, kern_t = _ada_and_taps(c, w_ada[i], b_ada[i], l, ch, f_w1[i], f_b1[i], f_w2[i], f_b2[i],
                                    f_w3[i], f_b3[i], f_w4[i], f_freq[i])
        sh1, sc1, gt1, sh2, sc2, gt2 = [mod[:, None, j * d:(j + 1) * d] for j in range(N_MOD)]
        h = _norm_mod(x, g_mix[i], sc1, sh1, NORM_EPS)
        w = w_in[i].astype(BF16)
        o1, o3 = 3 * ch, 3 * ch + 2 * qk
        u_t = _proj_t(h, w, 0, o1, kern_t.shape[-1], F32)
        qk_scale = jnp.concatenate([jnp.full((1, qk), q_scale, F32), jnp.ones((1, qk), F32)], axis=1)
        q_k = _proj(h, w, o1, 2 * qk, rope=(cos, sin), col_scale=qk_scale)
        v = _proj(h, w, o3, w.shape[1] - o3)
        y_hy = _hyena(u_t, kern_t, conv_w[i], conv_b[i], hyena_bias[i], BF16)
        hpad = -w_gate.shape[-1] % 1024
        y_da, (wo, wg, wu) = _diff_attention(
            q_k, v, lambda_q1[i], lambda_k1[i], lambda_q2[i], lambda_k2[i], subln_g[i], lam_init,
            cast=((w_out[i], 0), (w_gate[i], hpad), (w_up[i], hpad)))
        x = _outproj(y_hy, y_da, wo, x, gt1)
        h = _norm_mod(x, g_ffn[i], sc2, sh2, NORM_EPS)
        g, wd = _gateup(h, wg, wu, w_down[i])
        last = i == depth - 1
        x = _down(g, wd, x, gt2, g_final, NORM_EPS if last else None)
    return x
```

```python
import functools
import math

import numpy as np
import jax
import jax.numpy as jnp
from jax import lax
from jax.experimental import pallas as pl
from jax.experimental.pallas import tpu as pltpu

F32 = jnp.float32
BF16 = jnp.bfloat16

NORM_EPS = 1e-6
SUBLN_EPS = 1e-5
ROPE_THETA = 10000.0
FAST_DECAY_PCT = 0.3
SLOW_DECAY_PCT = 1.5
DECAY_TARGET = 1e-2
N_MOD = 6

V7X_VMEM_LIMIT_BYTES = 56 * 1024 * 1024
LANES = 128
EPILOGUE_ROWS = 16
ADA_TILE_BYTES = 13 * 512 * 1024


def _cparams(sem):
    return pltpu.CompilerParams(dimension_semantics=sem, vmem_limit_bytes=V7X_VMEM_LIMIT_BYTES)


def _tile(n, pref):
    t = min(n, pref)
    while n % t:
        t -= 1
    return t


def _weight_cols(w):
    return 1024 if w.dtype == BF16 else 512


def _silu(x):
    return x * (1.0 / (1.0 + jnp.exp(-x)))


def _split_bf16(a):
    hi = a.astype(BF16)
    lo = (a - hi.astype(F32)).astype(BF16)
    return hi, lo


def _dot3(a, b):
    ah, al = _split_bf16(a)
    bh, bl = _split_bf16(b)
    d = functools.partial(jnp.dot, preferred_element_type=F32)
    return d(ah, bh) + (d(ah, bl) + d(al, bh))


def _ada_kernel(c_ref, w_ref, b_ref, o_ref):
    o_ref[...] = _dot3(_silu(c_ref[...]), w_ref[...]) + b_ref[...]


def _norm_mod_kernel(x_ref, g_ref, sc_ref, sh_ref, o_ref, *, eps):
    x = x_ref[0]
    y = x * lax.rsqrt(jnp.mean(x * x, axis=-1, keepdims=True) + eps) * g_ref[...]
    o_ref[0] = (y * (1.0 + sc_ref[0]) + sh_ref[0]).astype(o_ref.dtype)


def _norm_mod(x, g, sc, sh, eps):
    bsz, l, d = x.shape
    tr = _tile(l, 512)
    return pl.pallas_call(
        functools.partial(_norm_mod_kernel, eps=eps),
        grid=(bsz, l // tr),
        in_specs=[pl.BlockSpec((1, tr, d), lambda b, i: (b, i, 0)),
                  pl.BlockSpec((1, d), lambda b, i: (0, 0)),
                  pl.BlockSpec((1, 1, d), lambda b, i: (b, 0, 0)),
                  pl.BlockSpec((1, 1, d), lambda b, i: (b, 0, 0))],
        out_specs=pl.BlockSpec((1, tr, d), lambda b, i: (b, i, 0)),
        out_shape=jax.ShapeDtypeStruct((bsz, l, d), BF16),
        compiler_params=_cparams(("parallel", "parallel")),
        name="norm_mod",
    )(x, g.reshape(1, d), sc, sh)


def _rope_kernel(pos_ref, inv_ref, sgn_ref, cos_ref, sin_ref):
    ang = pos_ref[0].astype(F32) * inv_ref[...]
    cos_ref[0] = jnp.cos(ang)
    sin_ref[0] = jnp.sin(ang) * sgn_ref[...]


def _rope_tables(positions, dim):
    bsz, l = positions.shape
    half = dim // 2
    inv = 1.0 / (ROPE_THETA ** (jnp.arange(0, dim, 2, dtype=F32) / dim))
    inv = jnp.concatenate([inv, inv]).reshape(1, dim)
    sgn = jnp.asarray(np.concatenate([-np.ones(half), np.ones(half)]).reshape(1, dim), F32)
    tr = _tile(l, 1024)
    shp = jax.ShapeDtypeStruct((bsz, l, dim), F32)
    return pl.pallas_call(
        _rope_kernel,
        grid=(bsz, l // tr),
        in_specs=[pl.BlockSpec((1, tr, 1), lambda b, i: (b, i, 0)),
                  pl.BlockSpec((1, dim), lambda b, i: (0, 0)),
                  pl.BlockSpec((1, dim), lambda b, i: (0, 0))],
        out_specs=[pl.BlockSpec((1, tr, dim), lambda b, i: (b, i, 0))] * 2,
        out_shape=[shp, shp],
        compiler_params=_cparams(("parallel", "parallel")),
        name="rope_tables",
    )(positions.reshape(bsz, l, 1), inv, sgn)


def _proj_t_kernel(w_ref, h_ref, o_ref):
    res = lax.dot_general(w_ref[...].astype(BF16), h_ref[0], (((0,), (1,)), ((), ())),
                          preferred_element_type=F32)
    o_ref[0] = res.reshape(o_ref.shape[1:]).astype(o_ref.dtype)


def _proj_t(h, w, col0, n, r, out_dtype):
    bsz, l, d = h.shape
    tm, tn = _tile(l, 1024), _tile(math.gcd(n, col0) if col0 else n, _weight_cols(w))
    assert tm % (8 * r) == 0
    j0 = col0 // tn
    return pl.pallas_call(
        _proj_t_kernel,
        grid=(bsz, l // tm, n // tn),
        in_specs=[pl.BlockSpec((d, tn), lambda b, i, j: (0, j + j0)),
                  pl.BlockSpec((1, tm, d), lambda b, i, j: (b, i, 0))],
        out_specs=pl.BlockSpec((1, tn, tm // r, r), lambda b, i, j: (b, j, i, 0)),
        out_shape=jax.ShapeDtypeStruct((bsz, n, l // r, r), out_dtype),
        compiler_params=_cparams(("parallel", "parallel", "arbitrary")),
        name="proj_t",
    )(w, h)


def _proj_kernel(h_ref, w_ref, *rest, rope_dim):
    acc = jnp.dot(h_ref[0], w_ref[...].astype(BF16), preferred_element_type=F32)
    if rope_dim:
        cos_ref, sin_ref, scale_ref, o_ref = rest
        cos, sin = cos_ref[0], sin_ref[0]
        for g in range(acc.shape[1] // rope_dim):
            cols = slice(g * rope_dim, (g + 1) * rope_dim)
            xg = acc[:, cols]
            yg = xg * cos + pltpu.roll(xg, rope_dim // 2, 1) * sin
            o_ref[0, :, cols] = (yg * scale_ref[:, cols]).astype(o_ref.dtype)
    else:
        (o_ref,) = rest
        o_ref[0] = acc.astype(o_ref.dtype)


def _proj(h, w, col0, n, rope=None, col_scale=None):
    bsz, l, d = h.shape
    tm, tn = _tile(l, 1024), _tile(math.gcd(n, col0) if col0 else n, _weight_cols(w))
    j0 = col0 // tn
    in_specs = [pl.BlockSpec((1, tm, d), lambda b, i, j: (b, i, 0)),
                pl.BlockSpec((d, tn), lambda b, i, j: (0, j + j0))]
    args = [h, w]
    rope_dim = 0
    if rope is not None:
        rope_dim = rope[0].shape[-1]
        in_specs += [pl.BlockSpec((1, tm, rope_dim), lambda b, i, j: (b, i, 0))] * 2
        in_specs += [pl.BlockSpec((1, tn), lambda b, i, j: (0, j))]
        args += [*rope, col_scale]
    return pl.pallas_call(
        functools.partial(_proj_kernel, rope_dim=rope_dim),
        grid=(bsz, l // tm, n // tn),
        in_specs=in_specs,
        out_specs=pl.BlockSpec((1, tm, tn), lambda b, i, j: (b, i, j)),
        out_shape=jax.ShapeDtypeStruct((bsz, l, n), BF16),
        compiler_params=_cparams(("parallel", "parallel", "arbitrary")),
        name="proj_rope" if rope_dim else "proj",
    )(*args)


def _filt_kernel(z_ref, t_ref, w1_ref, b1_ref, w2_ref, b2_ref, w3_ref, b3_ref, fr_ref,
                 w4_ref, ad_ref, o_ref):
    fr = fr_ref[...]
    h = jnp.sin(fr * (_dot3(w1_ref[...], z_ref[...]) + b1_ref[...]))
    h = jnp.sin(fr * (_dot3(w2_ref[...], h) + b2_ref[...]))
    h = jnp.sin(fr * (_dot3(w3_ref[...], h) + b3_ref[...]))
    t = t_ref[...]
    decay = jnp.exp(-(ad_ref[...] * t[0:1, :]))
    o_ref[...] = (_dot3(w4_ref[0], h) * decay * t[1:2, :]).reshape(o_ref.shape)


def _ada_filt_kernel(c_ref, wa_ref, ba_ref, *rest, stride):
    mod_ref, kern_ref = rest[-2:]
    _ada_kernel(c_ref, wa_ref, ba_ref, mod_ref)

    @pl.when(pl.program_id(0) % stride == 0)
    def _():
        _filt_kernel(*rest[:-2], kern_ref)


def _ada_and_taps(cvec, w_ada, b_ada, l, c, f_w1, f_b1, f_w2, f_b2, f_w3, f_b3, f_w4, f_freq):
    n = 2 * l
    emb, hid = f_w1.shape
    bands = (emb - 1) // 2
    t = jnp.linspace(0.0, 1.0, l, dtype=F32)[:, None]
    w = 2.0 * math.pi * jnp.arange(l, dtype=F32)[:, None] / l
    f = jnp.linspace(1e-4, bands - 1, bands, dtype=F32)[None, :]
    z = jnp.concatenate([t, jnp.cos(f * w), -jnp.sin(f * w)], axis=-1)
    cols = lambda a: jnp.concatenate([a, a[:1], a[:0:-1]], axis=0).T
    embp = -(-emb // 8) * 8
    z = jnp.pad(cols(z), ((0, embp - emb), (0, 0)))
    mask = jnp.ones((1, n), F32).at[0, l].set(0.0)
    tm = jnp.concatenate([cols(t), mask], axis=0)
    min_decay = math.log(DECAY_TARGET) / SLOW_DECAY_PCT
    max_decay = math.log(DECAY_TARGET) / FAST_DECAY_PCT
    ad = jnp.abs(jnp.linspace(min_decay, max_decay, c, dtype=F32)).reshape(c, 1)
    w1t = jnp.pad(f_w1.T, ((0, 0), (0, embp - emb)))
    w4t = f_w4.T.reshape(2, c, hid)
    col = lambda a: a.reshape(hid, 1)
    bsz, d = cvec.shape
    n_mod = w_ada.shape[1]
    r = int(round(math.sqrt(n)))
    assert r * r == n
    common = math.gcd(n_mod // LANES, n // LANES)
    fits = lambda s: common % s == 0 and s % 2 == 0 and (n_mod // s) * d * 4 <= ADA_TILE_BYTES
    nf = n // (8 * r)
    nt = next((s for s in range(nf, common + 1, nf) if fits(s)), 0) if nf and nf % 2 == 0 else 0
    folded = nt > 0
    if not folded:
        nt = next((s for s in range(2, common + 1, 2) if fits(s)), common)
        nf = nt
    stride = nt // nf
    tn, tn_mod = n // nf, n_mod // nt
    rows = 8
    cp = jnp.zeros((rows, d), F32).at[:bsz].set(cvec)
    full = lambda shape: pl.BlockSpec(shape, lambda i: (0,) * len(shape))
    if folded:
        kern_spec = pl.BlockSpec((c, 8, r), lambda i: (0, i // stride, 0))
        kern_shape = jax.ShapeDtypeStruct((c, r, r), F32)
    else:
        kern_spec = pl.BlockSpec((c, tn), lambda i: (0, i // stride))
        kern_shape = jax.ShapeDtypeStruct((c, n), F32)
    mod, kern = pl.pallas_call(
        functools.partial(_ada_filt_kernel, stride=stride),
        grid=(nt,),
        in_specs=[full((rows, d)),
                  pl.BlockSpec((d, tn_mod), lambda i: (0, i)),
                  pl.BlockSpec((1, tn_mod), lambda i: (0, i)),
                  pl.BlockSpec((embp, tn), lambda i: (0, i // stride)),
                  pl.BlockSpec((2, tn), lambda i: (0, i // stride)),
                  full((hid, embp)), full((hid, 1)), full((hid, hid)), full((hid, 1)),
                  full((hid, hid)), full((hid, 1)), full((hid, 1)),
                  pl.BlockSpec((1, c, hid), lambda i: ((2 * (i // stride)) // nf, 0, 0)),
                  full((c, 1))],
        out_specs=[pl.BlockSpec((rows, tn_mod), lambda i: (0, i)), kern_spec],
        out_shape=[jax.ShapeDtypeStruct((rows, n_mod), F32), kern_shape],
        compiler_params=_cparams(("arbitrary",)),
        name="ada_filter",
    )(cp, w_ada, b_ada.reshape(1, n_mod), z, tm, w1t, col(f_b1), f_w2.T, col(f_b2), f_w3.T,
      col(f_b3), col(f_freq), w4t, ad)
    return mod[:bsz], kern.reshape(c, r, r)


def _dft_consts(r, tc):
    half = r // 2
    idx = np.arange(r)
    ang = -2.0 * np.pi * np.outer(idx, idx) / r
    fr, fi = np.cos(ang), np.sin(ang)
    angt = -2.0 * np.pi * np.outer(idx, idx) / (r * r)
    fa = np.block([[fr[:, :half], -fi[:, :half]], [fi[:, :half], fr[:, :half]]])
    fk = np.concatenate([fr, fi], axis=0)
    g = np.concatenate([fr, fi], axis=1)
    fin = np.block([[fr[:half], fi[:half]], [-fi[:half], fr[:half]]]) / float(r * r)
    mx = lambda a: jnp.asarray(a, F32).astype(BF16)
    bc = lambda a: jnp.broadcast_to(mx(a)[None], (tc,) + a.shape)
    return (bc(fa), bc(fk), mx(g), jnp.asarray(np.cos(angt), F32),
            jnp.asarray(np.sin(angt), F32), bc(fin))


def _shift_prev(u, lane, row):
    nr, nl = u.shape[-2], u.shape[-1]
    a = pltpu.roll(u, 1, u.ndim - 1)
    b = pltpu.roll(a, 1, u.ndim - 2)
    p = jnp.where(lane == 0, b, a)
    return jnp.where((lane == 0) & (row == 0), 0.0, p)


def _shift_next(u, lane, row):
    nr, nl = u.shape[-2], u.shape[-1]
    a = pltpu.roll(u, nl - 1, u.ndim - 1)
    b = pltpu.roll(a, nr - 1, u.ndim - 2)
    p = jnp.where(lane == nl - 1, b, a)
    return jnp.where((lane == nl - 1) & (row == nr - 1), 0.0, p)


def _lane_stage(y2, g_ref):
    tc, r2, r = y2.shape
    p = jnp.dot(y2.reshape(tc * r2, r).astype(BF16), g_ref[...], preferred_element_type=F32)
    p = p.reshape(tc, r2, r2)
    return p[:, :r, :r], p[:, :r, r:], p[:, r:, :r], p[:, r:, r:]


def _fwd_fft(x2, f1_ref, g_ref, tr, ti):
    r = tr.shape[0]
    a = jnp.einsum("cmk,ckr->cmr", f1_ref[...], x2.astype(BF16), preferred_element_type=F32)
    ar, ai = a[:, :r], a[:, r:]
    br = ar * tr - ai * ti
    bi = ar * ti + ai * tr
    p00, p01, p10, p11 = _lane_stage(jnp.concatenate([br, bi], axis=1), g_ref)
    return p00 - p11, p01 + p10


def _hyena_kernel(x0_ref, x1_ref, v_ref, w0_ref, w1_ref, wv_ref, b0_ref, b1_ref, bv_ref,
                  hb_ref, k_ref, fa_ref, fk_ref, g_ref, tr_ref, ti_ref, fin_ref, o_ref):
    shape = x0_ref.shape[1:]
    lane = lax.broadcasted_iota(jnp.int32, shape, 2)
    row = lax.broadcasted_iota(jnp.int32, shape, 1)
    tr, ti = tr_ref[...], ti_ref[...]

    def sconv(u_ref, w_ref, b_ref, b):
        u = u_ref[b].astype(F32)
        return (b_ref[...] + w_ref[0] * _shift_prev(u, lane, row) + w_ref[1] * u
                + w_ref[2] * _shift_next(u, lane, row))

    nb = x0_ref.shape[0]
    x0 = [sconv(x0_ref, w0_ref, b0_ref, b) for b in range(nb)]
    vx = [sconv(v_ref, wv_ref, bv_ref, b) * sconv(x1_ref, w1_ref, b1_ref, b) for b in range(nb)]

    kr, ki = _fwd_fft(k_ref[...], fk_ref, g_ref, tr, ti)
    for b0 in range(0, nb, 2):
        pair = vx[b0:b0 + 2]
        xi = pair[1] if len(pair) == 2 else jnp.zeros_like(pair[0])
        sr, si = _fwd_fft(jnp.concatenate([pair[0], xi], axis=1), fa_ref, g_ref, tr, ti)
        yr = sr * kr - si * ki
        yi = sr * ki + si * kr
        p00, p01, p10, p11 = _lane_stage(jnp.concatenate([yr, yi], axis=1), g_ref)
        cr, ci = p00 + p11, p10 - p01
        dr = cr * tr + ci * ti
        di = ci * tr - cr * ti
        d2 = jnp.concatenate([dr, di], axis=1).astype(BF16)
        y = jnp.einsum("cmk,ckr->cmr", fin_ref[...], d2, preferred_element_type=F32)
        half = shape[1]
        for j, yb in enumerate((y[:, :half], y[:, half:])[:len(pair)]):
            b = b0 + j
            res = (yb + vx[b] * hb_ref[...]) * x0[b]
            o_ref[b] = res.reshape(res.shape[0], res.shape[1] * res.shape[2]).astype(o_ref.dtype)


def _hyena(u4, k3, conv_w, conv_b, hyena_bias, out_dtype):
    bsz, c3, half, r = u4.shape
    c = c3 // 3
    l = half * r
    assert r == 2 * half and k3.shape == (c, r, r)
    tc = _tile(c, 16)
    nc = c // tc
    consts = _dft_consts(r, tc)
    cw = conv_w.reshape(conv_w.shape[0], c3, 1, 1)
    cb = conv_b.reshape(c3, 1, 1)
    hb = hyena_bias.reshape(c, 1, 1)
    stream = lambda s: pl.BlockSpec((bsz, tc, half, r), lambda j, s=s: (0, j + s * nc, 0, 0))
    wspec = lambda s: pl.BlockSpec((conv_w.shape[0], tc, 1, 1), lambda j, s=s: (0, j + s * nc, 0, 0))
    bspec = lambda s: pl.BlockSpec((tc, 1, 1), lambda j, s=s: (j + s * nc, 0, 0))
    cspec = lambda a: pl.BlockSpec(a.shape, lambda j, nd=a.ndim: (0,) * nd)
    return pl.pallas_call(
        _hyena_kernel,
        grid=(nc,),
        in_specs=[stream(0), stream(1), stream(2), wspec(0), wspec(1), wspec(2),
                  bspec(0), bspec(1), bspec(2), bspec(0),
                  pl.BlockSpec((tc, r, r), lambda j: (j, 0, 0))] + [cspec(a) for a in consts],
        out_specs=pl.BlockSpec((bsz, tc, l), lambda j: (0, j, 0)),
        out_shape=jax.ShapeDtypeStruct((bsz, c, l), out_dtype),
        compiler_params=_cparams(("parallel",)),
        name="hyena",
    )(u4, u4, u4, cw, cw, cw, cb, cb, cb, hb, k3, *consts)


def _rep(x, n):
    if n % LANES:
        return x[:, :n]
    return x if n == LANES else jnp.concatenate([x] * (n // LANES), axis=1)


def _attn_kernel(q_ref, k_ref, v_ref, lq1_ref, lk1_ref, lq2_ref, lk2_ref, g_ref, *rest,
                 ncast, hd, tk, lam_init, eps):
    cast_in, o_ref, cast_out = rest[:ncast], rest[ncast], rest[ncast + 1:2 * ncast + 1]
    s_ref, mb_ref, m_ref, l_ref, acc_ref = rest[2 * ncast + 1:]
    for src, dst in zip(cast_in, cast_out):
        c = src.shape[1]
        dst[:, :c] = src[...].astype(dst.dtype)
        if dst.shape[1] > c:
            dst[:, c:] = jnp.zeros((dst.shape[0], dst.shape[1] - c), dst.dtype)
    nkv = k_ref.shape[1] // tk
    tq = q_ref.shape[1]
    hw = v_ref.shape[2]

    def scores(t, slot, j):
        qj = q_ref[0, :, j * hd:(j + 1) * hd]
        kj = k_ref[0, t * tk:(t + 1) * tk, j * hd:(j + 1) * hd]
        s = lax.dot_general(qj, kj, (((1,), (1,)), ((), ())), preferred_element_type=F32)
        s_ref[slot, j] = s
        mb_ref[slot, j] = jnp.broadcast_to(jnp.max(s, axis=-1, keepdims=True), (tq, LANES))

    def consume(t, slot, j):
        v = v_ref[0, t * tk:(t + 1) * tk, :]
        m_prev = m_ref[j]
        m_new = jnp.maximum(m_prev, mb_ref[slot, j])
        alpha = jnp.exp2(m_prev - m_new)
        ps = [jnp.exp2(s_ref[slot, j, :, c * LANES:(c + 1) * LANES] - m_new)
              for c in range(tk // LANES)]
        l_ref[j] = alpha * l_ref[j] + functools.reduce(lambda a, b: a + b, ps)
        p = jnp.concatenate(ps, axis=1).astype(v.dtype)
        acc_ref[j] = _rep(alpha, hw) * acc_ref[j] + jnp.dot(p, v, preferred_element_type=F32)
        m_ref[j] = m_new

    m_ref[...] = jnp.full(m_ref.shape, -jnp.inf, F32)
    l_ref[...] = jnp.zeros(l_ref.shape, F32)
    acc_ref[...] = jnp.zeros(acc_ref.shape, F32)
    scores(0, 0, 0)
    scores(0, 0, 1)
    for t in range(nkv):
        for j in range(2):
            if t + 1 < nkv:
                scores(t + 1, (t + 1) % 2, j)
            consume(t, t % 2, j)

    lam = (jnp.exp(jnp.sum(lq1_ref[...] * lk1_ref[...], axis=-1, keepdims=True))
           - jnp.exp(jnp.sum(lq2_ref[...] * lk2_ref[...], axis=-1, keepdims=True)) + lam_init)
    l0 = jnp.sum(l_ref[0], axis=-1, keepdims=True)
    l1 = jnp.sum(l_ref[1], axis=-1, keepdims=True)
    o = acc_ref[0] / l0 - lam * (acc_ref[1] / l1)
    o = o * lax.rsqrt(jnp.mean(o * o, axis=-1, keepdims=True) + eps) * g_ref[...]
    o_ref[0] = (o * (1.0 - lam_init)).astype(o_ref.dtype)


def _cast_blocks(shape, steps, full_rows):
    r, c = shape
    for a in range(steps, 0, -1):
        b = steps // a
        if steps % a or r % a or c % b or (full_rows and b > 1):
            continue
        rb, cb = r // a, c // b
        if rb % 16 == 0 and cb % LANES == 0:
            return rb, cb, b
    return None


def _diff_attention(qk, v, lq1, lk1, lq2, lk2, subln_g, lam_init, cast=()):
    bsz, l, width = v.shape
    hd = lq1.shape[-1]
    hw = 2 * hd
    heads = width // hw
    tq, tk = _tile(l, 512), _tile(l, 1024)
    nq = l // tq
    steps = bsz * heads * nq
    vec = lambda a: a.reshape(1, -1).astype(F32)
    vspec = lambda n: pl.BlockSpec((1, n), lambda b, h, i: (0, 0))
    plans = [_cast_blocks(a.shape, steps, pad > 0) for a, pad in cast]
    riders = [(a, pad, p) for (a, pad), p in zip(cast, plans) if p is not None]

    def cast_spec(p, pad):
        rb, cb, ncb = p
        return pl.BlockSpec((rb, cb + pad), lambda b, h, i: (((b * heads + h) * nq + i) // ncb,
                                                              ((b * heads + h) * nq + i) % ncb))

    cast_in_specs = [cast_spec(p, 0) for _, _, p in riders]
    cast_out_specs = [cast_spec(p, pad) for _, pad, p in riders]
    outs = pl.pallas_call(
        functools.partial(_attn_kernel, ncast=len(riders), hd=hd, tk=tk, lam_init=lam_init,
                          eps=SUBLN_EPS),
        grid=(bsz, heads, nq),
        in_specs=[pl.BlockSpec((1, tq, hw), lambda b, h, i: (b, i, h)),
                  pl.BlockSpec((1, l, hw), lambda b, h, i: (b, 0, heads + h)),
                  pl.BlockSpec((1, l, hw), lambda b, h, i: (b, 0, h)),
                  vspec(hd), vspec(hd), vspec(hd), vspec(hd), vspec(hw)] + cast_in_specs,
        out_specs=[pl.BlockSpec((1, tq, hw), lambda b, h, i: (b, i, h))] + cast_out_specs,
        out_shape=[jax.ShapeDtypeStruct((bsz, l, width), BF16)]
                  + [jax.ShapeDtypeStruct((a.shape[0], a.shape[1] + pad), BF16)
                     for a, pad, _ in riders],
        scratch_shapes=[pltpu.VMEM((2, 2, tq, tk), F32), pltpu.VMEM((2, 2, tq, LANES), F32),
                        pltpu.VMEM((2, tq, LANES), F32), pltpu.VMEM((2, tq, LANES), F32),
                        pltpu.VMEM((2, tq, hw), F32)],
        compiler_params=_cparams(("parallel", "parallel", "arbitrary")),
        name="diff_attn",
    )(qk, qk, v, vec(lq1), vec(lk1), vec(lq2), vec(lk2), vec(subln_g), *[a for a, _, _ in riders])
    rounded = iter(outs[1:])
    return outs[0], [next(rounded) if p is not None else jnp.pad(a.astype(BF16), ((0, 0), (0, pad)))
                     for (a, pad), p in zip(cast, plans)]


def _outproj_kernel(yh_ref, yd_ref, w1_ref, w2_ref, x_ref, gt_ref, o_ref):
    acc = lax.dot_general(yh_ref[0], w1_ref[...].astype(BF16), (((0,), (0,)), ((), ())),
                          preferred_element_type=F32)
    acc = acc + jnp.dot(yd_ref[0], w2_ref[...].astype(BF16), preferred_element_type=F32)
    o_ref[0] = x_ref[0] + gt_ref[0] * acc


def _outproj(yh_t, yd, w, x, gt):
    bsz, l, d = x.shape
    c, kd = yh_t.shape[1], yd.shape[2]
    assert c == kd and w.shape[0] == c + kd
    tm, tn = _tile(l, 1024), _tile(d, _weight_cols(w))
    return pl.pallas_call(
        _outproj_kernel,
        grid=(bsz, l // tm, d // tn),
        in_specs=[pl.BlockSpec((1, c, tm), lambda b, i, j: (b, 0, i)),
                  pl.BlockSpec((1, tm, kd), lambda b, i, j: (b, i, 0)),
                  pl.BlockSpec((c, tn), lambda b, i, j: (0, j)),
                  pl.BlockSpec((kd, tn), lambda b, i, j: (1, j)),
                  pl.BlockSpec((1, tm, tn), lambda b, i, j: (b, i, j)),
                  pl.BlockSpec((1, 1, tn), lambda b, i, j: (b, 0, j))],
        out_specs=pl.BlockSpec((1, tm, tn), lambda b, i, j: (b, i, j)),
        out_shape=jax.ShapeDtypeStruct((bsz, l, d), F32),
        compiler_params=_cparams(("parallel", "parallel", "arbitrary")),
        name="outproj",
    )(yh_t, yd, w, w, x, gt)


def _gateup_kernel(h_ref, wg_ref, wu_ref, *rest, src_blocks):
    if src_blocks:
        src_ref, o_ref, dst_ref = rest
        step = (pl.program_id(0) * pl.num_programs(1) + pl.program_id(1)) * pl.num_programs(2) \
            + pl.program_id(2)

        @pl.when(step < src_blocks)
        def _():
            dst_ref[...] = src_ref[...].astype(dst_ref.dtype)

        @pl.when(step >= src_blocks)
        def _():
            dst_ref[...] = jnp.zeros(dst_ref.shape, dst_ref.dtype)
    else:
        (o_ref,) = rest
    h = h_ref[0]
    g = jnp.dot(h, wg_ref[...], preferred_element_type=F32)
    u = jnp.dot(h, wu_ref[...], preferred_element_type=F32)
    o_ref[0] = (_silu(g) * u).astype(o_ref.dtype)


def _gateup(h, wg, wu, wd):
    bsz, l, d = h.shape
    n = wg.shape[1]
    tm, tn = _tile(l, 1024), _tile(n, 512)
    grid = (bsz, l // tm, n // tn)
    steps = grid[0] * grid[1] * grid[2]
    wspec = pl.BlockSpec((d, tn), lambda b, i, j: (0, j))
    in_specs = [pl.BlockSpec((1, tm, d), lambda b, i, j: (b, i, 0)), wspec, wspec]
    out_specs = [pl.BlockSpec((1, tm, tn), lambda b, i, j: (b, i, j))]
    out_shape = [jax.ShapeDtypeStruct((bsz, l, n), BF16)]
    args = [h, wg, wu]
    rb = n // steps
    rides = n % steps == 0 and rb % 16 == 0 and wd.shape[0] % rb == 0
    src_blocks = wd.shape[0] // rb if rides else 0
    if rides:
        lin = lambda b, i, j: (b * grid[1] + i) * grid[2] + j
        in_specs.append(pl.BlockSpec((rb, wd.shape[1]),
                                     lambda b, i, j: (jnp.minimum(lin(b, i, j), src_blocks - 1), 0)))
        out_specs.append(pl.BlockSpec((rb, wd.shape[1]), lambda b, i, j: (lin(b, i, j), 0)))
        out_shape.append(jax.ShapeDtypeStruct((n, wd.shape[1]), BF16))
        args.append(wd)
    outs = pl.pallas_call(
        functools.partial(_gateup_kernel, src_blocks=src_blocks),
        grid=grid,
        in_specs=in_specs,
        out_specs=out_specs,
        out_shape=out_shape,
        compiler_params=_cparams(("parallel", "parallel", "arbitrary")),
        name="gateup",
    )(*args)
    if rides:
        return outs[0], outs[1]
    return outs[0], jnp.pad(wd.astype(BF16), ((0, n - wd.shape[0]), (0, 0)))


def _down_kernel(g_ref, w_ref, x_ref, gt_ref, gf_ref, o_ref, *, final_eps):
    k = pl.program_id(2)
    d = o_ref.shape[2]
    tn = _tile(d, 1024)
    chunks = [slice(c * tn, (c + 1) * tn) for c in range(d // tn)]
    part = lambda cols: jnp.dot(g_ref[0], w_ref[:, cols], preferred_element_type=F32)

    @pl.when(k == 0)
    def _():
        for cols in chunks:
            o_ref[0, :, cols] = part(cols)

    @pl.when(k > 0)
    def _():
        for cols in chunks:
            o_ref[0, :, cols] += part(cols)

    @pl.when(k == pl.num_programs(2) - 1)
    def _():
        def finish(r, carry):
            rows = pl.ds(pl.multiple_of(r * EPILOGUE_ROWS, EPILOGUE_ROWS), EPILOGUE_ROWS)
            x2 = x_ref[0, rows, :] + gt_ref[0] * o_ref[0, rows, :]
            if final_eps is not None:
                ms = jnp.mean(x2 * x2, axis=-1, keepdims=True)
                x2 = x2 * lax.rsqrt(ms + final_eps) * gf_ref[...]
            o_ref[0, rows, :] = x2
            return carry

        lax.fori_loop(0, o_ref.shape[1] // EPILOGUE_ROWS, finish, 0)


def _down(g, w, x, gt, g_final, final_eps):
    bsz, l, d = x.shape
    kdim = g.shape[2]
    tm, tk = _tile(l, 512), _tile(kdim, 1024)
    return pl.pallas_call(
        functools.partial(_down_kernel, final_eps=final_eps),
        grid=(bsz, l // tm, kdim // tk),
        in_specs=[pl.BlockSpec((1, tm, tk), lambda b, i, k: (b, i, k)),
                  pl.BlockSpec((tk, d), lambda b, i, k: (k, 0)),
                  pl.BlockSpec((1, tm, d), lambda b, i, k: (b, i, 0)),
                  pl.BlockSpec((1, 1, d), lambda b, i, k: (b, 0, 0)),
                  pl.BlockSpec((1, d), lambda b, i, k: (0, 0))],
        out_specs=pl.BlockSpec((1, tm, d), lambda b, i, k: (b, i, 0)),
        out_shape=jax.ShapeDtypeStruct((bsz, l, d), F32),
        compiler_params=_cparams(("parallel", "parallel", "arbitrary")),
        name="down",
    )(g, w, x, gt, g_final.reshape(1, d))


def kernel(x, c, positions, w_ada, b_ada, g_mix, g_ffn, w_in, conv_w, conv_b, f_w1, f_b1, f_w2, f_b2, f_w3, f_b3, f_w4, f_freq, hyena_bias, lambda_q1, lambda_k1, lambda_q2, lambda_k2, subln_g, w_out, w_gate, w_up, w_down, g_final):
    bsz, l, d = x.shape
    depth = w_ada.shape[0]
    ch = hyena_bias.shape[-1]
    hd = lambda_q1.shape[-1]
    qk = (w_in.shape[-1] - 3 * ch - (d - ch)) // 2
    cos, sin = _rope_tables(positions, hd)
    q_scale = hd ** -0.5 * math.log2(math.e)
    for i in range(depth):
        lam_init = 0.8 - 0.6 * math.exp(-0.3 * i)
        mod, kern_t = _ada_and_taps(c, w_ada[i], b_ada[i], l, ch, f_w1[i], f_b1[i], f_w2[i], f_b2[i],
                                    f_w3[i], f_b3[i], f_w4[i], f_freq[i])
        sh1, sc1, gt1, sh2, sc2, gt2 = [mod[:, None, j * d:(j + 1) * d] for j in range(N_MOD)]
        h = _norm_mod(x, g_mix[i], sc1, sh1, NORM_EPS)
        w = w_in[i].astype(BF16)
        o1, o3 = 3 * ch, 3 * ch + 2 * qk
        u_t = _proj_t(h, w, 0, o1, kern_t.shape[-1], F32)
        qk_scale = jnp.concatenate([jnp.full((1, qk), q_scale, F32), jnp.ones((1, qk), F32)], axis=1)
        q_k = _proj(h, w, o1, 2 * qk, rope=(cos, sin), col_scale=qk_scale)
        v = _proj(h, w, o3, w.shape[1] - o3)
        y_hy = _hyena(u_t, kern_t, conv_w[i], conv_b[i], hyena_bias[i], BF16)
        hpad = -w_gate.shape[-1] % 1024
        y_da, (wo, wg, wu) = _diff_attention(
            q_k, v, lambda_q1[i], lambda_k1[i], lambda_q2[i], lambda_k2[i], subln_g[i], lam_init,
            cast=((w_out[i], 0), (w_gate[i], hpad), (w_up[i], hpad)))
        x = _outproj(y_hy, y_da, wo, x, gt1)
        h = _norm_mod(x, g_ffn[i], sc2, sh2, NORM_EPS)
        g, wd = _gateup(h, wg, wu, w_down[i])
        last = i == depth - 1
        x = _down(g, wd, x, gt2, g_final, NORM_EPS if last else None)
    return x
```

```python
import functools
import math

import numpy as np
import jax
import jax.numpy as jnp
from jax import lax
from jax.experimental import pallas as pl
from jax.experimental.pallas import tpu as pltpu

F32 = jnp.float32
BF16 = jnp.bfloat16

NORM_EPS = 1e-6
SUBLN_EPS = 1e-5
ROPE_THETA = 10000.0
FAST_DECAY_PCT = 0.3
SLOW_DECAY_PCT = 1.5
DECAY_TARGET = 1e-2
N_MOD = 6

V7X_VMEM_LIMIT_BYTES = 56 * 1024 * 1024
LANES = 128
EPILOGUE_ROWS = 16
ADA_TILE_BYTES = 13 * 512 * 1024


def _cparams(sem):
    return pltpu.CompilerParams(dimension_semantics=sem, vmem_limit_bytes=V7X_VMEM_LIMIT_BYTES)


def _tile(n, pref):
    t = min(n, pref)
    while n % t:
        t -= 1
    return t


def _weight_cols(w):
    return 1024 if w.dtype == BF16 else 512


def _silu(x):
    return x * (1.0 / (1.0 + jnp.exp(-x)))


def _split_bf16(a):
    hi = a.astype(BF16)
    lo = (a - hi.astype(F32)).astype(BF16)
    return hi, lo


def _dot3(a, b):
    ah, al = _split_bf16(a)
    bh, bl = _split_bf16(b)
    d = functools.partial(jnp.dot, preferred_element_type=F32)
    return d(ah, bh) + (d(ah, bl) + d(al, bh))


def _ada_kernel(c_ref, w_ref, b_ref, o_ref):
    o_ref[...] = _dot3(_silu(c_ref[...]), w_ref[...]) + b_ref[...]


def _norm_mod_kernel(x_ref, g_ref, sc_ref, sh_ref, o_ref, *, eps):
    x = x_ref[0]
    y = x * lax.rsqrt(jnp.mean(x * x, axis=-1, keepdims=True) + eps) * g_ref[...]
    o_ref[0] = (y * (1.0 + sc_ref[0]) + sh_ref[0]).astype(o_ref.dtype)


def _norm_mod(x, g, sc, sh, eps):
    bsz, l, d = x.shape
    tr = _tile(l, 512)
    return pl.pallas_call(
        functools.partial(_norm_mod_kernel, eps=eps),
        grid=(bsz, l // tr),
        in_specs=[pl.BlockSpec((1, tr, d), lambda b, i: (b, i, 0)),
                  pl.BlockSpec((1, d), lambda b, i: (0, 0)),
                  pl.BlockSpec((1, 1, d), lambda b, i: (b, 0, 0)),
                  pl.BlockSpec((1, 1, d), lambda b, i: (b, 0, 0))],
        out_specs=pl.BlockSpec((1, tr, d), lambda b, i: (b, i, 0)),
        out_shape=jax.ShapeDtypeStruct((bsz, l, d), BF16),
        compiler_params=_cparams(("parallel", "parallel")),
        name="norm_mod",
    )(x, g.reshape(1, d), sc, sh)


def _rope_kernel(pos_ref, inv_ref, sgn_ref, cos_ref, sin_ref):
    ang = pos_ref[0].astype(F32) * inv_ref[...]
    cos_ref[0] = jnp.cos(ang)
    sin_ref[0] = jnp.sin(ang) * sgn_ref[...]


def _rope_tables(positions, dim):
    bsz, l = positions.shape
    half = dim // 2
    inv = 1.0 / (ROPE_THETA ** (jnp.arange(0, dim, 2, dtype=F32) / dim))
    inv = jnp.concatenate([inv, inv]).reshape(1, dim)
    sgn = jnp.asarray(np.concatenate([-np.ones(half), np.ones(half)]).reshape(1, dim), F32)
    tr = _tile(l, 1024)
    shp = jax.ShapeDtypeStruct((bsz, l, dim), F32)
    return pl.pallas_call(
        _rope_kernel,
        grid=(bsz, l // tr),
        in_specs=[pl.BlockSpec((1, tr, 1), lambda b, i: (b, i, 0)),
                  pl.BlockSpec((1, dim), lambda b, i: (0, 0)),
                  pl.BlockSpec((1, dim), lambda b, i: (0, 0))],
        out_specs=[pl.BlockSpec((1, tr, dim), lambda b, i: (b, i, 0))] * 2,
        out_shape=[shp, shp],
        compiler_params=_cparams(("parallel", "parallel")),
        name="rope_tables",
    )(positions.reshape(bsz, l, 1), inv, sgn)


def _proj_t_kernel(w_ref, h_ref, o_ref):
    res = lax.dot_general(w_ref[...].astype(BF16), h_ref[0], (((0,), (1,)), ((), ())),
                          preferred_element_type=F32)
    o_ref[0] = res.reshape(o_ref.shape[1:]).astype(o_ref.dtype)


def _proj_t(h, w, col0, n, r, out_dtype):
    bsz, l, d = h.shape
    tm, tn = _tile(l, 1024), _tile(math.gcd(n, col0) if col0 else n, _weight_cols(w))
    assert tm % (8 * r) == 0
    j0 = col0 // tn
    return pl.pallas_call(
        _proj_t_kernel,
        grid=(bsz, l // tm, n // tn),
        in_specs=[pl.BlockSpec((d, tn), lambda b, i, j: (0, j + j0)),
                  pl.BlockSpec((1, tm, d), lambda b, i, j: (b, i, 0))],
        out_specs=pl.BlockSpec((1, tn, tm // r, r), lambda b, i, j: (b, j, i, 0)),
        out_shape=jax.ShapeDtypeStruct((bsz, n, l // r, r), out_dtype),
        compiler_params=_cparams(("parallel", "parallel", "arbitrary")),
        name="proj_t",
    )(w, h)


def _proj_kernel(h_ref, w_ref, *rest, rope_dim):
    acc = jnp.dot(h_ref[0], w_ref[...].astype(BF16), preferred_element_type=F32)
    if rope_dim:
        cos_ref, sin_ref, scale_ref, o_ref = rest
        cos, sin = cos_ref[0], sin_ref[0]
        for g in range(acc.shape[1] // rope_dim):
            cols = slice(g * rope_dim, (g + 1) * rope_dim)
            xg = acc[:, cols]
            yg = xg * cos + pltpu.roll(xg, rope_dim // 2, 1) * sin
            o_ref[0, :, cols] = (yg * scale_ref[:, cols]).astype(o_ref.dtype)
    else:
        (o_ref,) = rest
        o_ref[0] = acc.astype(o_ref.dtype)


def _proj(h, w, col0, n, rope=None, col_scale=None):
    bsz, l, d = h.shape
    tm, tn = _tile(l, 1024), _tile(math.gcd(n, col0) if col0 else n, _weight_cols(w))
    j0 = col0 // tn
    in_specs = [pl.BlockSpec((1, tm, d), lambda b, i, j: (b, i, 0)),
                pl.BlockSpec((d, tn), lambda b, i, j: (0, j + j0))]
    args = [h, w]
    rope_dim = 0
    if rope is not None:
        rope_dim = rope[0].shape[-1]
        in_specs += [pl.BlockSpec((1, tm, rope_dim), lambda b, i, j: (b, i, 0))] * 2
        in_specs += [pl.BlockSpec((1, tn), lambda b, i, j: (0, j))]
        args += [*rope, col_scale]
    return pl.pallas_call(
        functools.partial(_proj_kernel, rope_dim=rope_dim),
        grid=(bsz, l // tm, n // tn),
        in_specs=in_specs,
        out_specs=pl.BlockSpec((1, tm, tn), lambda b, i, j: (b, i, j)),
        out_shape=jax.ShapeDtypeStruct((bsz, l, n), BF16),
        compiler_params=_cparams(("parallel", "parallel", "arbitrary")),
        name="proj_rope" if rope_dim else "proj",
    )(*args)


def _filt_kernel(z_ref, t_ref, w1_ref, b1_ref, w2_ref, b2_ref, w3_ref, b3_ref, fr_ref,
                 w4_ref, ad_ref, o_ref):
    fr = fr_ref[...]
    h = jnp.sin(fr * (_dot3(w1_ref[...], z_ref[...]) + b1_ref[...]))
    h = jnp.sin(fr * (_dot3(w2_ref[...], h) + b2_ref[...]))
    h = jnp.sin(fr * (_dot3(w3_ref[...], h) + b3_ref[...]))
    t = t_ref[...]
    decay = jnp.exp(-(ad_ref[...] * t[0:1, :]))
    o_ref[...] = (_dot3(w4_ref[0], h) * decay * t[1:2, :]).reshape(o_ref.shape)


def _ada_filt_kernel(c_ref, wa_ref, ba_ref, *rest, stride):
    mod_ref, kern_ref = rest[-2:]
    _ada_kernel(c_ref, wa_ref, ba_ref, mod_ref)

    @pl.when(pl.program_id(0) % stride == 0)
    def _():
        _filt_kernel(*rest[:-2], kern_ref)


def _ada_and_taps(cvec, w_ada, b_ada, l, c, f_w1, f_b1, f_w2, f_b2, f_w3, f_b3, f_w4, f_freq):
    n = 2 * l
    emb, hid = f_w1.shape
    bands = (emb - 1) // 2
    t = jnp.linspace(0.0, 1.0, l, dtype=F32)[:, None]
    w = 2.0 * math.pi * jnp.arange(l, dtype=F32)[:, None] / l
    f = jnp.linspace(1e-4, bands - 1, bands, dtype=F32)[None, :]
    z = jnp.concatenate([t, jnp.cos(f * w), -jnp.sin(f * w)], axis=-1)
    cols = lambda a: jnp.concatenate([a, a[:1], a[:0:-1]], axis=0).T
    embp = -(-emb // 8) * 8
    z = jnp.pad(cols(z), ((0, embp - emb), (0, 0)))
    mask = jnp.ones((1, n), F32).at[0, l].set(0.0)
    tm = jnp.concatenate([cols(t), mask], axis=0)
    min_decay = math.log(DECAY_TARGET) / SLOW_DECAY_PCT
    max_decay = math.log(DECAY_TARGET) / FAST_DECAY_PCT
    ad = jnp.abs(jnp.linspace(min_decay, max_decay, c, dtype=F32)).reshape(c, 1)
    w1t = jnp.pad(f_w1.T, ((0, 0), (0, embp - emb)))
    w4t = f_w4.T.reshape(2, c, hid)
    col = lambda a: a.reshape(hid, 1)
    bsz, d = cvec.shape
    n_mod = w_ada.shape[1]
    r = int(round(math.sqrt(n)))
    assert r * r == n
    common = math.gcd(n_mod // LANES, n // LANES)
    fits = lambda s: common % s == 0 and s % 2 == 0 and (n_mod // s) * d * 4 <= ADA_TILE_BYTES
    nf = n // (8 * r)
    nt = next((s for s in range(nf, common + 1, nf) if fits(s)), 0) if nf and nf % 2 == 0 else 0
    folded = nt > 0
    if not folded:
        nt = next((s for s in range(2, common + 1, 2) if fits(s)), common)
        nf = nt
    stride = nt // nf
    tn, tn_mod = n // nf, n_mod // nt
    rows = 8
    cp = jnp.zeros((rows, d), F32).at[:bsz].set(cvec)
    full = lambda shape: pl.BlockSpec(shape, lambda i: (0,) * len(shape))
    if folded:
        kern_spec = pl.BlockSpec((c, 8, r), lambda i: (0, i // stride, 0))
        kern_shape = jax.ShapeDtypeStruct((c, r, r), F32)
    else:
        kern_spec = pl.BlockSpec((c, tn), lambda i: (0, i // stride))
        kern_shape = jax.ShapeDtypeStruct((c, n), F32)
    mod, kern = pl.pallas_call(
        functools.partial(_ada_filt_kernel, stride=stride),
        grid=(nt,),
        in_specs=[full((rows, d)),
                  pl.BlockSpec((d, tn_mod), lambda i: (0, i)),
                  pl.BlockSpec((1, tn_mod), lambda i: (0, i)),
                  pl.BlockSpec((embp, tn), lambda i: (0, i // stride)),
                  pl.BlockSpec((2, tn), lambda i: (0, i // stride)),
                  full((hid, embp)), full((hid, 1)), full((hid, hid)), full((hid, 1)),
                  full((hid, hid)), full((hid, 1)), full((hid, 1)),
                  pl.BlockSpec((1, c, hid), lambda i: ((2 * (i // stride)) // nf, 0, 0)),
                  full((c, 1))],
        out_specs=[pl.BlockSpec((rows, tn_mod), lambda i: (0, i)), kern_spec],
        out_shape=[jax.ShapeDtypeStruct((rows, n_mod), F32), kern_shape],
        compiler_params=_cparams(("arbitrary",)),
        name="ada_filter",
    )(cp, w_ada, b_ada.reshape(1, n_mod), z, tm, w1t, col(f_b1), f_w2.T, col(f_b2), f_w3.T,
      col(f_b3), col(f_freq), w4t, ad)
    return mod[:bsz], kern.reshape(c, r, r)


def _dft_consts(r, tc):
    half = r // 2
    idx = np.arange(r)
    ang = -2.0 * np.pi * np.outer(idx, idx) / r
    fr, fi = np.cos(ang), np.sin(ang)
    angt = -2.0 * np.pi * np.outer(idx, idx) / (r * r)
    fa = np.block([[fr[:, :half], -fi[:, :half]], [fi[:, :half], fr[:, :half]]])
    fk = np.concatenate([fr, fi], axis=0)
    g = np.concatenate([fr, fi], axis=1)
    fin = np.block([[fr[:half], fi[:half]], [-fi[:half], fr[:half]]]) / float(r * r)
    mx = lambda a: jnp.asarray(a, F32).astype(BF16)
    bc = lambda a: jnp.broadcast_to(mx(a)[None], (tc,) + a.shape)
    return (bc(fa), bc(fk), mx(g), jnp.asarray(np.cos(angt), F32),
            jnp.asarray(np.sin(angt), F32), bc(fin))


def _shift_prev(u, lane, row):
    nr, nl = u.shape[-2], u.shape[-1]
    a = pltpu.roll(u, 1, u.ndim - 1)
    b = pltpu.roll(a, 1, u.ndim - 2)
    p = jnp.where(lane == 0, b, a)
    return jnp.where((lane == 0) & (row == 0), 0.0, p)


def _shift_next(u, lane, row):
    nr, nl = u.shape[-2], u.shape[-1]
    a = pltpu.roll(u, nl - 1, u.ndim - 1)
    b = pltpu.roll(a, nr - 1, u.ndim - 2)
    p = jnp.where(lane == nl - 1, b, a)
    return jnp.where((lane == nl - 1) & (row == nr - 1), 0.0, p)


def _lane_stage(y2, g_ref):
    tc, r2, r = y2.shape
    p = jnp.dot(y2.reshape(tc * r2, r).astype(BF16), g_ref[...], preferred_element_type=F32)
    p = p.reshape(tc, r2, r2)
    return p[:, :r, :r], p[:, :r, r:], p[:, r:, :r], p[:, r:, r:]


def _fwd_fft(x2, f1_ref, g_ref, tr, ti):
    r = tr.shape[0]
    a = jnp.einsum("cmk,ckr->cmr", f1_ref[...], x2.astype(BF16), preferred_element_type=F32)
    ar, ai = a[:, :r], a[:, r:]
    br = ar * tr - ai * ti
    bi = ar * ti + ai * tr
    p00, p01, p10, p11 = _lane_stage(jnp.concatenate([br, bi], axis=1), g_ref)
    return p00 - p11, p01 + p10


def _hyena_kernel(x0_ref, x1_ref, v_ref, w0_ref, w1_ref, wv_ref, b0_ref, b1_ref, bv_ref,
                  hb_ref, k_ref, fa_ref, fk_ref, g_ref, tr_ref, ti_ref, fin_ref, o_ref):
    shape = x0_ref.shape[1:]
    lane = lax.broadcasted_iota(jnp.int32, shape, 2)
    row = lax.broadcasted_iota(jnp.int32, shape, 1)
    tr, ti = tr_ref[...], ti_ref[...]

    def sconv(u_ref, w_ref, b_ref, b):
        u = u_ref[b].astype(F32)
        return (b_ref[...] + w_ref[0] * _shift_prev(u, lane, row) + w_ref[1] * u
                + w_ref[2] * _shift_next(u, lane, row))

    nb = x0_ref.shape[0]
    x0 = [sconv(x0_ref, w0_ref, b0_ref, b) for b in range(nb)]
    vx = [sconv(v_ref, wv_ref, bv_ref, b) * sconv(x1_ref, w1_ref, b1_ref, b) for b in range(nb)]

    kr, ki = _fwd_fft(k_ref[...], fk_ref, g_ref, tr, ti)
    for b0 in range(0, nb, 2):
        pair = vx[b0:b0 + 2]
        xi = pair[1] if len(pair) == 2 else jnp.zeros_like(pair[0])
        sr, si = _fwd_fft(jnp.concatenate([pair[0], xi], axis=1), fa_ref, g_ref, tr, ti)
        yr = sr * kr - si * ki
        yi = sr * ki + si * kr
        p00, p01, p10, p11 = _lane_stage(jnp.concatenate([yr, yi], axis=1), g_ref)
        cr, ci = p00 + p11, p10 - p01
        dr = cr * tr + ci * ti
        di = ci * tr - cr * ti
        d2 = jnp.concatenate([dr, di], axis=1).astype(BF16)
        y = jnp.einsum("cmk,ckr->cmr", fin_ref[...], d2, preferred_element_type=F32)
        half = shape[1]
        for j, yb in enumerate((y[:, :half], y[:, half:])[:len(pair)]):
            b = b0 + j
            res = (yb + vx[b] * hb_ref[...]) * x0[b]
            o_ref[b] = res.reshape(res.shape[0], res.shape[1] * res.shape[2]).astype(o_ref.dtype)


def _hyena(u4, k3, conv_w, conv_b, hyena_bias, out_dtype):
    bsz, c3, half, r = u4.shape
    c = c3 // 3
    l = half * r
    assert r == 2 * half and k3.shape == (c, r, r)
    tc = _tile(c, 16)
    nc = c // tc
    consts = _dft_consts(r, tc)
    cw = conv_w.reshape(conv_w.shape[0], c3, 1, 1)
    cb = conv_b.reshape(c3, 1, 1)
    hb = hyena_bias.reshape(c, 1, 1)
    stream = lambda s: pl.BlockSpec((bsz, tc, half, r), lambda j, s=s: (0, j + s * nc, 0, 0))
    wspec = lambda s: pl.BlockSpec((conv_w.shape[0], tc, 1, 1), lambda j, s=s: (0, j + s * nc, 0, 0))
    bspec = lambda s: pl.BlockSpec((tc, 1, 1), lambda j, s=s: (j + s * nc, 0, 0))
    cspec = lambda a: pl.BlockSpec(a.shape, lambda j, nd=a.ndim: (0,) * nd)
    return pl.pallas_call(
        _hyena_kernel,
        grid=(nc,),
        in_specs=[stream(0), stream(1), stream(2), wspec(0), wspec(1), wspec(2),
                  bspec(0), bspec(1), bspec(2), bspec(0),
                  pl.BlockSpec((tc, r, r), lambda j: (j, 0, 0))] + [cspec(a) for a in consts],
        out_specs=pl.BlockSpec((bsz, tc, l), lambda j: (0, j, 0)),
        out_shape=jax.ShapeDtypeStruct((bsz, c, l), out_dtype),
        compiler_params=_cparams(("parallel",)),
        name="hyena",
    )(u4, u4, u4, cw, cw, cw, cb, cb, cb, hb, k3, *consts)


def _rep(x, n):
    if n % LANES:
        return x[:, :n]
    return x if n == LANES else jnp.concatenate([x] * (n // LANES), axis=1)


def _attn_kernel(q_ref, k_ref, v_ref, lq1_ref, lk1_ref, lq2_ref, lk2_ref, g_ref, *rest,
                 ncast, hd, tk, lam_init, eps):
    cast_in, o_ref, cast_out = rest[:ncast], rest[ncast], rest[ncast + 1:2 * ncast + 1]
    s_ref, mb_ref, m_ref, l_ref, acc_ref = rest[2 * ncast + 1:]
    for src, dst in zip(cast_in, cast_out):
        c = src.shape[1]
        dst[:, :c] = src[...].astype(dst.dtype)
        if dst.shape[1] > c:
            dst[:, c:] = jnp.zeros((dst.shape[0], dst.shape[1] - c), dst.dtype)
    nkv = k_ref.shape[1] // tk
    tq = q_ref.shape[1]
    hw = v_ref.shape[2]

    def scores(t, slot, j):
        qj = q_ref[0, :, j * hd:(j + 1) * hd]
        kj = k_ref[0, t * tk:(t + 1) * tk, j * hd:(j + 1) * hd]
        s = lax.dot_general(qj, kj, (((1,), (1,)), ((), ())), preferred_element_type=F32)
        s_ref[slot, j] = s
        mb_ref[slot, j] = jnp.broadcast_to(jnp.max(s, axis=-1, keepdims=True), (tq, LANES))

    def consume(t, slot, j):
        v = v_ref[0, t * tk:(t + 1) * tk, :]
        m_prev = m_ref[j]
        m_new = jnp.maximum(m_prev, mb_ref[slot, j])
        alpha = jnp.exp2(m_prev - m_new)
        ps = [jnp.exp2(s_ref[slot, j, :, c * LANES:(c + 1) * LANES] - m_new)
              for c in range(tk // LANES)]
        l_ref[j] = alpha * l_ref[j] + functools.reduce(lambda a, b: a + b, ps)
        p = jnp.concatenate(ps, axis=1).astype(v.dtype)
        acc_ref[j] = _rep(alpha, hw) * acc_ref[j] + jnp.dot(p, v, preferred_element_type=F32)
        m_ref[j] = m_new

    m_ref[...] = jnp.full(m_ref.shape, -jnp.inf, F32)
    l_ref[...] = jnp.zeros(l_ref.shape, F32)
    acc_ref[...] = jnp.zeros(acc_ref.shape, F32)
    scores(0, 0, 0)
    scores(0, 0, 1)
    for t in range(nkv):
        for j in range(2):
            if t + 1 < nkv:
                scores(t + 1, (t + 1) % 2, j)
            consume(t, t % 2, j)

    lam = (jnp.exp(jnp.sum(lq1_ref[...] * lk1_ref[...], axis=-1, keepdims=True))
           - jnp.exp(jnp.sum(lq2_ref[...] * lk2_ref[...], axis=-1, keepdims=True)) + lam_init)
    l0 = jnp.sum(l_ref[0], axis=-1, keepdims=True)
    l1 = jnp.sum(l_ref[1], axis=-1, keepdims=True)
    o = acc_ref[0] / l0 - lam * (acc_ref[1] / l1)
    o = o * lax.rsqrt(jnp.mean(o * o, axis=-1, keepdims=True) + eps) * g_ref[...]
    o_ref[0] = (o * (1.0 - lam_init)).astype(o_ref.dtype)


def _cast_blocks(shape, steps, full_rows):
    r, c = shape
    for a in range(steps, 0, -1):
        b = steps // a
        if steps % a or r % a or c % b or (full_rows and b > 1):
            continue
        rb, cb = r // a, c // b
        if rb % 16 == 0 and cb % LANES == 0:
            return rb, cb, b
    return None


def _diff_attention(qk, v, lq1, lk1, lq2, lk2, subln_g, lam_init, cast=()):
    bsz, l, width = v.shape
    hd = lq1.shape[-1]
    hw = 2 * hd
    heads = width // hw
    tq, tk = _tile(l, 512), _tile(l, 1024)
    nq = l // tq
    steps = bsz * heads * nq
    vec = lambda a: a.reshape(1, -1).astype(F32)
    vspec = lambda n: pl.BlockSpec((1, n), lambda b, h, i: (0, 0))
    plans = [_cast_blocks(a.shape, steps, pad > 0) for a, pad in cast]
    riders = [(a, pad, p) for (a, pad), p in zip(cast, plans) if p is not None]

    def cast_spec(p, pad):
        rb, cb, ncb = p
        return pl.BlockSpec((rb, cb + pad), lambda b, h, i: (((b * heads + h) * nq + i) // ncb,
                                                              ((b * heads + h) * nq + i) % ncb))

    cast_in_specs = [cast_spec(p, 0) for _, _, p in riders]
    cast_out_specs = [cast_spec(p, pad) for _, pad, p in riders]
    outs = pl.pallas_call(
        functools.partial(_attn_kernel, ncast=len(riders), hd=hd, tk=tk, lam_init=lam_init,
                          eps=SUBLN_EPS),
        grid=(bsz, heads, nq),
        in_specs=[pl.BlockSpec((1, tq, hw), lambda b, h, i: (b, i, h)),
                  pl.BlockSpec((1, l, hw), lambda b, h, i: (b, 0, heads + h)),
                  pl.BlockSpec((1, l, hw), lambda b, h, i: (b, 0, h)),
                  vspec(hd), vspec(hd), vspec(hd), vspec(hd), vspec(hw)] + cast_in_specs,
        out_specs=[pl.BlockSpec((1, tq, hw), lambda b, h, i: (b, i, h))] + cast_out_specs,
        out_shape=[jax.ShapeDtypeStruct((bsz, l, width), BF16)]
                  + [jax.ShapeDtypeStruct((a.shape[0], a.shape[1] + pad), BF16)
                     for a, pad, _ in riders],
        scratch_shapes=[pltpu.VMEM((2, 2, tq, tk), F32), pltpu.VMEM((2, 2, tq, LANES), F32),
                        pltpu.VMEM((2, tq, LANES), F32), pltpu.VMEM((2, tq, LANES), F32),
                        pltpu.VMEM((2, tq, hw), F32)],
        compiler_params=_cparams(("parallel", "parallel", "arbitrary")),
        name="diff_attn",
    )(qk, qk, v, vec(lq1), vec(lk1), vec(lq2), vec(lk2), vec(subln_g), *[a for a, _, _ in riders])
    rounded = iter(outs[1:])
    return outs[0], [next(rounded) if p is not None else jnp.pad(a.astype(BF16), ((0, 0), (0, pad)))
                     for (a, pad), p in zip(cast, plans)]


def _outproj_kernel(yh_ref, yd_ref, w1_ref, w2_ref, x_ref, gt_ref, o_ref):
    acc = lax.dot_general(yh_ref[0], w1_ref[...].astype(BF16), (((0,), (0,)), ((), ())),
                          preferred_element_type=F32)
    acc = acc + jnp.dot(yd_ref[0], w2_ref[...].astype(BF16), preferred_element_type=F32)
    o_ref[0] = x_ref[0] + gt_ref[0] * acc


def _outproj(yh_t, yd, w, x, gt):
    bsz, l, d = x.shape
    c, kd = yh_t.shape[1], yd.shape[2]
    assert c == kd and w.shape[0] == c + kd
    tm, tn = _tile(l, 1024), _tile(d, _weight_cols(w))
    return pl.pallas_call(
        _outproj_kernel,
        grid=(bsz, l // tm, d // tn),
        in_specs=[pl.BlockSpec((1, c, tm), lambda b, i, j: (b, 0, i)),
                  pl.BlockSpec((1, tm, kd), lambda b, i, j: (b, i, 0)),
                  pl.BlockSpec((c, tn), lambda b, i, j: (0, j)),
                  pl.BlockSpec((kd, tn), lambda b, i, j: (1, j)),
                  pl.BlockSpec((1, tm, tn), lambda b, i, j: (b, i, j)),
                  pl.BlockSpec((1, 1, tn), lambda b, i, j: (b, 0, j))],
        out_specs=pl.BlockSpec((1, tm, tn), lambda b, i, j: (b, i, j)),
        out_shape=jax.ShapeDtypeStruct((bsz, l, d), F32),
        compiler_params=_cparams(("parallel", "parallel", "arbitrary")),
        name="outproj",
    )(yh_t, yd, w, w, x, gt)


def _gateup_kernel(h_ref, wg_ref, wu_ref, *rest, src_blocks):
    if src_blocks:
        src_ref, o_ref, dst_ref = rest
        step = (pl.program_id(0) * pl.num_programs(1) + pl.program_id(1)) * pl.num_programs(2) \
            + pl.program_id(2)

        @pl.when(step < src_blocks)
        def _():
            dst_ref[...] = src_ref[...].astype(dst_ref.dtype)

        @pl.when(step >= src_blocks)
        def _():
            dst_ref[...] = jnp.zeros(dst_ref.shape, dst_ref.dtype)
    else:
        (o_ref,) = rest
    h = h_ref[0]
    g = jnp.dot(h, wg_ref[...], preferred_element_type=F32)
    u = jnp.dot(h, wu_ref[...], preferred_element_type=F32)
    o_ref[0] = (_silu(g) * u).astype(o_ref.dtype)


def _gateup(h, wg, wu, wd):
    bsz, l, d = h.shape
    n = wg.shape[1]
    tm, tn = _tile(l, 2048), _tile(n, 256)
    grid = (bsz, l // tm, n // tn)
    steps = grid[0] * grid[1] * grid[2]
    wspec = pl.BlockSpec((d, tn), lambda b, i, j: (0, j))
    in_specs = [pl.BlockSpec((1, tm, d), lambda b, i, j: (b, i, 0)), wspec, wspec]
    out_specs = [pl.BlockSpec((1, tm, tn), lambda b, i, j: (b, i, j))]
    out_shape = [jax.ShapeDtypeStruct((bsz, l, n), BF16)]
    args = [h, wg, wu]
    rb = n // steps
    rides = n % steps == 0 and rb % 16 == 0 and wd.shape[0] % rb == 0
    src_blocks = wd.shape[0] // rb if rides else 0
    if rides:
        lin = lambda b, i, j: (b * grid[1] + i) * grid[2] + j
        in_specs.append(pl.BlockSpec((rb, wd.shape[1]),
                                     lambda b, i, j: (jnp.minimum(lin(b, i, j), src_blocks - 1), 0)))
        out_specs.append(pl.BlockSpec((rb, wd.shape[1]), lambda b, i, j: (lin(b, i, j), 0)))
        out_shape.append(jax.ShapeDtypeStruct((n, wd.shape[1]), BF16))
        args.append(wd)
    outs = pl.pallas_call(
        functools.partial(_gateup_kernel, src_blocks=src_blocks),
        grid=grid,
        in_specs=in_specs,
        out_specs=out_specs,
        out_shape=out_shape,
        compiler_params=_cparams(("parallel", "parallel", "arbitrary")),
        name="gateup",
    )(*args)
    if rides:
        return outs[0], outs[1]
    return outs[0], jnp.pad(wd.astype(BF16), ((0, n - wd.shape[0]), (0, 0)))


def _down_kernel(g_ref, w_ref, x_ref, gt_ref, gf_ref, o_ref, *, final_eps):
    k = pl.program_id(2)
    d = o_ref.shape[2]
    tn = _tile(d, 1024)
    chunks = [slice(c * tn, (c + 1) * tn) for c in range(d // tn)]
    part = lambda cols: jnp.dot(g_ref[0], w_ref[:, cols], preferred_element_type=F32)

    @pl.when(k == 0)
    def _():
        for cols in chunks:
            o_ref[0, :, cols] = part(cols)

    @pl.when(k > 0)
    def _():
        for cols in chunks:
            o_ref[0, :, cols] += part(cols)

    @pl.when(k == pl.num_programs(2) - 1)
    def _():
        def finish(r, carry):
            rows = pl.ds(pl.multiple_of(r * EPILOGUE_ROWS, EPILOGUE_ROWS), EPILOGUE_ROWS)
            x2 = x_ref[0, rows, :] + gt_ref[0] * o_ref[0, rows, :]
            if final_eps is not None:
                ms = jnp.mean(x2 * x2, axis=-1, keepdims=True)
                x2 = x2 * lax.rsqrt(ms + final_eps) * gf_ref[...]
            o_ref[0, rows, :] = x2
            return carry

        lax.fori_loop(0, o_ref.shape[1] // EPILOGUE_ROWS, finish, 0)


def _down(g, w, x, gt, g_final, final_eps):
    bsz, l, d = x.shape
    kdim = g.shape[2]
    tm, tk = _tile(l, 512), _tile(kdim, 1024)
    return pl.pallas_call(
        functools.partial(_down_kernel, final_eps=final_eps),
        grid=(bsz, l // tm, kdim // tk),
        in_specs=[pl.BlockSpec((1, tm, tk), lambda b, i, k: (b, i, k)),
                  pl.BlockSpec((tk, d), lambda b, i, k: (k, 0)),
                  pl.BlockSpec((1, tm, d), lambda b, i, k: (b, i, 0)),
                  pl.BlockSpec((1, 1, d), lambda b, i, k: (b, 0, 0)),
                  pl.BlockSpec((1, d), lambda b, i, k: (0, 0))],
        out_specs=pl.BlockSpec((1, tm, d), lambda b, i, k: (b, i, 0)),
        out_shape=jax.ShapeDtypeStruct((bsz, l, d), F32),
        compiler_params=_cparams(("parallel", "parallel", "arbitrary")),
        name="down",
    )(g, w, x, gt, g_final.reshape(1, d))


def kernel(x, c, positions, w_ada, b_ada, g_mix, g_ffn, w_in, conv_w, conv_b, f_w1, f_b1, f_w2, f_b2, f_w3, f_b3, f_w4, f_freq, hyena_bias, lambda_q1, lambda_k1, lambda_q2, lambda_k2, subln_g, w_out, w_gate, w_up, w_down, g_final):
    bsz, l, d = x.shape
    depth = w_ada.shape[0]
    ch = hyena_bias.shape[-1]
    hd = lambda_q1.shape[-1]
    qk = (w_in.shape[-1] - 3 * ch - (d - ch)) // 2
    cos, sin = _rope_tables(positions, hd)
    q_scale = hd ** -0.5 * math.log2(math.e)
    for i in range(depth):
        lam_init = 0.8 - 0.6 * math.exp(-0.3 * i)
        mod, kern_t = _ada_and_taps(c, w_ada[i], b_ada[i], l, ch, f_w1[i], f_b1[i], f_w2[i], f_b2[i],
                                    f_w3[i], f_b3[i], f_w4[i], f_freq[i])
        sh1, sc1, gt1, sh2, sc2, gt2 = [mod[:, None, j * d:(j + 1) * d] for j in range(N_MOD)]
        h = _norm_mod(x, g_mix[i], sc1, sh1, NORM_EPS)
        w = w_in[i].astype(BF16)
        o1, o3 = 3 * ch, 3 * ch + 2 * qk
        u_t = _proj_t(h, w, 0, o1, kern_t.shape[-1], F32)
        qk_scale = jnp.concatenate([jnp.full((1, qk), q_scale, F32), jnp.ones((1, qk), F32)], axis=1)
        q_k = _proj(h, w, o1, 2 * qk, rope=(cos, sin), col_scale=qk_scale)
        v = _proj(h, w, o3, w.shape[1] - o3)
        y_hy = _hyena(u_t, kern_t, conv_w[i], conv_b[i], hyena_bias[i], BF16)
        hpad = -w_gate.shape[-1] % 1024
        y_da, (wo, wg, wu) = _diff_attention(
            q_k, v, lambda_q1[i], lambda_k1[i], lambda_q2[i], lambda_k2[i], subln_g[i], lam_init,
            cast=((w_out[i], 0), (w_gate[i], hpad), (w_up[i], hpad)))
        x = _outproj(y_hy, y_da, wo, x, gt1)
        h = _norm_mod(x, g_ffn[i], sc2, sh2, NORM_EPS)
        g, wd = _gateup(h, wg, wu, w_down[i])
        last = i == depth - 1
        x = _down(g, wd, x, gt2, g_final, NORM_EPS if last else None)
    return x
```
